```python
import jax, jax.numpy as jnp
from jax import lax
import numpy as np

D_MODEL = 1024
BATCH = 4
SEQ = 4096
DEPTH = 4
DEC_BATCH = 32
DEC_SEQ = 1
PAST_LEN = 8192
PAGE_SIZE = 128

HEAD_DIM = 64
ROT_DIM = HEAD_DIM // 4
ROPE_THETA = 500000.0
ATTN_SCALE = HEAD_DIM ** -0.5
A_HEADS = 12
A_KV_HEADS = 4
IDX_HEADS = 8
IDX_DIM = 64
IDX_SCALE = (IDX_HEADS * IDX_DIM) ** -0.5
TOPK_MAX = 256
B_GROUPS = ((128, 1), (512, 4), (2048, 16))
B_HEADS = 4
MEM_LEN = 256
MEM_HEADS = 4
D_FF = 2816
EPS = 1e-6
Q_BLOCK = 128
N_A = (DEPTH + 1) // 2
N_B = DEPTH // 2
A_IN = (A_HEADS + 2 * A_KV_HEADS + MEM_HEADS) * HEAD_DIM + IDX_HEADS * IDX_DIM + IDX_DIM + IDX_HEADS
A_OUT = (A_HEADS + MEM_HEADS) * HEAD_DIM
B_IN = (len(B_GROUPS) * 3 * B_HEADS + MEM_HEADS) * HEAD_DIM
B_OUT = (B_HEADS + MEM_HEADS) * HEAD_DIM

kernel_name = 'dsa_dilated_hybrid_decoder_step'


def rmsnorm(x, g):
    xf = x.astype(jnp.float32)
    y = xf * lax.rsqrt(jnp.mean(xf * xf, axis=-1, keepdims=True) + EPS)
    return (y * g.astype(jnp.float32)).astype(x.dtype)


def rope(x, pos):
    half = ROT_DIM // 2
    inv = jnp.power(ROPE_THETA, -jnp.arange(half, dtype=jnp.float32) / half)
    ang = pos.astype(jnp.float32)[:, None] * inv[None, :]
    shp = (pos.shape[0],) + (1,) * (x.ndim - 3) + (half,)
    cos, sin = jnp.cos(ang).reshape(shp), jnp.sin(ang).reshape(shp)
    xr = x[..., :ROT_DIM].astype(jnp.float32)
    x1, x2 = xr[..., :half], xr[..., half:]
    rot = jnp.concatenate([x1 * cos - x2 * sin, x2 * cos + x1 * sin], axis=-1).astype(x.dtype)
    return jnp.concatenate([rot, x[..., ROT_DIM:]], axis=-1)


def swiglu(x, w_in, w_out):
    g, u = jnp.split(x @ w_in, 2, axis=-1)
    return (jax.nn.silu(g) * u) @ w_out


def memory_kv(mem, g, w):
    b = mem.shape[0]
    return (rmsnorm(mem, g) @ w).reshape(b, MEM_LEN, 2, MEM_HEADS, HEAD_DIM)


def mem_attention(q, mem_kv):
    s = jnp.einsum('bthd,bmhd->bhtm', q, mem_kv[:, :, 0]).astype(jnp.float32) * ATTN_SCALE
    p = jax.nn.softmax(s, axis=-1).astype(mem_kv.dtype)
    o = jnp.einsum('bhtm,bmhd->bthd', p, mem_kv[:, :, 1])
    return o.reshape(q.shape[0], q.shape[1], MEM_HEADS * HEAD_DIM)


def prep_a(p, pos):
    b, t = p.shape[:2]
    sizes = (A_HEADS * HEAD_DIM, A_KV_HEADS * HEAD_DIM, A_KV_HEADS * HEAD_DIM,
             IDX_HEADS * IDX_DIM, IDX_DIM, IDX_HEADS)
    q, k, v, iq, ik, iw, mq = jnp.split(p, np.cumsum(sizes).tolist(), axis=-1)
    q = rope(q.reshape(b, t, A_HEADS, HEAD_DIM), pos)
    k = rope(k.reshape(b, t, A_KV_HEADS, HEAD_DIM), pos)
    v = v.reshape(b, t, A_KV_HEADS, HEAD_DIM)
    iq = rope(iq.reshape(b, t, IDX_HEADS, IDX_DIM), pos)
    ik = rope(ik[:, :, None, :], pos)[:, :, 0, :]
    iw = iw.astype(jnp.float32) * IDX_SCALE
    kv = jnp.stack([k, v], axis=2)
    return q, kv, iq, ik, iw, mq.reshape(b, t, MEM_HEADS, HEAD_DIM)


def indexer_scores(iq, ik, iw, qpos, kpos):
    rel = jax.nn.relu(jnp.einsum('bthd,bsd->bths', iq, ik).astype(jnp.float32))
    sc = jnp.einsum('bths,bth->bts', rel, iw)
    return jnp.where(kpos[None, None, :] <= qpos[None, :, None], sc, -jnp.inf)


def gqa_sparse_attend(q, kv_sel, valid):
    b, t, h, dh = q.shape
    qg = q.reshape(b, t, A_KV_HEADS, h // A_KV_HEADS, dh)
    s = jnp.einsum('btngd,btknd->btngk', qg, kv_sel[:, :, :, 0]).astype(jnp.float32) * ATTN_SCALE
    s = jnp.where(valid[:, :, None, None, :], s, -jnp.inf)
    pr = jax.nn.softmax(s, axis=-1).astype(kv_sel.dtype)
    o = jnp.einsum('btngk,btknd->btngd', pr, kv_sel[:, :, :, 1])
    return o.reshape(b, t, h * dh)


def mixer_a_prompt(p):
    b, s = p.shape[:2]
    pos = jnp.arange(s)
    q, kv, iq, ik, iw, mq = prep_a(p, pos)
    n_sel = min(TOPK_MAX, s // 4)
    take = jax.vmap(lambda a, ix: a[ix])

    def block(t0):
        sl = lambda a: lax.dynamic_slice_in_dim(a, t0, Q_BLOCK, axis=1)
        sc = indexer_scores(sl(iq), ik, sl(iw), t0 + jnp.arange(Q_BLOCK), pos)
        vals, idx = lax.top_k(sc, n_sel)
        return gqa_sparse_attend(sl(q), take(kv, idx), jnp.isfinite(vals))

    o = lax.map(block, jnp.arange(s // Q_BLOCK) * Q_BLOCK)
    o = jnp.moveaxis(o, 0, 1).reshape(b, s, A_HEADS * HEAD_DIM)
    return o, mq, kv, ik


def mixer_a_sample(p, kv_pool, kidx_pool, page_table):
    b, t = p.shape[:2]
    pos = PAST_LEN + jnp.arange(t)
    q, kv, iq, ik, iw, mq = prep_a(p, pos)
    n_len = PAST_LEN + t
    n_sel = min(TOPK_MAX, n_len // 4)
    ik_past = kidx_pool[page_table].reshape(b, PAST_LEN, IDX_DIM)
    ik_all = jnp.concatenate([ik_past, ik], axis=1)
    sc = indexer_scores(iq, ik_all, iw, pos, jnp.arange(n_len))
    vals, idx = lax.top_k(sc, n_sel)
    ip = jnp.minimum(idx, PAST_LEN - 1)
    phys = jax.vmap(lambda pt, lp: pt[lp])(page_table, ip // PAGE_SIZE)
    rows = phys * PAGE_SIZE + ip % PAGE_SIZE
    kv_past = kv_pool.reshape((-1,) + kv_pool.shape[2:])[rows]
    kv_new = jax.vmap(lambda a, ix: a[ix])(kv, jnp.clip(idx - PAST_LEN, 0, t - 1))
    kv_sel = jnp.where((idx < PAST_LEN)[..., None, None, None], kv_past, kv_new)
    o = gqa_sparse_attend(q, kv_sel, jnp.isfinite(vals))
    return o, mq, kv, ik


def prep_b(p, pos):
    b, t = p.shape[:2]
    nqkv = len(B_GROUPS) * 3 * B_HEADS * HEAD_DIM
    qkv = p[..., :nqkv].reshape(b, t, len(B_GROUPS), 3, B_HEADS, HEAD_DIM)
    mq = p[..., nqkv:].reshape(b, t, MEM_HEADS, HEAD_DIM)
    q = rope(qkv[:, :, :, 0], pos)
    kv = jnp.stack([rope(qkv[:, :, :, 1], pos), qkv[:, :, :, 2]], axis=3)
    return q, kv, mq


def dilated_offsets(base, n_q, window, dil):
    return base + np.arange(n_q)[:, None] - dil * np.arange(window // dil + 1)[None, :]


def dilated_stats(q, sel, valid):
    s = jnp.einsum('bthd,btnhd->bthn', q, sel[:, :, :, 0]).astype(jnp.float32) * ATTN_SCALE
    s = jnp.where(valid[:, None, :], s, -jnp.inf)
    m = jnp.max(s, axis=-1, keepdims=True)
    e = jnp.exp(s - m)
    den = jnp.sum(e, axis=-1)
    o = jnp.einsum('bthn,btnhd->bthd', e, sel[:, :, :, 1].astype(jnp.float32)) / den[..., None]
    return o, m[..., 0], den


def combine_groups(stats):
    o = jnp.stack([st[0] for st in stats])
    m = jnp.stack([st[1] for st in stats])
    den = jnp.stack([st[2] for st in stats])
    wgt = den * jnp.exp(m - jnp.max(m, axis=0, keepdims=True))
    return jnp.sum(wgt[..., None] * o, axis=0) / jnp.sum(wgt, axis=0)[..., None]


def mixer_b_prompt(p):
    b, s = p.shape[:2]
    q, kv, mq = prep_b(p, jnp.arange(s))
    kv_pad = [jnp.pad(kv[:, :, g], ((0, 0), (w, 0), (0, 0), (0, 0), (0, 0)))
              for g, (w, d) in enumerate(B_GROUPS)]

    def block(t0):
        qb = lax.dynamic_slice_in_dim(q, t0, Q_BLOCK, axis=1)
        stats = []
        for g, (w, d) in enumerate(B_GROUPS):
            win = lax.dynamic_slice_in_dim(kv_pad[g], t0, w + Q_BLOCK, axis=1)
            loc = dilated_offsets(w, Q_BLOCK, w, d)
            stats.append(dilated_stats(qb[:, :, g], win[:, loc], (t0 - w + loc) >= 0))
        return combine_groups(stats)

    o = lax.map(block, jnp.arange(s // Q_BLOCK) * Q_BLOCK)
    o = jnp.moveaxis(o, 0, 1).reshape(b, s, B_HEADS * HEAD_DIM).astype(p.dtype)
    bufs = [kv[:, s - min(w, s):, g] for g, (w, d) in enumerate(B_GROUPS)]
    return o, mq, bufs


def mixer_b_sample(p, bufs):
    b, t = p.shape[:2]
    q, kv, mq = prep_b(p, PAST_LEN + jnp.arange(t))
    stats, new_bufs = [], []
    for g, (w, d) in enumerate(B_GROUPS):
        cat = jnp.concatenate([bufs[g], kv[:, :, g]], axis=1)
        loc = dilated_offsets(bufs[g].shape[1], t, w, d)
        stats.append(dilated_stats(q[:, :, g], cat[:, np.maximum(loc, 0)], loc >= 0))
        new_bufs.append(cat[:, cat.shape[1] - min(w, PAST_LEN + t):])
    o = combine_groups(stats).reshape(b, t, B_HEADS * HEAD_DIM).astype(p.dtype)
    return o, mq, new_bufs


def setup_inputs(seed: int = 0) -> dict:
    key = jax.random.key(seed)
    ks = jax.random.split(key, 24)
    n_pages = PAST_LEN // PAGE_SIZE
    n_used = DEC_BATCH * n_pages
    n_pool = n_used + (n_used + 3) // 4
    nrm = lambda k, shape, sc=1.0: jax.random.normal(k, shape, jnp.float32) * sc
    x_prompt = nrm(ks[0], (BATCH, SEQ, D_MODEL))
    x_sample = nrm(ks[1], (DEC_BATCH, DEC_SEQ, D_MODEL))
    cache_kv_a = nrm(ks[2], (N_A, n_pool, PAGE_SIZE, 2, A_KV_HEADS, HEAD_DIM))
    cache_kidx_a = nrm(ks[3], (N_A, n_pool, PAGE_SIZE, IDX_DIM))
    state_kv_b1 = nrm(ks[4], (N_B, DEC_BATCH, min(B_GROUPS[0][0], PAST_LEN), 2, B_HEADS, HEAD_DIM))
    state_kv_b2 = nrm(ks[5], (N_B, DEC_BATCH, min(B_GROUPS[1][0], PAST_LEN), 2, B_HEADS, HEAD_DIM))
    state_kv_b3 = nrm(ks[6], (N_B, DEC_BATCH, min(B_GROUPS[2][0], PAST_LEN), 2, B_HEADS, HEAD_DIM))
    cache_mem_kv = nrm(ks[7], (DEPTH, DEC_BATCH, MEM_LEN, 2, MEM_HEADS, HEAD_DIM))
    page_table = jax.random.permutation(ks[8], n_pool)[:n_used].reshape(DEC_BATCH, n_pages).astype(jnp.int32)
    mem_prompt = nrm(ks[9], (BATCH, MEM_LEN, D_MODEL))
    ffn_norm = 1.0 + nrm(ks[10], (DEPTH, 2, D_MODEL), 0.02)
    w_ffn_in = nrm(ks[11], (DEPTH, 2, D_MODEL, 2 * D_FF), D_MODEL ** -0.5)
    w_ffn_out = nrm(ks[12], (DEPTH, 2, D_FF, D_MODEL), D_FF ** -0.5)
    mix_norm = 1.0 + nrm(ks[13], (DEPTH, D_MODEL), 0.02)
    mem_norm = 1.0 + nrm(ks[14], (DEPTH, D_MODEL), 0.02)
    w_mem_kv = nrm(ks[15], (DEPTH, D_MODEL, 2 * MEM_HEADS * HEAD_DIM), D_MODEL ** -0.5)
    w_in_a = nrm(ks[16], (N_A, D_MODEL, A_IN), D_MODEL ** -0.5)
    w_out_a = nrm(ks[17], (N_A, A_OUT, D_MODEL), A_OUT ** -0.5)
    w_in_b = nrm(ks[18], (N_B, D_MODEL, B_IN), D_MODEL ** -0.5)
    w_out_b = nrm(ks[19], (N_B, B_OUT, D_MODEL), B_OUT ** -0.5)
    final_norm = 1.0 + nrm(ks[20], (D_MODEL,), 0.02)
    return {'x_prompt': x_prompt, 'x_sample': x_sample, 'cache_kv_a': cache_kv_a,
            'cache_kidx_a': cache_kidx_a, 'state_kv_b1': state_kv_b1, 'state_kv_b2': state_kv_b2,
            'state_kv_b3': state_kv_b3, 'cache_mem_kv': cache_mem_kv, 'page_table': page_table,
            'mem_prompt': mem_prompt, 'ffn_norm': ffn_norm, 'w_ffn_in': w_ffn_in, 'w_ffn_out': w_ffn_out,
            'mix_norm': mix_norm, 'mem_norm': mem_norm, 'w_mem_kv': w_mem_kv, 'w_in_a': w_in_a,
            'w_out_a': w_out_a, 'w_in_b': w_in_b, 'w_out_b': w_out_b, 'final_norm': final_norm}


def reference(x_prompt, x_sample, cache_kv_a, cache_kidx_a, state_kv_b1, state_kv_b2, state_kv_b3,
              cache_mem_kv, page_table, mem_prompt, ffn_norm, w_ffn_in, w_ffn_out, mix_norm, mem_norm,
              w_mem_kv, w_in_a, w_out_a, w_in_b, w_out_b, final_norm):
    yp, ys = x_prompt, x_sample
    bufs_in = (state_kv_b1, state_kv_b2, state_kv_b3)
    kv_a_p, kidx_a_p, kv_a_s, kidx_a_s, mem_p = [], [], [], [], []
    b_p = ([], [], [])
    b_s = ([], [], [])
    for i in range(DEPTH):
        j = i // 2
        yp = yp + 0.5 * swiglu(rmsnorm(yp, ffn_norm[i, 0]), w_ffn_in[i, 0], w_ffn_out[i, 0])
        ys = ys + 0.5 * swiglu(rmsnorm(ys, ffn_norm[i, 0]), w_ffn_in[i, 0], w_ffn_out[i, 0])
        hp = rmsnorm(yp, mix_norm[i])
        hs = rmsnorm(ys, mix_norm[i])
        mkv_p = memory_kv(mem_prompt, mem_norm[i], w_mem_kv[i])
        mem_p.append(mkv_p)
        if i % 2 == 0:
            op, mqp, kvp, ikp = mixer_a_prompt(hp @ w_in_a[j])
            os_, mqs, kvs, iks = mixer_a_sample(hs @ w_in_a[j], cache_kv_a[j], cache_kidx_a[j], page_table)
            kv_a_p.append(kvp)
            kidx_a_p.append(ikp)
            kv_a_s.append(kvs)
            kidx_a_s.append(iks)
            w_out = w_out_a[j]
        else:
            op, mqp, bp = mixer_b_prompt(hp @ w_in_b[j])
            os_, mqs, bs = mixer_b_sample(hs @ w_in_b[j], [bf[j] for bf in bufs_in])
            for g in range(len(B_GROUPS)):
                b_p[g].append(bp[g])
                b_s[g].append(bs[g])
            w_out = w_out_b[j]
        yp = yp + jnp.concatenate([op, mem_attention(mqp, mkv_p).astype(op.dtype)], axis=-1) @ w_out
        ys = ys + jnp.concatenate([os_, mem_attention(mqs, cache_mem_kv[i]).astype(os_.dtype)], axis=-1) @ w_out
        yp = yp + 0.5 * swiglu(rmsnorm(yp, ffn_norm[i, 1]), w_ffn_in[i, 1], w_ffn_out[i, 1])
        ys = ys + 0.5 * swiglu(rmsnorm(ys, ffn_norm[i, 1]), w_ffn_in[i, 1], w_ffn_out[i, 1])
    y_prompt = rmsnorm(yp, final_norm)
    y_sample = rmsnorm(ys, final_norm)
    new_kv_a_prompt = jnp.stack(kv_a_p)
    new_kidx_a_prompt = jnp.stack(kidx_a_p)
    new_b1_prompt = jnp.stack(b_p[0])
    new_b2_prompt = jnp.stack(b_p[1])
    new_b3_prompt = jnp.stack(b_p[2])
    new_mem_kv_prompt = jnp.stack(mem_p)
    new_kv_a_sample = jnp.stack(kv_a_s)
    new_kidx_a_sample = jnp.stack(kidx_a_s)
    new_b1_sample = jnp.stack(b_s[0])
    new_b2_sample = jnp.stack(b_s[1])
    new_b3_sample = jnp.stack(b_s[2])
    return (y_prompt, y_sample, new_kv_a_prompt, new_kidx_a_prompt, new_b1_prompt, new_b2_prompt,
            new_b3_prompt, new_mem_kv_prompt, new_kv_a_sample, new_kidx_a_sample, new_b1_sample,
            new_b2_sample, new_b3_sample)
```

```python
import functools

import jax
import jax.numpy as jnp
import numpy as np
from jax import lax
from jax.experimental import pallas as pl
from jax.experimental.pallas import tpu as pltpu

F32 = jnp.float32
BF = jnp.bfloat16
I32 = jnp.int32

HEAD_DIM = 64
ROT_DIM = HEAD_DIM // 4
ROPE_THETA = 500000.0
ATTN_SCALE = HEAD_DIM ** -0.5
A_HEADS = 12
A_KV_HEADS = 4
A_GROUP = A_HEADS // A_KV_HEADS
IDX_HEADS = 8
IDX_DIM = 64
IDX_SCALE = (IDX_HEADS * IDX_DIM) ** -0.5
TOPK_MAX = 256
B_GROUPS = ((128, 1), (512, 4), (2048, 16))
B_HEADS = 4
MEM_HEADS = 4
EPS = 1e-6
PAGE_SIZE = 128

LANES = 128
PAIR = 2 * HEAD_DIM
KV_W = A_KV_HEADS * HEAD_DIM
VMEM_LIMIT_V7X = 56 * 2 ** 20
NEG_BIG = -1e30
M_INIT = -3e38
INT_MIN = -2 ** 31
KEY_NEG_INF = -2139095041

TM = 512
TQ = 256
TB = 128
PAGES_PER_STEP = 16


def _cp(n_axes):
    return pltpu.CompilerParams(dimension_semantics=("arbitrary",) * n_axes,
                                vmem_limit_bytes=VMEM_LIMIT_V7X)


def _dot(a, b):
    return jnp.dot(a, b, preferred_element_type=F32)


def _dot_nt(a, b):
    return lax.dot_general(a, b, (((1,), (1,)), ((), ())), preferred_element_type=F32)


def _rms(x, g):
    return x * lax.rsqrt(jnp.mean(x * x, axis=-1, keepdims=True) + EPS) * g


def _sort_key(x):
    b = pltpu.bitcast(x + 0.0, I32)
    return b ^ ((b >> 31) & 0x7FFFFFFF)


def _ffn_body(y_ref, g_ref, win_ref, wout_ref, o_ref, *, d_ff, fc):
    x = y_ref[...]
    xn = _rms(x, g_ref[...]).astype(BF)
    acc = jnp.zeros(x.shape, F32)
    for c in range(d_ff // fc):
        a = _dot(xn, win_ref[:, c * fc:(c + 1) * fc])
        u = _dot(xn, win_ref[:, d_ff + c * fc:d_ff + (c + 1) * fc])
        h = (a * jax.nn.sigmoid(a) * u).astype(BF)
        acc = acc + _dot(h, wout_ref[c * fc:(c + 1) * fc, :])
    o_ref[...] = x + 0.5 * acc


def _ffn(y, g, w_in, w_out, tm):
    m, d = y.shape
    d_ff = w_out.shape[0]
    fc = 256 if d_ff % 256 == 0 else LANES
    assert d_ff % fc == 0 and m % tm == 0
    return pl.pallas_call(
        functools.partial(_ffn_body, d_ff=d_ff, fc=fc),
        grid=(m // tm,),
        in_specs=[pl.BlockSpec((tm, d), lambda i: (i, 0)),
                  pl.BlockSpec((1, d), lambda i: (0, 0)),
                  pl.BlockSpec((d, 2 * d_ff), lambda i: (0, 0), pipeline_mode=pl.Buffered(1)),
                  pl.BlockSpec((d_ff, d), lambda i: (0, 0), pipeline_mode=pl.Buffered(1))],
        out_specs=pl.BlockSpec((tm, d), lambda i: (i, 0)),
        out_shape=jax.ShapeDtypeStruct((m, d), F32),
        compiler_params=_cp(1),
        name="ffn",
    )(y, g.reshape(1, d), w_in, w_out)


def _rope_tables(pos):
    half = ROT_DIM // 2
    inv = jnp.power(ROPE_THETA, -jnp.arange(half, dtype=F32) / half)
    ang = pos.astype(F32)[:, None] * inv[None, :]
    cos, sin = jnp.cos(ang), jnp.sin(ang)
    t = pos.shape[0]
    ones = jnp.ones((t, HEAD_DIM - ROT_DIM), F32)
    zeros = jnp.zeros((t, HEAD_DIM - ROT_DIM), F32)
    zh = jnp.zeros((t, half), F32)
    cos64 = jnp.concatenate([cos, cos, ones], axis=1)
    sa64 = jnp.concatenate([-sin, zh, zeros], axis=1)
    sb64 = jnp.concatenate([zh, sin, zeros], axis=1)
    return tuple(jnp.tile(a, (1, LANES // HEAD_DIM)) for a in (cos64, sa64, sb64))


def _proj_body(*refs, segs, nts, n_in):
    y_ref, g_ref, cos_ref, sa_ref, sb_ref, w_ref = refs[:6]
    wt_refs = refs[6:n_in]
    outs = refs[n_in:]
    xn = _rms(y_ref[...], g_ref[...]).astype(BF)
    for lo, hi, rope, scale, writes in segs:
        p = _dot(xn, w_ref[:, lo:hi])
        if rope:
            cos, sa, sb = cos_ref[...], sa_ref[...], sb_ref[...]
            chunks = []
            for c in range((hi - lo) // LANES):
                pc = p[:, c * LANES:(c + 1) * LANES]
                chunks.append(pc * cos + pltpu.roll(pc, LANES - ROT_DIM // 2, 1) * sa
                              + pltpu.roll(pc, ROT_DIM // 2, 1) * sb)
            p = chunks[0] if len(chunks) == 1 else jnp.concatenate(chunks, axis=1)
        if scale != 1.0:
            p = p * scale
        for oi, slo, shi, dlo in writes:
            outs[oi][:, dlo:dlo + (shi - slo)] = p[:, slo:shi].astype(outs[oi].dtype)
    for t, (oi, scale) in enumerate(nts):
        r = _dot_nt(wt_refs[t][...], xn)
        if scale != 1.0:
            r = r * scale
        outs[oi][...] = r.astype(outs[oi].dtype)


def _proj(y, g, tables, n_pos_blocks, w, wts, segs, nts, out_defs, tm):
    m, d = y.shape
    assert m % tm == 0
    n_in = 6 + len(wts)
    pos_map = (lambda i: (i % n_pos_blocks, 0)) if n_pos_blocks > 1 else (lambda i: (0, 0))
    in_specs = [pl.BlockSpec((tm, d), lambda i: (i, 0)),
                pl.BlockSpec((1, d), lambda i: (0, 0)),
                pl.BlockSpec((tm, LANES), pos_map),
                pl.BlockSpec((tm, LANES), pos_map),
                pl.BlockSpec((tm, LANES), pos_map),
                pl.BlockSpec(w.shape, lambda i: (0, 0), pipeline_mode=pl.Buffered(1))]
    in_specs += [pl.BlockSpec(wt.shape, lambda i: (0, 0), pipeline_mode=pl.Buffered(1)) for wt in wts]
    out_specs, out_shapes = [], []
    for width, dtype, transposed in out_defs:
        if transposed:
            out_specs.append(pl.BlockSpec((width, tm), lambda i: (0, i)))
            out_shapes.append(jax.ShapeDtypeStruct((width, m), dtype))
        else:
            out_specs.append(pl.BlockSpec((tm, width), lambda i: (i, 0)))
            out_shapes.append(jax.ShapeDtypeStruct((m, width), dtype))
    return pl.pallas_call(
        functools.partial(_proj_body, segs=tuple(segs), nts=tuple(nts), n_in=n_in),
        grid=(m // tm,),
        in_specs=in_specs,
        out_specs=out_specs,
        out_shape=out_shapes,
        compiler_params=_cp(1),
        name="proj",
    )(y, g.reshape(1, d), *tables, w, *wts)


def _heads(w, lo, n_heads):
    return w[:, lo:lo + n_heads * HEAD_DIM].reshape(w.shape[0], n_heads, HEAD_DIM)


def _aligned(wh, halves):
    d, h, _ = wh.shape
    z = jnp.zeros_like(wh)
    sel = jnp.asarray(halves, I32).reshape(1, h, 1)
    lo = jnp.where(sel == 0, wh, z)
    hi = jnp.where(sel == 1, wh, z)
    return jnp.concatenate([lo, hi], axis=2).reshape(d, h * PAIR)


def _blocked(wh, slots, n_rows):
    d, h, _ = wh.shape
    onehot = jnp.asarray(np.eye(A_KV_HEADS, dtype=np.float32)[np.asarray(slots)], wh.dtype)
    blk = (wh[:, :, None, :] * onehot[None, :, :, None]).reshape(d, h * KV_W)
    return jnp.pad(blk, ((0, 0), (0, (n_rows - h) * KV_W)))


def _expand_out(w_rows, slots, n_rows):
    h = len(slots)
    d = w_rows.shape[1]
    wh = w_rows.reshape(h, HEAD_DIM, d)
    onehot = jnp.asarray(np.eye(A_KV_HEADS, dtype=np.float32)[np.asarray(slots)], w_rows.dtype)
    blk = (wh[:, None, :, :] * onehot[:, :, None, None]).reshape(h * KV_W, d)
    return jnp.pad(blk, ((0, (n_rows - h) * KV_W), (0, 0)))


_A_OFF = np.cumsum([0, A_HEADS * HEAD_DIM, KV_W, KV_W, IDX_HEADS * IDX_DIM, IDX_DIM, IDX_HEADS]).tolist()
_A_Q_HALF = [(h // A_GROUP) % 2 for h in range(A_HEADS)]
_A_Q_SLOT = [h // A_GROUP for h in range(A_HEADS)]
_MEM_HALF = [h % 2 for h in range(MEM_HEADS)]
_MEM_SLOT = list(range(MEM_HEADS))
_B_HALF = [h % 2 for h in range(B_HEADS)]
_B_SLOT = list(range(B_HEADS))
_SAMPLE_A_ROWS = 16
_SAMPLE_ROWS = 8


def _segments(items):
    segs, cols, lo = [], [], 0
    for wc, rope, scale, writes in items:
        hi = lo + wc.shape[1]
        segs.append((lo, hi, rope, scale, tuple(writes)))
        cols.append(wc)
        lo = hi
    return jnp.concatenate(cols, axis=1).astype(BF), segs


def _proj_a_prompt(y, g, w, tables, n_pos_blocks):
    o = _A_OFF
    zpad = jnp.zeros((w.shape[0], LANES - IDX_DIM), w.dtype)
    items = [
        (_aligned(_heads(w, o[0], A_HEADS), _A_Q_HALF), True, ATTN_SCALE, [(0, 0, A_HEADS * PAIR, 0)]),
        (w[:, o[1]:o[2]], True, 1.0, [(1, 0, KV_W, 0), (2, 0, KV_W, 0)]),
        (w[:, o[2]:o[3]], False, 1.0, [(1, 0, KV_W, KV_W)]),
        (_aligned(_heads(w, o[3], IDX_HEADS), [0] * IDX_HEADS), True, 1.0, [(3, 0, IDX_HEADS * PAIR, 0)]),
        (jnp.concatenate([w[:, o[4]:o[5]], zpad], axis=1), True, 1.0, [(4, 0, IDX_DIM, 0), (5, 0, LANES, 0)]),
        (_aligned(_heads(w, o[6], MEM_HEADS), _MEM_HALF), False, ATTN_SCALE, [(6, 0, MEM_HEADS * PAIR, 0)]),
    ]
    wcat, segs = _segments(items)
    wts = [w[:, o[2]:o[3]].T.astype(BF), w[:, o[5]:o[6]].T.astype(BF)]
    nts = [(7, 1.0), (8, IDX_SCALE)]
    out_defs = [(A_HEADS * PAIR, BF, False), (2 * KV_W, F32, False), (KV_W, BF, False),
                (IDX_HEADS * PAIR, BF, False), (IDX_DIM, F32, False), (LANES, BF, False),
                (MEM_HEADS * PAIR, BF, False), (KV_W, BF, True), (IDX_HEADS, F32, True)]
    return _proj(y, g, tables, n_pos_blocks, wcat, wts, segs, nts, out_defs, TM)


def _proj_a_sample(y, g, w, tables):
    o = _A_OFF
    zpad = jnp.zeros((w.shape[0], LANES - IDX_DIM), w.dtype)
    wpad = jnp.zeros((w.shape[0], LANES - IDX_HEADS), w.dtype)
    items = [
        (_blocked(_heads(w, o[0], A_HEADS), _A_Q_SLOT, _SAMPLE_A_ROWS), True, ATTN_SCALE,
         [(0, 0, _SAMPLE_A_ROWS * KV_W, 0)]),
        (w[:, o[1]:o[2]], True, 1.0, [(1, 0, KV_W, 0)]),
        (w[:, o[2]:o[3]], False, 1.0, [(1, 0, KV_W, KV_W)]),
        (w[:, o[3]:o[4]], True, 1.0, [(2, 0, IDX_HEADS * IDX_DIM, 0)]),
        (jnp.concatenate([w[:, o[4]:o[5]], zpad], axis=1), True, 1.0, [(3, 0, IDX_DIM, 0)]),
        (jnp.concatenate([w[:, o[5]:o[6]], wpad], axis=1), False, IDX_SCALE, [(4, 0, LANES, 0)]),
        (_blocked(_heads(w, o[6], MEM_HEADS), _MEM_SLOT, _SAMPLE_ROWS), False, ATTN_SCALE,
         [(5, 0, _SAMPLE_ROWS * KV_W, 0)]),
    ]
    wcat, segs = _segments(items)
    out_defs = [(_SAMPLE_A_ROWS * KV_W, BF, False), (2 * KV_W, F32, False), (IDX_HEADS * IDX_DIM, BF, False),
                (IDX_DIM, F32, False), (LANES, F32, False), (_SAMPLE_ROWS * KV_W, BF, False)]
    return _proj(y, g, tables, 1, wcat, [], segs, [], out_defs, y.shape[0])


def _b_off(g, t):
    return (g * 3 + t) * B_HEADS * HEAD_DIM


def _proj_b_prompt(y, g, w, tables, n_pos_blocks):
    items, out_defs = [], []
    for gi in range(len(B_GROUPS)):
        items += [
            (_aligned(_heads(w, _b_off(gi, 0), B_HEADS), _B_HALF), True, ATTN_SCALE,
             [(2 * gi, 0, B_HEADS * PAIR, 0)]),
            (w[:, _b_off(gi, 1):_b_off(gi, 1) + KV_W], True, 1.0, [(2 * gi + 1, 0, KV_W, 0)]),
            (w[:, _b_off(gi, 2):_b_off(gi, 2) + KV_W], False, 1.0, [(2 * gi + 1, 0, KV_W, KV_W)]),
        ]
        out_defs += [(B_HEADS * PAIR, BF, False), (2 * KV_W, F32, False)]
    n_o = 2 * len(B_GROUPS)
    items.append((_aligned(_heads(w, _b_off(len(B_GROUPS), 0), MEM_HEADS), _MEM_HALF), False, ATTN_SCALE,
                  [(n_o, 0, MEM_HEADS * PAIR, 0)]))
    out_defs.append((MEM_HEADS * PAIR, BF, False))
    wcat, segs = _segments(items)
    return _proj(y, g, tables, n_pos_blocks, wcat, [], segs, [], out_defs, TM)


def _proj_b_sample(y, g, w, tables):
    ng = len(B_GROUPS)
    items, out_defs = [], [(ng * _SAMPLE_ROWS * KV_W, BF, False)]
    for gi in range(ng):
        items += [
            (_blocked(_heads(w, _b_off(gi, 0), B_HEADS), _B_SLOT, _SAMPLE_ROWS), True, ATTN_SCALE,
             [(0, 0, _SAMPLE_ROWS * KV_W, gi * _SAMPLE_ROWS * KV_W)]),
            (w[:, _b_off(gi, 1):_b_off(gi, 1) + KV_W], True, 1.0, [(1 + gi, 0, KV_W, 0)]),
            (w[:, _b_off(gi, 2):_b_off(gi, 2) + KV_W], False, 1.0, [(1 + gi, 0, KV_W, KV_W)]),
        ]
        out_defs.append((2 * KV_W, F32, False))
    items.append((_blocked(_heads(w, _b_off(ng, 0), MEM_HEADS), _MEM_SLOT, _SAMPLE_ROWS), False, ATTN_SCALE,
                  [(1 + ng, 0, _SAMPLE_ROWS * KV_W, 0)]))
    out_defs.append((_SAMPLE_ROWS * KV_W, BF, False))
    wcat, segs = _segments(items)
    return _proj(y, g, tables, 1, wcat, [], segs, [], out_defs, y.shape[0])


def _proj_plain(y, g, w, tm):
    n = w.shape[1]
    dummy = jnp.zeros((tm, LANES), F32)
    segs = [(0, n, False, 1.0, ((0, 0, n, 0),))]
    return _proj(y, g, (dummy, dummy, dummy), 1, w.astype(BF), [], segs, [], [(n, F32, False)], tm)[0]


def _select_bias(read, write, n_chunks, rows, n_q, k_sel, count_axis):
    red_shape = (1, n_q) if count_axis == 0 else (n_q, 1)

    def count(pred):
        def body(c, cnt):
            return cnt + jnp.sum(jnp.where(pred(read(c)), 1, 0).astype(I32), axis=count_axis, keepdims=True)
        return lax.fori_loop(0, n_chunks, body, jnp.zeros(red_shape, I32))

    t0 = jnp.where(count(lambda x: x >= 0) >= k_sel, 0, INT_MIN).astype(I32)

    def bit_body(i, t):
        cand = t + lax.shift_left(jnp.int32(1), 30 - i)
        return jnp.where(count(lambda x: x >= cand) >= k_sel, cand, t)

    thr = lax.fori_loop(0, 31, bit_body, t0)
    n_gt = count(lambda x: x > thr)
    n_ge = count(lambda x: x >= thr)
    has_tie = jnp.max(n_ge) > k_sel

    @pl.when(jnp.logical_not(has_tie))
    def _():
        def body(c, _):
            x = read(c)
            sel = jnp.logical_and(x >= thr, x > KEY_NEG_INF)
            write(c, jnp.where(sel, 0.0, NEG_BIG).astype(F32))
            return 0
        lax.fori_loop(0, n_chunks, body, 0)

    @pl.when(has_tie)
    def _():
        need = (k_sel - n_gt).astype(F32)
        r = lax.broadcasted_iota(I32, (rows, rows), 0)
        cidx = lax.broadcasted_iota(I32, (rows, rows), 1)
        if count_axis == 0:
            tri = jnp.where(cidx < r, 1.0, 0.0).astype(BF)
        else:
            tri = jnp.where(r < cidx, 1.0, 0.0).astype(BF)

        def body(c, seen):
            x = read(c)
            eq = jnp.logical_and(x == thr, x > KEY_NEG_INF)
            eqf = jnp.where(eq, 1.0, 0.0).astype(BF)
            rank = (_dot(tri, eqf) if count_axis == 0 else _dot(eqf, tri)) + seen
            sel = jnp.logical_and(jnp.logical_or(x > thr, jnp.logical_and(eq, rank < need)), x > KEY_NEG_INF)
            write(c, jnp.where(sel, 0.0, NEG_BIG).astype(F32))
            return seen + jnp.sum(eqf.astype(F32), axis=count_axis, keepdims=True)
        lax.fori_loop(0, n_chunks, body, jnp.zeros(red_shape, F32))


def _attn_a_body(q_ref, iq_ref, iw_ref, k_ref, ik_ref, vt_ref, o_ref, sc_ref, *, tq, k_sel):
    qi = pl.program_id(1)
    iw = iw_ref[...]

    def idx_chunk(r0, diag):
        ikc = ik_ref[pl.ds(r0, tq), :]
        acc = jnp.zeros((tq, tq), F32)
        for h in range(IDX_HEADS):
            s = _dot_nt(ikc, iq_ref[:, h * PAIR:(h + 1) * PAIR])
            acc = acc + jnp.maximum(s, 0.0) * iw[h:h + 1, :]
        if diag:
            kpos = lax.broadcasted_iota(I32, (tq, tq), 0)
            qpos = lax.broadcasted_iota(I32, (tq, tq), 1)
            acc = jnp.where(kpos <= qpos, acc, -jnp.inf)
        sc_ref[pl.ds(r0, tq), :] = _sort_key(acc)

    def full_chunk(c, _):
        idx_chunk(pl.multiple_of(c * tq, tq), False)
        return 0
    lax.fori_loop(0, qi, full_chunk, 0)
    idx_chunk(pl.multiple_of(qi * tq, tq), True)

    def read(c):
        return sc_ref[pl.ds(pl.multiple_of(c * tq, tq), tq), :]

    def write(c, v):
        sc_ref[pl.ds(pl.multiple_of(c * tq, tq), tq), :] = pltpu.bitcast(v, I32)

    _select_bias(read, write, qi + 1, tq, tq, k_sel, 0)

    for hp in range(A_HEADS // 2):
        outs = []
        for h in (2 * hp, 2 * hp + 1):
            n = h // A_GROUP
            pair = n // 2
            qh = q_ref[:, h * PAIR:(h + 1) * PAIR]

            def body(c, carry, qh=qh, n=n, pair=pair):
                m, l, acc = carry
                r0 = pl.multiple_of(c * tq, tq)
                kc = k_ref[pl.ds(r0, tq), pair * PAIR:(pair + 1) * PAIR]
                s = _dot_nt(kc, qh) + pltpu.bitcast(sc_ref[pl.ds(r0, tq), :], F32)
                m_new = jnp.maximum(m, jnp.max(s, axis=0, keepdims=True))
                alpha = jnp.exp(m - m_new)
                p = jnp.exp(s - m_new)
                l = alpha * l + jnp.sum(p, axis=0, keepdims=True)
                vt = vt_ref[n * HEAD_DIM:(n + 1) * HEAD_DIM, pl.ds(r0, tq)]
                acc = alpha * acc + _dot(vt, p.astype(BF))
                return m_new, l, acc

            init = (jnp.full((1, tq), M_INIT, F32), jnp.zeros((1, tq), F32), jnp.zeros((HEAD_DIM, tq), F32))
            _, l, acc = lax.fori_loop(0, qi + 1, body, init)
            outs.append(acc / l)
        o_ref[:, hp * PAIR:(hp + 1) * PAIR] = jnp.concatenate(outs, axis=0).T.astype(o_ref.dtype)


def _attn_a_prompt(q_al, iq_al, iw_t, k_bf, ik128, v_t, batch, seq):
    tq = min(TQ, seq)
    nq = seq // tq
    m = batch * seq
    k_sel = min(TOPK_MAX, seq // 4)
    return pl.pallas_call(
        functools.partial(_attn_a_body, tq=tq, k_sel=k_sel),
        grid=(batch, nq),
        in_specs=[pl.BlockSpec((tq, A_HEADS * PAIR), lambda b, i: (b * nq + i, 0)),
                  pl.BlockSpec((tq, IDX_HEADS * PAIR), lambda b, i: (b * nq + i, 0)),
                  pl.BlockSpec((IDX_HEADS, tq), lambda b, i: (0, b * nq + i)),
                  pl.BlockSpec((seq, KV_W), lambda b, i: (b, 0)),
                  pl.BlockSpec((seq, LANES), lambda b, i: (b, 0)),
                  pl.BlockSpec((KV_W, seq), lambda b, i: (0, b))],
        out_specs=pl.BlockSpec((tq, A_HEADS * HEAD_DIM), lambda b, i: (b * nq + i, 0)),
        out_shape=jax.ShapeDtypeStruct((m, A_HEADS * HEAD_DIM), BF),
        scratch_shapes=[pltpu.VMEM((seq, tq), I32)],
        compiler_params=_cp(2),
        name="attn_a_prompt",
    )(q_al, iq_al, iw_t, k_bf, ik128, v_t)


def _attn_b_body(q_ref, kvc_ref, kvp_ref, o_ref, m_ref, d_ref, *, tb, span):
    i = pl.program_id(2)
    q = q_ref[0]
    kvc, kvp = kvc_ref[0], kvp_ref[0]
    kcat = jnp.concatenate([kvp[:, :KV_W], kvc[:, :KV_W]], axis=0).astype(BF)
    vcat = jnp.concatenate([kvp[:, KV_W:], kvc[:, KV_W:]], axis=0).astype(BF)
    qrow = lax.broadcasted_iota(I32, (tb, 2 * tb), 0)
    kcol = lax.broadcasted_iota(I32, (tb, 2 * tb), 1)
    rel = qrow + tb - kcol
    valid = jnp.logical_and(jnp.logical_and(rel >= 0, rel <= span), jnp.logical_or(kcol >= tb, i > 0))
    lo_half = lax.broadcasted_iota(I32, (tb, PAIR), 1) < HEAD_DIM
    o_p, m_p, d_p = [], [], []
    for pair in range(B_HEADS // 2):
        kp = kcat[:, pair * PAIR:(pair + 1) * PAIR]
        vp = vcat[:, pair * PAIR:(pair + 1) * PAIR]
        res = []
        for h in (2 * pair, 2 * pair + 1):
            s = jnp.where(valid, _dot_nt(q[:, h * PAIR:(h + 1) * PAIR], kp), -jnp.inf)
            mx = jnp.max(s, axis=1, keepdims=True)
            e = jnp.exp(s - mx)
            den = jnp.sum(e, axis=1, keepdims=True)
            res.append((_dot(e.astype(BF), vp) / den, mx, den))
        o_p.append(jnp.where(lo_half, res[0][0], res[1][0]))
        m_p.append(jnp.where(lo_half, res[0][1], res[1][1]))
        d_p.append(jnp.where(lo_half, res[0][2], res[1][2]))
    o_ref[0] = jnp.concatenate(o_p, axis=1)
    m_ref[0] = jnp.concatenate(m_p, axis=1)
    d_ref[0] = jnp.concatenate(d_p, axis=1)


def _attn_b_prompt(q_al, kv, batch, seq, window, dil):
    sd = seq // dil
    tb = min(TB, sd)
    span = window // dil
    assert tb >= span and sd % tb == 0
    nb = sd // tb
    qv = q_al.reshape(batch, sd, dil * B_HEADS * PAIR)
    kvv = kv.reshape(batch, sd, dil * 2 * KV_W)
    stat = jax.ShapeDtypeStruct((batch, sd, dil * KV_W), F32)
    ospec = pl.BlockSpec((1, tb, KV_W), lambda b, r, i: (b, i, r))
    outs = pl.pallas_call(
        functools.partial(_attn_b_body, tb=tb, span=span),
        grid=(batch, dil, nb),
        in_specs=[pl.BlockSpec((1, tb, B_HEADS * PAIR), lambda b, r, i: (b, i, r)),
                  pl.BlockSpec((1, tb, 2 * KV_W), lambda b, r, i: (b, i, r)),
                  pl.BlockSpec((1, tb, 2 * KV_W), lambda b, r, i: (b, jnp.maximum(i - 1, 0), r))],
        out_specs=[ospec, ospec, ospec],
        out_shape=[stat, stat, stat],
        compiler_params=_cp(3),
        name="attn_b_prompt",
    )(qv, kvv, kvv)
    return [a.reshape(batch * seq, KV_W) for a in outs]


def _memattn_body(mq_ref, mem_ref, o_ref):
    mq = mq_ref[...]
    mem = mem_ref[0]
    mk, mv = mem[:, :KV_W].astype(BF), mem[:, KV_W:].astype(BF)
    lo_half = lax.broadcasted_iota(I32, (mq.shape[0], PAIR), 1) < HEAD_DIM
    outs = []
    for pair in range(MEM_HEADS // 2):
        res = []
        for h in (2 * pair, 2 * pair + 1):
            s = _dot_nt(mq[:, h * PAIR:(h + 1) * PAIR], mk[:, pair * PAIR:(pair + 1) * PAIR])
            e = jnp.exp(s - jnp.max(s, axis=1, keepdims=True))
            p = e / jnp.sum(e, axis=1, keepdims=True)
            res.append(_dot(p.astype(BF), mv[:, pair * PAIR:(pair + 1) * PAIR]))
        outs.append(jnp.where(lo_half, res[0], res[1]))
    o_ref[...] = jnp.concatenate(outs, axis=1).astype(o_ref.dtype)


def _memattn_prompt(mq_al, mem_kv, seq, tm):
    m = mq_al.shape[0]
    mem_len = mem_kv.shape[1]
    tiles_per_seq = seq // tm
    return pl.pallas_call(
        _memattn_body,
        grid=(m // tm,),
        in_specs=[pl.BlockSpec((tm, MEM_HEADS * PAIR), lambda i: (i, 0)),
                  pl.BlockSpec((1, mem_len, 2 * KV_W), lambda i: (i // tiles_per_seq, 0, 0))],
        out_specs=pl.BlockSpec((tm, KV_W), lambda i: (i, 0)),
        out_shape=jax.ShapeDtypeStruct((m, KV_W), BF),
        compiler_params=_cp(1),
        name="memattn_prompt",
    )(mq_al, mem_kv)


def _outproj_body(*refs, n_x, combine):
    y_ref = refs[0]
    pos = 1
    xs = []
    if combine:
        og, mg, dg = refs[1:4], refs[4:7], refs[7:10]
        pos = 10
        ms = [r[...] for r in mg]
        mmax = jnp.maximum(jnp.maximum(ms[0], ms[1]), ms[2])
        wgt = [dg[k][...] * jnp.exp(ms[k] - mmax) for k in range(3)]
        num = wgt[0] * og[0][...] + wgt[1] * og[1][...] + wgt[2] * og[2][...]
        xs.append(num / (wgt[0] + wgt[1] + wgt[2]))
    xs += [refs[pos + k][...] for k in range(n_x)]
    w_refs = refs[pos + n_x:pos + n_x + len(xs)]
    o_ref = refs[-1]
    acc = y_ref[...]
    for x, w_ref in zip(xs, w_refs):
        acc = acc + _dot(x.astype(BF), w_ref[...])
    o_ref[...] = acc


def _outproj(y, stats, xs, ws, tm):
    m, d = y.shape
    row = lambda width: pl.BlockSpec((tm, width), lambda i: (i, 0))
    in_specs = [row(d)] + [row(a.shape[1]) for a in stats] + [row(x.shape[1]) for x in xs]
    in_specs += [pl.BlockSpec(w.shape, lambda i: (0, 0), pipeline_mode=pl.Buffered(1)) for w in ws]
    return pl.pallas_call(
        functools.partial(_outproj_body, n_x=len(xs), combine=bool(stats)),
        grid=(m // tm,),
        in_specs=in_specs,
        out_specs=row(d),
        out_shape=jax.ShapeDtypeStruct((m, d), F32),
        compiler_params=_cp(1),
        name="outproj",
    )(y, *stats, *xs, *[w.astype(BF) for w in ws])


def _final_norm_body(y_ref, g_ref, o_ref):
    o_ref[...] = _rms(y_ref[...], g_ref[...])


def _final_norm(y, g, tm):
    m, d = y.shape
    return pl.pallas_call(
        _final_norm_body,
        grid=(m // tm,),
        in_specs=[pl.BlockSpec((tm, d), lambda i: (i, 0)), pl.BlockSpec((1, d), lambda i: (0, 0))],
        out_specs=pl.BlockSpec((tm, d), lambda i: (i, 0)),
        out_shape=jax.ShapeDtypeStruct((m, d), F32),
        compiler_params=_cp(1),
        name="final_norm",
    )(y, g.reshape(1, d))


def _page_copies(pool_ref, buf_ref, sem_ref, pt_ref, b, first_page, n_pages, slot):
    return [pltpu.make_async_copy(pool_ref.at[pt_ref[b, first_page + p]],
                                  buf_ref.at[slot, pl.ds(p * PAGE_SIZE, PAGE_SIZE), :],
                                  sem_ref.at[slot]) for p in range(n_pages)]


def _idx_sample_body(pt_ref, iq_ref, iw_ref, ikn_ref, pool_ref, bias_ref, kbuf, sem, sc_ref,
                     *, n_seq, n_pages, k_sel):
    b = pl.program_id(0)
    past = n_pages * PAGE_SIZE
    slot = b % 2

    @pl.when(b == 0)
    def _():
        for cp in _page_copies(pool_ref, kbuf, sem, pt_ref, 0, 0, n_pages, 0):
            cp.start()

    @pl.when(b + 1 < n_seq)
    def _():
        for cp in _page_copies(pool_ref, kbuf, sem, pt_ref, b + 1, 0, n_pages, 1 - slot):
            cp.start()

    for cp in _page_copies(pool_ref, kbuf, sem, pt_ref, b, 0, n_pages, slot):
        cp.wait()

    iq = iq_ref[0]
    iw = iw_ref[0]
    s = _dot_nt(iq, kbuf[slot].astype(BF))
    row = jnp.sum(jnp.maximum(s, 0.0) * iw, axis=0, keepdims=True)
    ikn = ikn_ref[0].astype(BF).astype(F32)
    s_new = jnp.sum(iq.astype(F32) * ikn, axis=1, keepdims=True)
    r_new = jnp.sum(jnp.maximum(s_new, 0.0) * iw, axis=0, keepdims=True)
    lane = lax.broadcasted_iota(I32, (1, LANES), 1)
    tail = jnp.where(lane == 0, r_new, -jnp.inf)
    @pl.when(b == 0)
    def _():
        sc_ref[...] = jnp.zeros(sc_ref.shape, I32)

    is_b = lax.broadcasted_iota(I32, (n_seq, 1), 0) == b
    sc_ref[:, :past] = jnp.where(is_b, _sort_key(row), sc_ref[:, :past])
    sc_ref[:, past:] = jnp.where(is_b, _sort_key(tail), sc_ref[:, past:])

    @pl.when(b == n_seq - 1)
    def _():
        def read(c):
            return sc_ref[:, pl.ds(pl.multiple_of(c * LANES, LANES), LANES)]

        def write(c, v):
            bias_ref[:, pl.ds(pl.multiple_of(c * LANES, LANES), LANES)] = v

        _select_bias(read, write, past // LANES + 1, LANES, n_seq, k_sel, 1)


def _idx_sample(page_table, iq3, iw3, ikn3, pool):
    n_seq, n_pages = page_table.shape
    past = n_pages * PAGE_SIZE
    width = past + LANES
    k_sel = min(TOPK_MAX, (past + 1) // 4)
    grid_spec = pltpu.PrefetchScalarGridSpec(
        num_scalar_prefetch=1,
        grid=(n_seq,),
        in_specs=[pl.BlockSpec((1, IDX_HEADS, IDX_DIM), lambda b, pt: (b, 0, 0)),
                  pl.BlockSpec((1, IDX_HEADS, 1), lambda b, pt: (b, 0, 0)),
                  pl.BlockSpec((1, 1, IDX_DIM), lambda b, pt: (b, 0, 0)),
                  pl.BlockSpec(memory_space=pl.ANY)],
        out_specs=pl.BlockSpec((n_seq, width), lambda b, pt: (0, 0)),
        scratch_shapes=[pltpu.VMEM((2, past, IDX_DIM), F32),
                        pltpu.SemaphoreType.DMA((2,)),
                        pltpu.VMEM((n_seq, width), I32)])
    return pl.pallas_call(
        functools.partial(_idx_sample_body, n_seq=n_seq, n_pages=n_pages, k_sel=k_sel),
        grid_spec=grid_spec,
        out_shape=jax.ShapeDtypeStruct((n_seq, width), F32),
        compiler_params=_cp(1),
        name="idx_sample",
    )(page_table, iq3, iw3, ikn3, pool)


def _attn_a_sample_body(pt_ref, q_ref, bias_ref, kvn_ref, pool_ref, o_ref, kvbuf, sem, m_s, l_s, acc_s,
                        *, n_seq, n_steps, pg):
    b, s = pl.program_id(0), pl.program_id(1)
    t = b * n_steps + s
    slot = t % 2
    keys = pg * PAGE_SIZE
    past = n_steps * keys

    @pl.when(t == 0)
    def _():
        for cp in _page_copies(pool_ref, kvbuf, sem, pt_ref, 0, 0, pg, 0):
            cp.start()

    @pl.when(t + 1 < n_seq * n_steps)
    def _():
        wrap = s + 1 == n_steps
        nb = jnp.where(wrap, b + 1, b)
        ns = jnp.where(wrap, 0, s + 1)
        for cp in _page_copies(pool_ref, kvbuf, sem, pt_ref, nb, ns * pg, pg, 1 - slot):
            cp.start()

    for cp in _page_copies(pool_ref, kvbuf, sem, pt_ref, b, s * pg, pg, slot):
        cp.wait()

    @pl.when(s == 0)
    def _():
        m_s[...] = jnp.full(m_s.shape, M_INIT, F32)
        l_s[...] = jnp.zeros(l_s.shape, F32)
        acc_s[...] = jnp.zeros(acc_s.shape, F32)

    qb = q_ref[0]
    kv = kvbuf[slot]
    bias = bias_ref[0, :, pl.ds(pl.multiple_of(s * keys, keys), keys)]
    sc = _dot_nt(qb, kv[:, :KV_W].astype(BF)) + bias
    m_old = m_s[...]
    m_new = jnp.maximum(m_old, jnp.max(sc, axis=1, keepdims=True))
    alpha = jnp.exp(m_old - m_new)
    p = jnp.exp(sc - m_new)
    l_new = alpha * l_s[...] + jnp.sum(p, axis=1, keepdims=True)
    acc_new = alpha * acc_s[...] + _dot(p.astype(BF), kv[:, KV_W:].astype(BF))
    m_s[...] = m_new
    l_s[...] = l_new
    acc_s[...] = acc_new

    @pl.when(s == n_steps - 1)
    def _():
        kn = kvn_ref[0].astype(BF).astype(F32)
        b_new = bias_ref[0, :, past:past + LANES][:, 0:1]
        s_new = jnp.sum(qb.astype(F32) * kn[:, :KV_W], axis=1, keepdims=True) + b_new
        m_fin = jnp.maximum(m_new, s_new)
        a_fin = jnp.exp(m_new - m_fin)
        p_new = jnp.exp(s_new - m_fin)
        l_fin = a_fin * l_new + p_new
        o_ref[0] = (a_fin * acc_new + p_new.astype(BF).astype(F32) * kn[:, KV_W:]) / l_fin


def _attn_a_sample(page_table, qblk, bias3, kvn3, pool):
    n_seq, n_pages = page_table.shape
    pg = min(PAGES_PER_STEP, n_pages)
    assert n_pages % pg == 0
    n_steps = n_pages // pg
    width = bias3.shape[2]
    grid_spec = pltpu.PrefetchScalarGridSpec(
        num_scalar_prefetch=1,
        grid=(n_seq, n_steps),
        in_specs=[pl.BlockSpec((1, _SAMPLE_A_ROWS, KV_W), lambda b, s, pt: (b, 0, 0)),
                  pl.BlockSpec((1, 1, width), lambda b, s, pt: (b, 0, 0)),
                  pl.BlockSpec((1, 1, 2 * KV_W), lambda b, s, pt: (b, 0, 0)),
                  pl.BlockSpec(memory_space=pl.ANY)],
        out_specs=pl.BlockSpec((1, _SAMPLE_A_ROWS, KV_W), lambda b, s, pt: (b, 0, 0)),
        scratch_shapes=[pltpu.VMEM((2, pg * PAGE_SIZE, 2 * KV_W), F32),
                        pltpu.SemaphoreType.DMA((2,)),
                        pltpu.VMEM((_SAMPLE_A_ROWS, 1), F32),
                        pltpu.VMEM((_SAMPLE_A_ROWS, 1), F32),
                        pltpu.VMEM((_SAMPLE_A_ROWS, KV_W), F32)])
    return pl.pallas_call(
        functools.partial(_attn_a_sample_body, n_seq=n_seq, n_steps=n_steps, pg=pg),
        grid_spec=grid_spec,
        out_shape=jax.ShapeDtypeStruct((n_seq, _SAMPLE_A_ROWS, KV_W), F32),
        compiler_params=_cp(2),
        name="attn_a_sample",
    )(page_table, qblk, bias3, kvn3, pool)


def _attn_b_sample_body(q_ref, st0, st1, st2, kn0, kn1, kn2, o_ref):
    res = []
    for gi, (st_ref, kn_ref) in enumerate(((st0, kn0), (st1, kn1), (st2, kn2))):
        q = q_ref[0, gi * _SAMPLE_ROWS:(gi + 1) * _SAMPLE_ROWS, :]
        st = st_ref[0]
        kn = kn_ref[0].astype(BF).astype(F32)
        s = _dot_nt(q, st[:, :KV_W].astype(BF))
        s_new = jnp.sum(q.astype(F32) * kn[:, :KV_W], axis=1, keepdims=True)
        mx = jnp.maximum(jnp.max(s, axis=1, keepdims=True), s_new)
        e = jnp.exp(s - mx)
        e_new = jnp.exp(s_new - mx)
        den = jnp.sum(e, axis=1, keepdims=True) + e_new
        o = (_dot(e.astype(BF), st[:, KV_W:].astype(BF)) + e_new.astype(BF).astype(F32) * kn[:, KV_W:]) / den
        res.append((o, mx, den))
    mmax = jnp.maximum(jnp.maximum(res[0][1], res[1][1]), res[2][1])
    wgt = [r[2] * jnp.exp(r[1] - mmax) for r in res]
    num = wgt[0] * res[0][0] + wgt[1] * res[1][0] + wgt[2] * res[2][0]
    o_ref[0] = num / (wgt[0] + wgt[1] + wgt[2])


def _attn_b_sample(qblk3, states, kvns, layer):
    n_seq = qblk3.shape[0]
    in_specs = [pl.BlockSpec((1, len(B_GROUPS) * _SAMPLE_ROWS, KV_W), lambda b: (b, 0, 0))]
    views = []
    for st, (window, dil) in zip(states, B_GROUPS):
        assert st.shape[2] == window
        views.append(st.reshape(st.shape[0] * n_seq, window // dil, dil * 2 * KV_W))
        in_specs.append(pl.BlockSpec((1, window // dil, 2 * KV_W), lambda b: (layer * n_seq + b, 0, 0)))
    in_specs += [pl.BlockSpec((1, 1, 2 * KV_W), lambda b: (b, 0, 0))] * len(B_GROUPS)
    return pl.pallas_call(
        _attn_b_sample_body,
        grid=(n_seq,),
        in_specs=in_specs,
        out_specs=pl.BlockSpec((1, _SAMPLE_ROWS, KV_W), lambda b: (b, 0, 0)),
        out_shape=jax.ShapeDtypeStruct((n_seq, _SAMPLE_ROWS, KV_W), F32),
        compiler_params=_cp(1),
        name="attn_b_sample",
    )(qblk3, *views, *kvns)


def _memattn_sample_body(q_ref, mem_ref, o_ref):
    q = q_ref[0]
    mem = mem_ref[0]
    s = _dot_nt(q, mem[:, :KV_W].astype(BF))
    e = jnp.exp(s - jnp.max(s, axis=1, keepdims=True))
    p = e / jnp.sum(e, axis=1, keepdims=True)
    o_ref[0] = _dot(p.astype(BF), mem[:, KV_W:].astype(BF))


def _memattn_sample(mq3, mem_kv):
    n_seq, mem_len = mem_kv.shape[0], mem_kv.shape[1]
    return pl.pallas_call(
        _memattn_sample_body,
        grid=(n_seq,),
        in_specs=[pl.BlockSpec((1, _SAMPLE_ROWS, KV_W), lambda b: (b, 0, 0)),
                  pl.BlockSpec((1, mem_len, 2 * KV_W), lambda b: (b, 0, 0))],
        out_specs=pl.BlockSpec((1, _SAMPLE_ROWS, KV_W), lambda b: (b, 0, 0)),
        out_shape=jax.ShapeDtypeStruct((n_seq, _SAMPLE_ROWS, KV_W), F32),
        compiler_params=_cp(1),
        name="memattn_sample",
    )(mq3, mem_kv)


def kernel(x_prompt, x_sample, cache_kv_a, cache_kidx_a, state_kv_b1, state_kv_b2, state_kv_b3, cache_mem_kv, page_table, mem_prompt, ffn_norm, w_ffn_in, w_ffn_out, mix_norm, mem_norm, w_mem_kv, w_in_a, w_out_a, w_in_b, w_out_b, final_norm):
    batch, seq, d = x_prompt.shape
    n_seq = x_sample.shape[0]
    depth = ffn_norm.shape[0]
    n_pages = page_table.shape[1]
    past = n_pages * PAGE_SIZE
    mem_len = mem_prompt.shape[1]
    assert x_sample.shape[1] == 1 and seq % TM == 0
    states = (state_kv_b1, state_kv_b2, state_kv_b3)

    yp = x_prompt.reshape(batch * seq, d)
    ys = x_sample.reshape(n_seq, d)
    tab_p = _rope_tables(jnp.arange(seq))
    tab_s = _rope_tables(jnp.full((n_seq,), past, I32))
    n_pos_blocks = seq // TM
    w_in_bf = w_ffn_in.astype(BF)
    w_out_bf = w_ffn_out.astype(BF)
    mem_flat = mem_prompt.reshape(batch * mem_len, d)

    kv_a_p, kidx_a_p, kv_a_s, kidx_a_s, mem_p = [], [], [], [], []
    b_p = ([], [], [])
    b_s = ([], [], [])
    for i in range(depth):
        j = i // 2
        yp = _ffn(yp, ffn_norm[i, 0], w_in_bf[i, 0], w_out_bf[i, 0], TM)
        ys = _ffn(ys, ffn_norm[i, 0], w_in_bf[i, 0], w_out_bf[i, 0], n_seq)
        mkv = _proj_plain(mem_flat, mem_norm[i], w_mem_kv[i], min(TM, batch * mem_len))
        mem_p.append(mkv.reshape(batch, mem_len, 2, MEM_HEADS, HEAD_DIM))
        mkv3 = mkv.reshape(batch, mem_len, 2 * KV_W)
        mem_s3 = cache_mem_kv[i].reshape(n_seq, mem_len, 2 * KV_W)
        if i % 2 == 0:
            w = w_in_a[j]
            q_al, kv, k_bf, iq_al, ik, ik128, mq_al, v_t, iw_t = _proj_a_prompt(yp, mix_norm[i], w, tab_p, n_pos_blocks)
            o = _attn_a_prompt(q_al, iq_al, iw_t, k_bf, ik128, v_t, batch, seq)
            om = _memattn_prompt(mq_al, mkv3, seq, TM)
            n_o = A_HEADS * HEAD_DIM
            yp = _outproj(yp, [], [o, om], [w_out_a[j][:n_o], w_out_a[j][n_o:]], TM)
            kv_a_p.append(kv.reshape(batch, seq, 2, A_KV_HEADS, HEAD_DIM))
            kidx_a_p.append(ik.reshape(batch, seq, IDX_DIM))

            qblk, kvn, iq_s, ikn, iw_s, mq_blk = _proj_a_sample(ys, mix_norm[i], w, tab_s)
            bias = _idx_sample(page_table, iq_s.reshape(n_seq, IDX_HEADS, IDX_DIM),
                               iw_s[:, :IDX_HEADS].reshape(n_seq, IDX_HEADS, 1),
                               ikn.reshape(n_seq, 1, IDX_DIM), cache_kidx_a[j])
            pool = cache_kv_a[j].reshape(cache_kv_a.shape[1], PAGE_SIZE, 2 * KV_W)
            o_s = _attn_a_sample(page_table, qblk.reshape(n_seq, _SAMPLE_A_ROWS, KV_W),
                                 bias.reshape(n_seq, 1, past + LANES), kvn.reshape(n_seq, 1, 2 * KV_W), pool)
            om_s = _memattn_sample(mq_blk.reshape(n_seq, _SAMPLE_ROWS, KV_W), mem_s3)
            ys = _outproj(ys, [], [o_s.reshape(n_seq, -1), om_s.reshape(n_seq, -1)],
                          [_expand_out(w_out_a[j][:n_o], _A_Q_SLOT, _SAMPLE_A_ROWS),
                           _expand_out(w_out_a[j][n_o:], _MEM_SLOT, _SAMPLE_ROWS)], n_seq)
            kv_a_s.append(kvn.reshape(n_seq, 1, 2, A_KV_HEADS, HEAD_DIM))
            kidx_a_s.append(ikn.reshape(n_seq, 1, IDX_DIM))
        else:
            w = w_in_b[j]
            outs = _proj_b_prompt(yp, mix_norm[i], w, tab_p, n_pos_blocks)
            stats_o, stats_m, stats_d = [], [], []
            for gi, (window, dil) in enumerate(B_GROUPS):
                og, mg, dg = _attn_b_prompt(outs[2 * gi], outs[2 * gi + 1], batch, seq, window, dil)
                stats_o.append(og)
                stats_m.append(mg)
                stats_d.append(dg)
                kvg = outs[2 * gi + 1].reshape(batch, seq, 2, B_HEADS, HEAD_DIM)
                b_p[gi].append(kvg[:, seq - min(window, seq):])
            om = _memattn_prompt(outs[-1], mkv3, seq, TM)
            n_o = B_HEADS * HEAD_DIM
            yp = _outproj(yp, stats_o + stats_m + stats_d, [om], [w_out_b[j][:n_o], w_out_b[j][n_o:]], TM)

            souts = _proj_b_sample(ys, mix_norm[i], w, tab_s)
            kvns = [souts[1 + gi].reshape(n_seq, 1, 2 * KV_W) for gi in range(len(B_GROUPS))]
            o_s = _attn_b_sample(souts[0].reshape(n_seq, len(B_GROUPS) * _SAMPLE_ROWS, KV_W), states, kvns, j)
            om_s = _memattn_sample(souts[-1].reshape(n_seq, _SAMPLE_ROWS, KV_W), mem_s3)
            ys = _outproj(ys, [], [o_s.reshape(n_seq, -1), om_s.reshape(n_seq, -1)],
                          [_expand_out(w_out_b[j][:n_o], _B_SLOT, _SAMPLE_ROWS),
                           _expand_out(w_out_b[j][n_o:], _MEM_SLOT, _SAMPLE_ROWS)], n_seq)
            for gi in range(len(B_GROUPS)):
                new_row = souts[1 + gi].reshape(n_seq, 1, 2, B_HEADS, HEAD_DIM)
                b_s[gi].append(jnp.concatenate([states[gi][j][:, 1:], new_row], axis=1))
        yp = _ffn(yp, ffn_norm[i, 1], w_in_bf[i, 1], w_out_bf[i, 1], TM)
        ys = _ffn(ys, ffn_norm[i, 1], w_in_bf[i, 1], w_out_bf[i, 1], n_seq)

    y_prompt = _final_norm(yp, final_norm, TM).reshape(batch, seq, d)
    y_sample = _final_norm(ys, final_norm, n_seq).reshape(n_seq, 1, d)
    return (y_prompt, y_sample, jnp.stack(kv_a_p), jnp.stack(kidx_a_p), jnp.stack(b_p[0]), jnp.stack(b_p[1]),
            jnp.stack(b_p[2]), jnp.stack(mem_p), jnp.stack(kv_a_s), jnp.stack(kidx_a_s), jnp.stack(b_s[0]),
            jnp.stack(b_s[1]), jnp.stack(b_s[2]))
```

```python
import functools

import jax
import jax.numpy as jnp
import numpy as np
from jax import lax
from jax.experimental import pallas as pl
from jax.experimental.pallas import tpu as pltpu

F32 = jnp.float32
BF = jnp.bfloat16
I32 = jnp.int32

HEAD_DIM = 64
ROT_DIM = HEAD_DIM // 4
ROT_HALF = ROT_DIM // 2
ROPE_THETA = 500000.0
ATTN_SCALE = HEAD_DIM ** -0.5
A_HEADS = 12
A_KV_HEADS = 4
A_GROUP = A_HEADS // A_KV_HEADS
IDX_HEADS = 8
IDX_DIM = 64
IDX_SCALE = (IDX_HEADS * IDX_DIM) ** -0.5
TOPK_MAX = 256
B_GROUPS = ((128, 1), (512, 4), (2048, 16))
B_HEADS = 4
MEM_HEADS = 4
EPS = 1e-6
PAGE_SIZE = 128

LANES = 128
SUBLANES = 8
PAIR = 2 * HEAD_DIM
KV_W = A_KV_HEADS * HEAD_DIM
VMEM_LIMIT_V7X = 56 * 2 ** 20
NEG_BIG = -1e30
M_INIT = -3e38
INT_MAX = 2 ** 31 - 1
KEY_NEG_INF = -2139095041
MAX_SEARCH_PASSES = 70

TM = 512
TQ = 256
CLS = 128
PAGES_PER_STEP = 16
ONES_ROWS = 16


def _cp(n_axes):
    return pltpu.CompilerParams(dimension_semantics=("arbitrary",) * n_axes,
                                vmem_limit_bytes=VMEM_LIMIT_V7X)


def _dot(a, b):
    return jnp.dot(a, b, preferred_element_type=F32)


def _dot_nt(a, b):
    return lax.dot_general(a, b, (((1,), (1,)), ((), ())), preferred_element_type=F32)


def _rms(x, g):
    return x * lax.rsqrt(jnp.mean(x * x, axis=-1, keepdims=True) + EPS) * g


def _sort_key(x):
    b = pltpu.bitcast(x + 0.0, I32)
    return b ^ ((b >> 31) & 0x7FFFFFFF)


def _resident(shape):
    return pl.BlockSpec(shape, lambda *_: (0,) * len(shape), pipeline_mode=pl.Buffered(1))


def _ffn_body(y_ref, g_ref, win_ref, wout_ref, o_ref, *, d_ff, fc):
    x = y_ref[...]
    xn = _rms(x, g_ref[...]).astype(BF)
    acc = jnp.zeros(x.shape, F32)
    for c in range(d_ff // fc):
        a = _dot(xn, win_ref[:, c * fc:(c + 1) * fc])
        u = _dot(xn, win_ref[:, d_ff + c * fc:d_ff + (c + 1) * fc])
        h = (a * jax.nn.sigmoid(a) * u).astype(BF)
        acc = acc + _dot(h, wout_ref[c * fc:(c + 1) * fc, :])
    o_ref[...] = x + 0.5 * acc


def _ffn(y, g, w_in_all, w_out_all, layer, half, tm):
    m, d = y.shape
    d_ff = w_out_all.shape[2]
    fc = 256 if d_ff % 256 == 0 else LANES
    assert d_ff % fc == 0 and m % tm == 0
    return pl.pallas_call(
        functools.partial(_ffn_body, d_ff=d_ff, fc=fc),
        grid=(m // tm,),
        in_specs=[pl.BlockSpec((tm, d), lambda i: (i, 0)),
                  pl.BlockSpec((1, d), lambda i: (0, 0)),
                  pl.BlockSpec((None, None, d, 2 * d_ff), lambda i: (layer, half, 0, 0),
                               pipeline_mode=pl.Buffered(1)),
                  pl.BlockSpec((None, None, d_ff, d), lambda i: (layer, half, 0, 0),
                               pipeline_mode=pl.Buffered(1))],
        out_specs=pl.BlockSpec((tm, d), lambda i: (i, 0)),
        out_shape=jax.ShapeDtypeStruct((m, d), F32),
        compiler_params=_cp(1),
        name="ffn",
    )(y, g.reshape(1, d), w_in_all, w_out_all)


def _rope_tables(pos):
    inv = jnp.power(ROPE_THETA, -jnp.arange(ROT_HALF, dtype=F32) / ROT_HALF)
    ang = pos.astype(F32)[:, None] * inv[None, :]
    cos, sin = jnp.cos(ang), jnp.sin(ang)
    t = pos.shape[0]
    ones = jnp.ones((t, HEAD_DIM - ROT_DIM), F32)
    zeros = jnp.zeros((t, HEAD_DIM - ROT_DIM), F32)
    zh = jnp.zeros((t, ROT_HALF), F32)
    cos64 = jnp.concatenate([cos, cos, ones], axis=1)
    sa64 = jnp.concatenate([-sin, zh, zeros], axis=1)
    sb64 = jnp.concatenate([zh, sin, zeros], axis=1)
    row = tuple(jnp.tile(a, (1, LANES // HEAD_DIM)) for a in (cos64, sa64, sb64))
    return row + (cos.T, sin.T)


def _proj_body(*refs, segs, nts, n_in):
    y_ref, g_ref, cos_ref, sa_ref, sb_ref, cost_ref, sint_ref, w_ref = refs[:8]
    wt_refs = refs[8:n_in]
    outs = refs[n_in:]
    xn = _rms(y_ref[...], g_ref[...]).astype(BF)
    for lo, hi, rope, scale, writes in segs:
        p = _dot(xn, w_ref[:, lo:hi])
        if rope:
            cos, sa, sb = cos_ref[...], sa_ref[...], sb_ref[...]
            chunks = []
            for c in range((hi - lo) // LANES):
                pc = p[:, c * LANES:(c + 1) * LANES]
                chunks.append(pc * cos + pltpu.roll(pc, LANES - ROT_HALF, 1) * sa
                              + pltpu.roll(pc, ROT_HALF, 1) * sb)
            p = chunks[0] if len(chunks) == 1 else jnp.concatenate(chunks, axis=1)
        if scale != 1.0:
            p = p * scale
        for oi, slo, shi, dlo in writes:
            if len(outs[oi].shape) == 3:
                for c in range((shi - slo) // LANES):
                    outs[oi][dlo // LANES + c] = p[:, slo + c * LANES:slo + (c + 1) * LANES].astype(outs[oi].dtype)
            else:
                outs[oi][:, dlo:dlo + (shi - slo)] = p[:, slo:shi].astype(outs[oi].dtype)
    for t, (rope_rows, scale, writes) in enumerate(nts):
        r = _dot_nt(wt_refs[t][...], xn)
        if rope_rows:
            ct, st = cost_ref[...], sint_ref[...]
            pieces = []
            for h0 in range(0, rope_rows, HEAD_DIM):
                x1, x2 = r[h0:h0 + ROT_HALF], r[h0 + ROT_HALF:h0 + ROT_DIM]
                pieces += [x1 * ct - x2 * st, x2 * ct + x1 * st, r[h0 + ROT_DIM:h0 + HEAD_DIM]]
            if rope_rows < r.shape[0]:
                pieces.append(r[rope_rows:])
            r = jnp.concatenate(pieces, axis=0)
        if scale != 1.0:
            r = r * scale
        for oi, slo, shi, dlo in writes:
            outs[oi][dlo:dlo + (shi - slo), :] = r[slo:shi].astype(outs[oi].dtype)


def _proj(y, g, tables, n_seqs, w, wts, segs, nts, out_defs, tm):
    m, d = y.shape
    assert m % tm == 0
    per_seq = m // n_seqs // tm if tables[0].shape[0] != tm else 1
    n_in = 8 + len(wts)
    pos_row = lambda i: (i % per_seq, 0)
    pos_col = lambda i: (0, i % per_seq)
    in_specs = [pl.BlockSpec((tm, d), lambda i: (i, 0)),
                pl.BlockSpec((1, d), lambda i: (0, 0)),
                pl.BlockSpec((tm, LANES), pos_row),
                pl.BlockSpec((tm, LANES), pos_row),
                pl.BlockSpec((tm, LANES), pos_row),
                pl.BlockSpec((ROT_HALF, tm), pos_col),
                pl.BlockSpec((ROT_HALF, tm), pos_col),
                _resident(w.shape)] + [_resident(wt.shape) for wt in wts]
    out_specs, out_shapes = [], []
    for kind, size, dtype in out_defs:
        if kind == "row":
            out_specs.append(pl.BlockSpec((tm, size), lambda i: (i, 0)))
            out_shapes.append(jax.ShapeDtypeStruct((m, size), dtype))
        elif kind == "chunks":
            out_specs.append(pl.BlockSpec((size // LANES, tm, LANES), lambda i: (0, i, 0)))
            out_shapes.append(jax.ShapeDtypeStruct((size // LANES, m, LANES), dtype))
        elif kind == "col":
            out_specs.append(pl.BlockSpec((size, tm), lambda i: (0, i)))
            out_shapes.append(jax.ShapeDtypeStruct((size, m), dtype))
        else:
            out_specs.append(pl.BlockSpec((None, size, tm), lambda i: (i // per_seq, 0, i % per_seq)))
            out_shapes.append(jax.ShapeDtypeStruct((n_seqs, size, m // n_seqs), dtype))
    return pl.pallas_call(
        functools.partial(_proj_body, segs=tuple(segs), nts=tuple(nts), n_in=n_in),
        grid=(m // tm,),
        in_specs=in_specs,
        out_specs=out_specs,
        out_shape=out_shapes,
        compiler_params=_cp(1),
        name="proj",
    )(y, g.reshape(1, d), *tables, w, *wts)


def _heads(w, lo, n_heads):
    return w[:, lo:lo + n_heads * HEAD_DIM].reshape(w.shape[0], n_heads, HEAD_DIM)


def _aligned(wh, halves):
    d, h, _ = wh.shape
    z = jnp.zeros_like(wh)
    sel = jnp.asarray(halves, I32).reshape(1, h, 1)
    lo = jnp.where(sel == 0, wh, z)
    hi = jnp.where(sel == 1, wh, z)
    return jnp.concatenate([lo, hi], axis=2).reshape(d, h * PAIR)


def _blocked(wh, slots, n_rows):
    d, h, _ = wh.shape
    onehot = jnp.asarray(np.eye(A_KV_HEADS, dtype=np.float32)[np.asarray(slots)], wh.dtype)
    blk = (wh[:, :, None, :] * onehot[None, :, :, None]).reshape(d, h * KV_W)
    return jnp.pad(blk, ((0, 0), (0, (n_rows - h) * KV_W)))


def _expand_out(w_rows, slots, n_rows):
    h = len(slots)
    d = w_rows.shape[1]
    wh = w_rows.reshape(h, HEAD_DIM, d)
    onehot = jnp.asarray(np.eye(A_KV_HEADS, dtype=np.float32)[np.asarray(slots)], w_rows.dtype)
    blk = (wh[:, None, :, :] * onehot[:, :, None, None]).reshape(h * KV_W, d)
    return jnp.pad(blk, ((0, (n_rows - h) * KV_W), (0, 0)))


_A_OFF = np.cumsum([0, A_HEADS * HEAD_DIM, KV_W, KV_W, IDX_HEADS * IDX_DIM, IDX_DIM, IDX_HEADS]).tolist()
_A_Q_HALF = [(h // A_GROUP) % 2 for h in range(A_HEADS)]
_A_Q_SLOT = [h // A_GROUP for h in range(A_HEADS)]
_MEM_HALF = [h % 2 for h in range(MEM_HEADS)]
_MEM_SLOT = list(range(MEM_HEADS))
_B_HALF = [h % 2 for h in range(B_HEADS)]
_B_SLOT = list(range(B_HEADS))
_SAMPLE_A_ROWS = 16
_SAMPLE_ROWS = 8


def _segments(items):
    segs, cols, lo = [], [], 0
    for wc, rope, scale, writes in items:
        hi = lo + wc.shape[1]
        segs.append((lo, hi, rope, scale, tuple(writes)))
        cols.append(wc)
        lo = hi
    return jnp.concatenate(cols, axis=1).astype(BF), segs


def _proj_a_prompt(y, g, w, tables, batch):
    o = _A_OFF
    zpad = jnp.zeros((w.shape[0], LANES - IDX_DIM), w.dtype)
    items = [
        (_aligned(_heads(w, o[0], A_HEADS), _A_Q_HALF), True, ATTN_SCALE, [(0, 0, A_HEADS * PAIR, 0)]),
        (w[:, o[1]:o[2]], True, 1.0, [(1, 0, KV_W, 0)]),
        (_aligned(_heads(w, o[3], IDX_HEADS), [0] * IDX_HEADS), True, 1.0, [(2, 0, IDX_HEADS * PAIR, 0)]),
        (jnp.concatenate([w[:, o[4]:o[5]], zpad], axis=1), True, 1.0, [(3, 0, LANES, 0)]),
        (_aligned(_heads(w, o[6], MEM_HEADS), _MEM_HALF), False, ATTN_SCALE, [(4, 0, MEM_HEADS * PAIR, 0)]),
    ]
    wcat, segs = _segments(items)
    wts = [w[:, o[1]:o[3]].T.astype(BF), w[:, o[4]:o[5]].T.astype(BF), w[:, o[5]:o[6]].T.astype(BF)]
    nts = [(KV_W, 1.0, ((5, 0, 2 * KV_W, 0), (6, KV_W, 2 * KV_W, 0))),
           (IDX_DIM, 1.0, ((7, 0, IDX_DIM, 0),)),
           (0, IDX_SCALE, ((8, 0, IDX_HEADS, 0),))]
    out_defs = [("row", A_HEADS * PAIR, BF), ("row", KV_W, BF), ("row", IDX_HEADS * PAIR, BF),
                ("row", LANES, BF), ("row", MEM_HEADS * PAIR, BF), ("bcol", 2 * KV_W, F32),
                ("col", KV_W, BF), ("bcol", IDX_DIM, F32), ("col", IDX_HEADS, F32)]
    return _proj(y, g, tables, batch, wcat, wts, segs, nts, out_defs, TM)


def _proj_a_sample(y, g, w, tables):
    o = _A_OFF
    zpad = jnp.zeros((w.shape[0], LANES - IDX_DIM), w.dtype)
    wpad = jnp.zeros((w.shape[0], LANES - IDX_HEADS), w.dtype)
    items = [
        (_blocked(_heads(w, o[0], A_HEADS), _A_Q_SLOT, _SAMPLE_A_ROWS), True, ATTN_SCALE,
         [(0, 0, _SAMPLE_A_ROWS * KV_W, 0)]),
        (w[:, o[1]:o[2]], True, 1.0, [(1, 0, KV_W, 0)]),
        (w[:, o[2]:o[3]], False, 1.0, [(1, 0, KV_W, KV_W)]),
        (w[:, o[3]:o[4]], True, 1.0, [(2, 0, IDX_HEADS * IDX_DIM, 0)]),
        (jnp.concatenate([w[:, o[4]:o[5]], zpad], axis=1), True, 1.0, [(3, 0, IDX_DIM, 0)]),
        (jnp.concatenate([w[:, o[5]:o[6]], wpad], axis=1), False, IDX_SCALE, [(4, 0, LANES, 0)]),
        (_blocked(_heads(w, o[6], MEM_HEADS), _MEM_SLOT, _SAMPLE_ROWS), False, ATTN_SCALE,
         [(5, 0, _SAMPLE_ROWS * KV_W, 0)]),
    ]
    wcat, segs = _segments(items)
    out_defs = [("row", _SAMPLE_A_ROWS * KV_W, BF), ("row", 2 * KV_W, F32), ("row", IDX_HEADS * IDX_DIM, BF),
                ("row", IDX_DIM, F32), ("row", LANES, F32), ("row", _SAMPLE_ROWS * KV_W, BF)]
    return _proj(y, g, tables, 1, wcat, [], segs, [], out_defs, y.shape[0])


def _b_off(g, t):
    return (g * 3 + t) * B_HEADS * HEAD_DIM


def _proj_b_prompt(y, g, w, tables, batch):
    ng = len(B_GROUPS)
    items, out_defs, wts, nts = [], [], [], []
    for gi in range(ng):
        items += [
            (_aligned(_heads(w, _b_off(gi, 0), B_HEADS), _B_HALF), True, ATTN_SCALE,
             [(2 * gi, 0, B_HEADS * PAIR, 0)]),
            (w[:, _b_off(gi, 1):_b_off(gi, 1) + KV_W], True, 1.0, [(2 * gi + 1, 0, KV_W, 0)]),
            (w[:, _b_off(gi, 2):_b_off(gi, 2) + KV_W], False, 1.0, [(2 * gi + 1, 0, KV_W, KV_W)]),
        ]
        out_defs += [("chunks", B_HEADS * PAIR, F32), ("chunks", 2 * KV_W, F32)]
    items.append((_aligned(_heads(w, _b_off(ng, 0), MEM_HEADS), _MEM_HALF), False, ATTN_SCALE,
                  [(2 * ng, 0, MEM_HEADS * PAIR, 0)]))
    out_defs.append(("row", MEM_HEADS * PAIR, BF))
    for gi in range(ng):
        wts.append(w[:, _b_off(gi, 1):_b_off(gi, 1) + 2 * KV_W].T.astype(BF))
        nts.append((KV_W, 1.0, ((2 * ng + 1 + gi, 0, 2 * KV_W, 0),)))
        out_defs.append(("bcol", 2 * KV_W, F32))
    wcat, segs = _segments(items)
    return _proj(y, g, tables, batch, wcat, wts, segs, nts, out_defs, TM)


def _proj_b_sample(y, g, w, tables):
    ng = len(B_GROUPS)
    items, out_defs, wts, nts = [], [("row", ng * _SAMPLE_ROWS * KV_W, BF)], [], []
    for gi in range(ng):
        items += [
            (_blocked(_heads(w, _b_off(gi, 0), B_HEADS), _B_SLOT, _SAMPLE_ROWS), True, ATTN_SCALE,
             [(0, 0, _SAMPLE_ROWS * KV_W, gi * _SAMPLE_ROWS * KV_W)]),
            (w[:, _b_off(gi, 1):_b_off(gi, 1) + KV_W], True, 1.0, [(1 + gi, 0, KV_W, 0)]),
            (w[:, _b_off(gi, 2):_b_off(gi, 2) + KV_W], False, 1.0, [(1 + gi, 0, KV_W, KV_W)]),
        ]
        out_defs.append(("row", 2 * KV_W, F32))
    items.append((_blocked(_heads(w, _b_off(ng, 0), MEM_HEADS), _MEM_SLOT, _SAMPLE_ROWS), False, ATTN_SCALE,
                  [(1 + ng, 0, _SAMPLE_ROWS * KV_W, 0)]))
    out_defs.append(("row", _SAMPLE_ROWS * KV_W, BF))
    for gi in range(ng):
        wts.append(w[:, _b_off(gi, 1):_b_off(gi, 1) + 2 * KV_W].T.astype(BF))
        nts.append((KV_W, 1.0, ((2 + ng + gi, 0, 2 * KV_W, 0),)))
        out_defs.append(("col", 2 * KV_W, F32))
    wcat, segs = _segments(items)
    return _proj(y, g, tables, 1, wcat, wts, segs, nts, out_defs, y.shape[0])


def _memproj_body(y_ref, g_ref, wt_ref, o_ref):
    xn = _rms(y_ref[...], g_ref[...]).astype(BF)
    o_ref[...] = _dot_nt(wt_ref[...], xn)


def _memproj(mem_flat, g, w, batch):
    m, d = mem_flat.shape
    mem_len = m // batch
    n = w.shape[1]
    return pl.pallas_call(
        _memproj_body,
        grid=(batch,),
        in_specs=[pl.BlockSpec((mem_len, d), lambda b: (b, 0)),
                  pl.BlockSpec((1, d), lambda b: (0, 0)),
                  _resident((n, d))],
        out_specs=pl.BlockSpec((None, n, mem_len), lambda b: (b, 0, 0)),
        out_shape=jax.ShapeDtypeStruct((batch, n, mem_len), F32),
        compiler_params=_cp(1),
        name="memproj",
    )(mem_flat, g.reshape(1, d), w.T.astype(BF))


def _select_bias(count, read, write, n_chunks, rows, k_sel, red_shape, key_axis):
    lo0 = KEY_NEG_INF + 1
    c_valid = count(lambda x: x >= lo0)
    c_zero = count(lambda x: x >= 0)
    pos = c_zero >= k_sel
    few = c_valid <= k_sel
    lo = jnp.where(pos, 0, lo0).astype(I32)
    hi = jnp.where(pos, INT_MAX, 0).astype(I32)
    c_lo = jnp.where(pos, c_zero, c_valid)
    c_hi = jnp.where(pos, 0, c_zero)
    done = jnp.where(jnp.logical_or(few, c_zero == k_sel), 1, 0).astype(I32)
    thr = jnp.where(few, lo0, 0).astype(I32)

    def cond(st):
        return jnp.logical_and(st[0] < MAX_SEARCH_PASSES, st[1] == 0)

    def body(st):
        it, _, lo, hi, c_lo, c_hi, thr, done = st
        width = hi - lo
        frac = ((c_lo - k_sel).astype(F32) + 0.5) / jnp.maximum(c_lo - c_hi, 1).astype(F32)
        step = jnp.where(it % 2 == 0, (width.astype(F32) * frac).astype(I32), width >> 1)
        step = jnp.minimum(jnp.maximum(step, 1), jnp.maximum(width - 1, 1))
        cand = lo + step
        c = count(lambda x: x >= cand)
        hit = c == k_sel
        up = c > k_sel
        lo_n = jnp.where(up, cand, lo)
        hi_n = jnp.where(up, hi, cand)
        newly = jnp.logical_and(done == 0, jnp.logical_or(hit, hi_n - lo_n == 1))
        thr_n = jnp.where(newly, jnp.where(hit, cand, lo_n), thr)
        done_n = jnp.where(newly, 1, done)
        return (it + 1, jnp.min(done_n), lo_n, hi_n, jnp.where(up, c, c_lo), jnp.where(up, c_hi, c),
                thr_n, done_n)

    st = lax.while_loop(cond, body, (jnp.int32(0), jnp.min(done), lo, hi, c_lo, c_hi, thr, done))
    thr = st[6]
    n_gt = count(lambda x: x > thr)
    n_ge = count(lambda x: x >= thr)
    has_tie = jnp.max(n_ge) > k_sel

    @pl.when(jnp.logical_not(has_tie))
    def _():
        def wbody(c, _):
            x = read(c)
            sel = jnp.logical_and(x >= thr, x > KEY_NEG_INF)
            write(c, jnp.where(sel, 0.0, NEG_BIG).astype(F32))
            return 0
        lax.fori_loop(0, n_chunks, wbody, 0)

    @pl.when(has_tie)
    def _():
        need = (k_sel - n_gt).astype(F32)
        r = lax.broadcasted_iota(I32, (rows, rows), 0)
        cidx = lax.broadcasted_iota(I32, (rows, rows), 1)
        if key_axis == 0:
            tri = jnp.where(cidx < r, 1.0, 0.0).astype(BF)
        else:
            tri = jnp.where(r < cidx, 1.0, 0.0).astype(BF)

        def wbody(c, seen):
            x = read(c)
            eq = jnp.logical_and(x == thr, x > KEY_NEG_INF)
            eqf = jnp.where(eq, 1.0, 0.0).astype(BF)
            rank = (_dot(tri, eqf) if key_axis == 0 else _dot(eqf, tri)) + seen
            sel = jnp.logical_and(jnp.logical_or(x > thr, jnp.logical_and(eq, rank < need)), x > KEY_NEG_INF)
            write(c, jnp.where(sel, 0.0, NEG_BIG).astype(F32))
            return seen + jnp.sum(eqf.astype(F32), axis=key_axis, keepdims=True)
        lax.fori_loop(0, n_chunks, wbody, jnp.zeros(red_shape, F32))


def _attn_a_body(q_ref, iq_ref, iw_ref, k_ref, ik_ref, vt_ref, o_ref, sc_ref, m_ref, l_ref, acc_ref,
                 *, tq, k_sel):
    qi = pl.program_id(1)
    iw = iw_ref[...]

    def idx_chunk(r0, diag):
        ikc = ik_ref[pl.ds(r0, tq), :]
        acc = jnp.zeros((tq, tq), F32)
        for h in range(IDX_HEADS):
            s = _dot_nt(ikc, iq_ref[:, h * PAIR:(h + 1) * PAIR])
            acc = acc + jnp.maximum(s, 0.0) * iw[h:h + 1, :]
        if diag:
            kpos = lax.broadcasted_iota(I32, (tq, tq), 0)
            qpos = lax.broadcasted_iota(I32, (tq, tq), 1)
            acc = jnp.where(kpos <= qpos, acc, -jnp.inf)
        sc_ref[pl.ds(r0, tq), :] = _sort_key(acc)

    def full_chunk(c, _):
        idx_chunk(pl.multiple_of(c * tq, tq), False)
        return 0
    lax.fori_loop(0, qi, full_chunk, 0)
    idx_chunk(pl.multiple_of(qi * tq, tq), True)

    def read(c):
        return sc_ref[pl.ds(pl.multiple_of(c * tq, tq), tq), :]

    def write(c, v):
        sc_ref[pl.ds(pl.multiple_of(c * tq, tq), tq), :] = pltpu.bitcast(v, I32)

    def count(pred):
        def body(c, cnt):
            ones = jnp.where(pred(read(c)), 1, 0).astype(I32)
            return cnt + jnp.sum(ones.reshape(tq // SUBLANES, SUBLANES, tq), axis=0)
        part = lax.fori_loop(0, qi + 1, body, jnp.zeros((SUBLANES, tq), I32))
        return jnp.sum(part, axis=0, keepdims=True)

    _select_bias(count, read, write, qi + 1, tq, k_sel, (1, tq), 0)

    m_ref[...] = jnp.full(m_ref.shape, M_INIT, F32)
    l_ref[...] = jnp.zeros(l_ref.shape, F32)
    acc_ref[...] = jnp.zeros(acc_ref.shape, F32)
    ones_rows = jnp.ones((ONES_ROWS, tq), BF)

    def att_chunk(c, _):
        r0 = pl.multiple_of(c * tq, tq)
        bias = pltpu.bitcast(sc_ref[pl.ds(r0, tq), :], F32)
        for h in range(A_HEADS):
            n = h // A_GROUP
            pair = n // 2
            kc = k_ref[pl.ds(r0, tq), pair * PAIR:(pair + 1) * PAIR]
            s = _dot_nt(kc, q_ref[:, h * PAIR:(h + 1) * PAIR]) + bias
            m_old = m_ref[h:h + 1, :]
            m_new = jnp.maximum(m_old, jnp.max(s, axis=0, keepdims=True))
            alpha = jnp.exp(m_old - m_new)
            p = jnp.exp(s - m_new).astype(BF)
            vt = vt_ref[n * HEAD_DIM:(n + 1) * HEAD_DIM, pl.ds(r0, tq)]
            res = _dot(jnp.concatenate([vt, ones_rows], axis=0), p)
            rows = slice(h * HEAD_DIM, (h + 1) * HEAD_DIM)
            m_ref[h:h + 1, :] = m_new
            l_ref[h:h + 1, :] = alpha * l_ref[h:h + 1, :] + res[HEAD_DIM:HEAD_DIM + 1]
            acc_ref[rows, :] = alpha * acc_ref[rows, :] + res[:HEAD_DIM]
        return 0
    lax.fori_loop(0, qi + 1, att_chunk, 0)

    for hp in range(A_HEADS // 2):
        outs = [acc_ref[h * HEAD_DIM:(h + 1) * HEAD_DIM, :] / l_ref[h:h + 1, :] for h in (2 * hp, 2 * hp + 1)]
        o_ref[:, hp * PAIR:(hp + 1) * PAIR] = jnp.concatenate(outs, axis=0).T.astype(o_ref.dtype)


def _attn_a_prompt(q_al, iq_al, iw_t, k_bf, ik128, v_t, batch, seq):
    tq = min(TQ, seq)
    nq = seq // tq
    m = batch * seq
    k_sel = min(TOPK_MAX, seq // 4)
    return pl.pallas_call(
        functools.partial(_attn_a_body, tq=tq, k_sel=k_sel),
        grid=(batch, nq),
        in_specs=[pl.BlockSpec((tq, A_HEADS * PAIR), lambda b, i: (b * nq + i, 0)),
                  pl.BlockSpec((tq, IDX_HEADS * PAIR), lambda b, i: (b * nq + i, 0)),
                  pl.BlockSpec((IDX_HEADS, tq), lambda b, i: (0, b * nq + i)),
                  pl.BlockSpec((seq, KV_W), lambda b, i: (b, 0)),
                  pl.BlockSpec((seq, LANES), lambda b, i: (b, 0)),
                  pl.BlockSpec((KV_W, seq), lambda b, i: (0, b))],
        out_specs=pl.BlockSpec((tq, A_HEADS * HEAD_DIM), lambda b, i: (b * nq + i, 0)),
        out_shape=jax.ShapeDtypeStruct((m, A_HEADS * HEAD_DIM), BF),
        scratch_shapes=[pltpu.VMEM((seq, tq), I32),
                        pltpu.VMEM((2 * SUBLANES, tq), F32),
                        pltpu.VMEM((2 * SUBLANES, tq), F32),
                        pltpu.VMEM((A_HEADS * HEAD_DIM, tq), F32)],
        compiler_params=_cp(2),
        name="attn_a_prompt",
    )(q_al, iq_al, iw_t, k_bf, ik128, v_t)


def _attn_b_body(q_ref, kvc_ref, kvp_ref, o_ref, m_ref, d_ref, *, dil, span):
    i = pl.program_id(1)
    qrow = lax.broadcasted_iota(I32, (CLS, 2 * CLS), 0)
    kcol = lax.broadcasted_iota(I32, (CLS, 2 * CLS), 1)
    rel = qrow + CLS - kcol
    valid = jnp.logical_and(jnp.logical_and(rel >= 0, rel <= span), jnp.logical_or(kcol >= CLS, i > 0))
    lo_half = lax.broadcasted_iota(I32, (CLS, PAIR), 1) < HEAD_DIM

    n_pairs = B_HEADS // 2

    def one_class(rows):
        for pair in range(n_pairs):
            kp = jnp.concatenate([kvp_ref[pair, rows, :], kvc_ref[pair, rows, :]], axis=0).astype(BF)
            vp = jnp.concatenate([kvp_ref[n_pairs + pair, rows, :], kvc_ref[n_pairs + pair, rows, :]],
                                 axis=0).astype(BF)
            res = []
            for h in (2 * pair, 2 * pair + 1):
                s = jnp.where(valid, _dot_nt(q_ref[h, rows, :].astype(BF), kp), -jnp.inf)
                mx = jnp.max(s, axis=1, keepdims=True)
                e = jnp.exp(s - mx)
                den = jnp.sum(e, axis=1, keepdims=True)
                res.append((_dot(e.astype(BF), vp) / den, mx, den))
            o_ref[pair, rows, :] = jnp.where(lo_half, res[0][0], res[1][0])
            m_ref[pair, rows, :] = jnp.where(lo_half, res[0][1], res[1][1])
            d_ref[pair, rows, :] = jnp.where(lo_half, res[0][2], res[1][2])

    if dil == 1:
        one_class(pl.ds(0, CLS))
    else:
        def body(r, _):
            one_class(pl.ds(r, CLS, stride=dil))
            return 0
        lax.fori_loop(0, dil, body, 0)


def _attn_b_prompt(q_al, kv, batch, seq, window, dil):
    rows = CLS * dil
    span = window // dil
    assert span <= CLS and seq % rows == 0
    nr = seq // rows
    m = batch * seq
    n_pairs = B_HEADS // 2
    stat = jax.ShapeDtypeStruct((n_pairs, m, PAIR), F32)
    ospec = pl.BlockSpec((n_pairs, rows, PAIR), lambda b, i: (0, b * nr + i, 0))
    return pl.pallas_call(
        functools.partial(_attn_b_body, dil=dil, span=span),
        grid=(batch, nr),
        in_specs=[pl.BlockSpec((B_HEADS, rows, PAIR), lambda b, i: (0, b * nr + i, 0)),
                  pl.BlockSpec((2 * n_pairs, rows, PAIR), lambda b, i: (0, b * nr + i, 0)),
                  pl.BlockSpec((2 * n_pairs, rows, PAIR), lambda b, i: (0, b * nr + jnp.maximum(i - 1, 0), 0))],
        out_specs=[ospec, ospec, ospec],
        out_shape=[stat, stat, stat],
        compiler_params=_cp(2),
        name="attn_b_prompt",
    )(q_al, kv, kv)


def _memattn_body(mq_ref, mem_ref, o_ref):
    mq = mq_ref[...]
    mem = mem_ref[...].astype(BF)
    lo_half = lax.broadcasted_iota(I32, (mq.shape[0], PAIR), 1) < HEAD_DIM
    outs = []
    for pair in range(MEM_HEADS // 2):
        mk = mem[pair * PAIR:(pair + 1) * PAIR]
        mv = mem[KV_W + pair * PAIR:KV_W + (pair + 1) * PAIR]
        res = []
        for h in (2 * pair, 2 * pair + 1):
            s = _dot(mq[:, h * PAIR:(h + 1) * PAIR], mk)
            e = jnp.exp(s - jnp.max(s, axis=1, keepdims=True))
            p = e / jnp.sum(e, axis=1, keepdims=True)
            res.append(_dot_nt(p.astype(BF), mv))
        outs.append(jnp.where(lo_half, res[0], res[1]))
    o_ref[...] = jnp.concatenate(outs, axis=1).astype(o_ref.dtype)


def _memattn_prompt(mq_al, mem_t, seq, tm):
    m = mq_al.shape[0]
    mem_len = mem_t.shape[2]
    tiles_per_seq = seq // tm
    return pl.pallas_call(
        _memattn_body,
        grid=(m // tm,),
        in_specs=[pl.BlockSpec((tm, MEM_HEADS * PAIR), lambda i: (i, 0)),
                  pl.BlockSpec((None, 2 * KV_W, mem_len), lambda i: (i // tiles_per_seq, 0, 0))],
        out_specs=pl.BlockSpec((tm, KV_W), lambda i: (i, 0)),
        out_shape=jax.ShapeDtypeStruct((m, KV_W), BF),
        compiler_params=_cp(1),
        name="memattn_prompt",
    )(mq_al, mem_t)


def _outproj_body(*refs, n_x, combine):
    y_ref = refs[0]
    pos = 1
    xs = []
    if combine:
        og, mg, dg = refs[1:4], refs[4:7], refs[7:10]
        pos = 10
        ms = [r[...] for r in mg]
        mmax = jnp.maximum(jnp.maximum(ms[0], ms[1]), ms[2])
        wgt = [dg[k][...] * jnp.exp(ms[k] - mmax) for k in range(3)]
        num = wgt[0] * og[0][...] + wgt[1] * og[1][...] + wgt[2] * og[2][...]
        merged = num / (wgt[0] + wgt[1] + wgt[2])
        xs.append(jnp.concatenate([merged[c] for c in range(merged.shape[0])], axis=1))
    xs += [refs[pos + k][...] for k in range(n_x)]
    w_refs = refs[pos + n_x:pos + n_x + len(xs)]
    o_ref = refs[-1]
    acc = y_ref[...]
    for x, w_ref in zip(xs, w_refs):
        acc = acc + _dot(x.astype(BF), w_ref[...])
    o_ref[...] = acc


def _outproj(y, stats, xs, ws, tm):
    m, d = y.shape
    row = lambda width: pl.BlockSpec((tm, width), lambda i: (i, 0))
    chunked = lambda a: pl.BlockSpec((a.shape[0], tm, LANES), lambda i: (0, i, 0))
    in_specs = [row(d)] + [chunked(a) for a in stats] + [row(x.shape[1]) for x in xs]
    in_specs += [_resident(w.shape) for w in ws]
    return pl.pallas_call(
        functools.partial(_outproj_body, n_x=len(xs), combine=bool(stats)),
        grid=(m // tm,),
        in_specs=in_specs,
        out_specs=row(d),
        out_shape=jax.ShapeDtypeStruct((m, d), F32),
        compiler_params=_cp(1),
        name="outproj",
    )(y, *stats, *xs, *[w.astype(BF) for w in ws])


def _final_norm_body(y_ref, g_ref, o_ref):
    o_ref[...] = _rms(y_ref[...], g_ref[...])


def _final_norm(y, g, tm):
    m, d = y.shape
    return pl.pallas_call(
        _final_norm_body,
        grid=(m // tm,),
        in_specs=[pl.BlockSpec((tm, d), lambda i: (i, 0)), pl.BlockSpec((1, d), lambda i: (0, 0))],
        out_specs=pl.BlockSpec((tm, d), lambda i: (i, 0)),
        out_shape=jax.ShapeDtypeStruct((m, d), F32),
        compiler_params=_cp(1),
        name="final_norm",
    )(y, g.reshape(1, d))


def _page_copies(pool_ref, buf_ref, sem_ref, pt_ref, layer, b, first_page, n_pages, slot):
    return [pltpu.make_async_copy(pool_ref.at[layer, pt_ref[b, first_page + p]],
                                  buf_ref.at[slot, :, pl.ds(p * PAGE_SIZE, PAGE_SIZE)],
                                  sem_ref.at[slot]) for p in range(n_pages)]


def _idx_sample_body(pt_ref, iq_ref, iw_ref, ikn_ref, pool_ref, bias_ref, kbuf, sem, sc_ref,
                     *, layer, n_seq, n_pages, k_sel):
    b = pl.program_id(0)
    past = n_pages * PAGE_SIZE
    slot = b % 2

    @pl.when(b == 0)
    def _():
        sc_ref[...] = jnp.zeros(sc_ref.shape, I32)
        for cp in _page_copies(pool_ref, kbuf, sem, pt_ref, layer, 0, 0, n_pages, 0):
            cp.start()

    @pl.when(b + 1 < n_seq)
    def _():
        for cp in _page_copies(pool_ref, kbuf, sem, pt_ref, layer, b + 1, 0, n_pages, 1 - slot):
            cp.start()

    for cp in _page_copies(pool_ref, kbuf, sem, pt_ref, layer, b, 0, n_pages, slot):
        cp.wait()

    iq = iq_ref[0]
    iw = iw_ref[0]
    s = _dot(iq, kbuf[slot].astype(BF))
    row = jnp.sum(jnp.maximum(s, 0.0) * iw, axis=0, keepdims=True)
    ikn = ikn_ref[0].astype(BF).astype(F32)
    s_new = jnp.sum(iq.astype(F32) * ikn, axis=1, keepdims=True)
    r_new = jnp.sum(jnp.maximum(s_new, 0.0) * iw, axis=0, keepdims=True)
    lane = lax.broadcasted_iota(I32, (1, LANES), 1)
    tail = jnp.where(lane == 0, r_new, -jnp.inf)
    is_b = lax.broadcasted_iota(I32, (n_seq, 1), 0) == b
    sc_ref[:, :past] = jnp.where(is_b, _sort_key(row), sc_ref[:, :past])
    sc_ref[:, past:] = jnp.where(is_b, _sort_key(tail), sc_ref[:, past:])

    @pl.when(b == n_seq - 1)
    def _():
        def read(c):
            return sc_ref[:, pl.ds(pl.multiple_of(c * LANES, LANES), LANES)]

        def write(c, v):
            bias_ref[:, pl.ds(pl.multiple_of(c * LANES, LANES), LANES)] = v

        def count(pred):
            return jnp.sum(jnp.where(pred(sc_ref[...]), 1, 0).astype(I32), axis=1, keepdims=True)

        _select_bias(count, read, write, past // LANES + 1, LANES, k_sel, (n_seq, 1), 1)


def _idx_sample(page_table, iq3, iw3, ikn3, pool_t, layer):
    n_seq, n_pages = page_table.shape
    past = n_pages * PAGE_SIZE
    width = past + LANES
    k_sel = min(TOPK_MAX, (past + 1) // 4)
    grid_spec = pltpu.PrefetchScalarGridSpec(
        num_scalar_prefetch=1,
        grid=(n_seq,),
        in_specs=[pl.BlockSpec((1, IDX_HEADS, IDX_DIM), lambda b, pt: (b, 0, 0)),
                  pl.BlockSpec((1, IDX_HEADS, 1), lambda b, pt: (b, 0, 0)),
                  pl.BlockSpec((1, 1, IDX_DIM), lambda b, pt: (b, 0, 0)),
                  pl.BlockSpec(memory_space=pl.ANY)],
        out_specs=pl.BlockSpec((n_seq, width), lambda b, pt: (0, 0)),
        scratch_shapes=[pltpu.VMEM((2, IDX_DIM, past), F32),
                        pltpu.SemaphoreType.DMA((2,)),
                        pltpu.VMEM((n_seq, width), I32)])
    return pl.pallas_call(
        functools.partial(_idx_sample_body, layer=layer, n_seq=n_seq, n_pages=n_pages, k_sel=k_sel),
        grid_spec=grid_spec,
        out_shape=jax.ShapeDtypeStruct((n_seq, width), F32),
        compiler_params=_cp(1),
        name="idx_sample",
    )(page_table, iq3, iw3, ikn3, pool_t)


def _attn_a_sample_body(pt_ref, q_ref, bias_ref, kvn_ref, pool_ref, o_ref, kvbuf, sem, m_s, l_s, acc_s,
                        *, layer, n_seq, n_steps, pg):
    b, s = pl.program_id(0), pl.program_id(1)
    t = b * n_steps + s
    slot = t % 2
    keys = pg * PAGE_SIZE
    past = n_steps * keys

    @pl.when(t == 0)
    def _():
        for cp in _page_copies(pool_ref, kvbuf, sem, pt_ref, layer, 0, 0, pg, 0):
            cp.start()

    @pl.when(t + 1 < n_seq * n_steps)
    def _():
        wrap = s + 1 == n_steps
        nb = jnp.where(wrap, b + 1, b)
        ns = jnp.where(wrap, 0, s + 1)
        for cp in _page_copies(pool_ref, kvbuf, sem, pt_ref, layer, nb, ns * pg, pg, 1 - slot):
            cp.start()

    for cp in _page_copies(pool_ref, kvbuf, sem, pt_ref, layer, b, s * pg, pg, slot):
        cp.wait()

    @pl.when(s == 0)
    def _():
        m_s[...] = jnp.full(m_s.shape, M_INIT, F32)
        l_s[...] = jnp.zeros(l_s.shape, F32)
        acc_s[...] = jnp.zeros(acc_s.shape, F32)

    qb = q_ref[0]
    kv = kvbuf[slot]
    bias = bias_ref[0, :, pl.ds(pl.multiple_of(s * keys, keys), keys)]
    sc = _dot(qb, kv[:KV_W].astype(BF)) + bias
    m_old = m_s[...]
    m_new = jnp.maximum(m_old, jnp.max(sc, axis=1, keepdims=True))
    alpha = jnp.exp(m_old - m_new)
    p = jnp.exp(sc - m_new)
    l_new = alpha * l_s[...] + jnp.sum(p, axis=1, keepdims=True)
    acc_new = alpha * acc_s[...] + _dot_nt(p.astype(BF), kv[KV_W:].astype(BF))
    m_s[...] = m_new
    l_s[...] = l_new
    acc_s[...] = acc_new

    @pl.when(s == n_steps - 1)
    def _():
        kn = kvn_ref[0].astype(BF).astype(F32)
        b_new = bias_ref[0, :, past:past + LANES][:, 0:1]
        s_new = jnp.sum(qb.astype(F32) * kn[:, :KV_W], axis=1, keepdims=True) + b_new
        m_fin = jnp.maximum(m_new, s_new)
        a_fin = jnp.exp(m_new - m_fin)
        p_new = jnp.exp(s_new - m_fin)
        l_fin = a_fin * l_new + p_new
        o_ref[0] = (a_fin * acc_new + p_new.astype(BF).astype(F32) * kn[:, KV_W:]) / l_fin


def _attn_a_sample(page_table, qblk, bias3, kvn3, pool_t, layer):
    n_seq, n_pages = page_table.shape
    pg = min(PAGES_PER_STEP, n_pages)
    assert n_pages % pg == 0
    n_steps = n_pages // pg
    width = bias3.shape[2]
    grid_spec = pltpu.PrefetchScalarGridSpec(
        num_scalar_prefetch=1,
        grid=(n_seq, n_steps),
        in_specs=[pl.BlockSpec((1, _SAMPLE_A_ROWS, KV_W), lambda b, s, pt: (b, 0, 0)),
                  pl.BlockSpec((1, 1, width), lambda b, s, pt: (b, 0, 0)),
                  pl.BlockSpec((1, 1, 2 * KV_W), lambda b, s, pt: (b, 0, 0)),
                  pl.BlockSpec(memory_space=pl.ANY)],
        out_specs=pl.BlockSpec((1, _SAMPLE_A_ROWS, KV_W), lambda b, s, pt: (b, 0, 0)),
        scratch_shapes=[pltpu.VMEM((2, 2 * KV_W, pg * PAGE_SIZE), F32),
                        pltpu.SemaphoreType.DMA((2,)),
                        pltpu.VMEM((_SAMPLE_A_ROWS, 1), F32),
                        pltpu.VMEM((_SAMPLE_A_ROWS, 1), F32),
                        pltpu.VMEM((_SAMPLE_A_ROWS, KV_W), F32)])
    return pl.pallas_call(
        functools.partial(_attn_a_sample_body, layer=layer, n_seq=n_seq, n_steps=n_steps, pg=pg),
        grid_spec=grid_spec,
        out_shape=jax.ShapeDtypeStruct((n_seq, _SAMPLE_A_ROWS, KV_W), F32),
        compiler_params=_cp(2),
        name="attn_a_sample",
    )(page_table, qblk, bias3, kvn3, pool_t)


def _attn_b_sample_body(q_ref, st0, st1, st2, kn0, kn1, kn2, kc0, kc1, kc2, o_ref, ns0, ns1, ns2, *, n_seq):
    b = pl.program_id(0)
    res = []
    groups = ((st0, kn0, kc0, ns0), (st1, kn1, kc1, ns1), (st2, kn2, kc2, ns2))
    for gi, (st_ref, kn_ref, kc_ref, ns_ref) in enumerate(groups):
        window, dil = B_GROUPS[gi]
        q = q_ref[0, gi * _SAMPLE_ROWS:(gi + 1) * _SAMPLE_ROWS, :]
        st = st_ref[...]
        kn = kn_ref[0].astype(BF).astype(F32)
        s = _dot(q, st[:KV_W].astype(BF))
        if dil > 1:
            lane = lax.broadcasted_iota(I32, s.shape, 1)
            s = jnp.where((lane & (dil - 1)) == 0, s, -jnp.inf)
        s_new = jnp.sum(q.astype(F32) * kn[:, :KV_W], axis=1, keepdims=True)
        mx = jnp.maximum(jnp.max(s, axis=1, keepdims=True), s_new)
        e = jnp.exp(s - mx)
        e_new = jnp.exp(s_new - mx)
        den = jnp.sum(e, axis=1, keepdims=True) + e_new
        o = (_dot_nt(e.astype(BF), st[KV_W:].astype(BF)) + e_new.astype(BF).astype(F32) * kn[:, KV_W:]) / den
        res.append((o, mx, den))
        kc = kc_ref[...]
        col = jnp.sum(jnp.where(lax.broadcasted_iota(I32, kc.shape, 1) == b, kc, 0.0), axis=1, keepdims=True)
        last = lax.broadcasted_iota(I32, st.shape, 1) == window - 1
        ns_ref[...] = jnp.where(last, col, pltpu.roll(st, window - 1, 1))
    mmax = jnp.maximum(jnp.maximum(res[0][1], res[1][1]), res[2][1])
    wgt = [r[2] * jnp.exp(r[1] - mmax) for r in res]
    num = wgt[0] * res[0][0] + wgt[1] * res[1][0] + wgt[2] * res[2][0]
    o_ref[0] = num / (wgt[0] + wgt[1] + wgt[2])


def _attn_b_sample(qblk3, states_t, kvns, kvcs, layer):
    n_seq = qblk3.shape[0]
    in_specs = [pl.BlockSpec((1, len(B_GROUPS) * _SAMPLE_ROWS, KV_W), lambda b: (b, 0, 0))]
    out_specs = [pl.BlockSpec((1, _SAMPLE_ROWS, KV_W), lambda b: (b, 0, 0))]
    out_shapes = [jax.ShapeDtypeStruct((n_seq, _SAMPLE_ROWS, KV_W), F32)]
    for st, (window, _) in zip(states_t, B_GROUPS):
        assert st.shape[2:] == (2 * KV_W, window)
        in_specs.append(pl.BlockSpec((None, None, 2 * KV_W, window), lambda b: (layer, b, 0, 0)))
        out_specs.append(pl.BlockSpec((None, 2 * KV_W, window), lambda b: (b, 0, 0)))
        out_shapes.append(jax.ShapeDtypeStruct((n_seq, 2 * KV_W, window), F32))
    in_specs += [pl.BlockSpec((1, 1, 2 * KV_W), lambda b: (b, 0, 0))] * len(B_GROUPS)
    in_specs += [pl.BlockSpec((2 * KV_W, n_seq), lambda b: (0, 0))] * len(B_GROUPS)
    return pl.pallas_call(
        functools.partial(_attn_b_sample_body, n_seq=n_seq),
        grid=(n_seq,),
        in_specs=in_specs,
        out_specs=out_specs,
        out_shape=out_shapes,
        compiler_params=_cp(1),
        name="attn_b_sample",
    )(qblk3, *states_t, *kvns, *kvcs)


def _memattn_sample_body(q_ref, mem_ref, o_ref):
    q = q_ref[0]
    mem = mem_ref[...].astype(BF)
    s = _dot(q, mem[:KV_W])
    e = jnp.exp(s - jnp.max(s, axis=1, keepdims=True))
    p = e / jnp.sum(e, axis=1, keepdims=True)
    o_ref[0] = _dot_nt(p.astype(BF), mem[KV_W:])


def _memattn_sample(mq3, mem_t, layer):
    n_seq, mem_len = mem_t.shape[1], mem_t.shape[3]
    return pl.pallas_call(
        _memattn_sample_body,
        grid=(n_seq,),
        in_specs=[pl.BlockSpec((1, _SAMPLE_ROWS, KV_W), lambda b: (b, 0, 0)),
                  pl.BlockSpec((None, None, 2 * KV_W, mem_len), lambda b: (layer, b, 0, 0))],
        out_specs=pl.BlockSpec((1, _SAMPLE_ROWS, KV_W), lambda b: (b, 0, 0)),
        out_shape=jax.ShapeDtypeStruct((n_seq, _SAMPLE_ROWS, KV_W), F32),
        compiler_params=_cp(1),
        name="memattn_sample",
    )(mq3, mem_t)


def _feature_major(x):
    nd = x.ndim
    xt = jnp.transpose(x, tuple(range(nd - 4)) + (nd - 3, nd - 2, nd - 1, nd - 4))
    return xt.reshape(x.shape[:nd - 4] + (2 * KV_W, x.shape[nd - 4]))


def _token_major(x, heads):
    nd = x.ndim
    xr = x.reshape(x.shape[:nd - 2] + (2, heads, HEAD_DIM, x.shape[nd - 1]))
    return jnp.transpose(xr, tuple(range(nd - 2)) + (nd + 1, nd - 2, nd - 1, nd))


def kernel(x_prompt, x_sample, cache_kv_a, cache_kidx_a, state_kv_b1, state_kv_b2, state_kv_b3, cache_mem_kv, page_table, mem_prompt, ffn_norm, w_ffn_in, w_ffn_out, mix_norm, mem_norm, w_mem_kv, w_in_a, w_out_a, w_in_b, w_out_b, final_norm):
    batch, seq, d = x_prompt.shape
    n_seq = x_sample.shape[0]
    depth = ffn_norm.shape[0]
    n_pages = page_table.shape[1]
    past = n_pages * PAGE_SIZE
    mem_len = mem_prompt.shape[1]
    assert x_sample.shape[1] == 1 and seq % TM == 0
    ng = len(B_GROUPS)

    yp = x_prompt.reshape(batch * seq, d)
    ys = x_sample.reshape(n_seq, d)
    tab_p = _rope_tables(jnp.arange(seq))
    tab_s = _rope_tables(jnp.full((n_seq,), past, I32))
    w_in_bf = w_ffn_in.astype(BF)
    w_out_bf = w_ffn_out.astype(BF)
    mem_flat = mem_prompt.reshape(batch * mem_len, d)
    kv_pool_t = _feature_major(cache_kv_a)
    kidx_pool_t = jnp.swapaxes(cache_kidx_a, 2, 3)
    states_t = [_feature_major(s) for s in (state_kv_b1, state_kv_b2, state_kv_b3)]
    mem_cache_t = _feature_major(cache_mem_kv)

    kv_a_p, kidx_a_p, kv_a_s, kidx_a_s, mem_p = [], [], [], [], []
    b_p = ([], [], [])
    b_s = ([], [], [])
    for i in range(depth):
        j = i // 2
        yp = _ffn(yp, ffn_norm[i, 0], w_in_bf, w_out_bf, i, 0, TM)
        ys = _ffn(ys, ffn_norm[i, 0], w_in_bf, w_out_bf, i, 0, n_seq)
        mem_t = _memproj(mem_flat, mem_norm[i], w_mem_kv[i], batch)
        mem_p.append(mem_t)
        if i % 2 == 0:
            w = w_in_a[j]
            q_al, k_bf, iq_al, ik128, mq_al, kv_t, v_t, ik_t, iw_t = _proj_a_prompt(yp, mix_norm[i], w, tab_p, batch)
            o = _attn_a_prompt(q_al, iq_al, iw_t, k_bf, ik128, v_t, batch, seq)
            om = _memattn_prompt(mq_al, mem_t, seq, TM)
            n_o = A_HEADS * HEAD_DIM
            yp = _outproj(yp, [], [o, om], [w_out_a[j][:n_o], w_out_a[j][n_o:]], TM)
            kv_a_p.append(kv_t)
            kidx_a_p.append(ik_t)

            qblk, kvn, iq_s, ikn, iw_s, mq_blk = _proj_a_sample(ys, mix_norm[i], w, tab_s)
            bias = _idx_sample(page_table, iq_s.reshape(n_seq, IDX_HEADS, IDX_DIM),
                               iw_s[:, :IDX_HEADS].reshape(n_seq, IDX_HEADS, 1),
                               ikn.reshape(n_seq, 1, IDX_DIM), kidx_pool_t, j)
            o_s = _attn_a_sample(page_table, qblk.reshape(n_seq, _SAMPLE_A_ROWS, KV_W),
                                 bias.reshape(n_seq, 1, past + LANES), kvn.reshape(n_seq, 1, 2 * KV_W),
                                 kv_pool_t, j)
            om_s = _memattn_sample(mq_blk.reshape(n_seq, _SAMPLE_ROWS, KV_W), mem_cache_t, i)
            ys = _outproj(ys, [], [o_s.reshape(n_seq, -1), om_s.reshape(n_seq, -1)],
                          [_expand_out(w_out_a[j][:n_o], _A_Q_SLOT, _SAMPLE_A_ROWS),
                           _expand_out(w_out_a[j][n_o:], _MEM_SLOT, _SAMPLE_ROWS)], n_seq)
            kv_a_s.append(kvn.reshape(n_seq, 1, 2, A_KV_HEADS, HEAD_DIM))
            kidx_a_s.append(ikn.reshape(n_seq, 1, IDX_DIM))
        else:
            w = w_in_b[j]
            outs = _proj_b_prompt(yp, mix_norm[i], w, tab_p, batch)
            stats_o, stats_m, stats_d = [], [], []
            for gi, (window, dil) in enumerate(B_GROUPS):
                og, mg, dg = _attn_b_prompt(outs[2 * gi], outs[2 * gi + 1], batch, seq, window, dil)
                stats_o.append(og)
                stats_m.append(mg)
                stats_d.append(dg)
                b_p[gi].append(outs[2 * ng + 1 + gi][:, :, seq - min(window, seq):])
            om = _memattn_prompt(outs[2 * ng], mem_t, seq, TM)
            n_o = B_HEADS * HEAD_DIM
            yp = _outproj(yp, stats_o + stats_m + stats_d, [om], [w_out_b[j][:n_o], w_out_b[j][n_o:]], TM)

            souts = _proj_b_sample(ys, mix_norm[i], w, tab_s)
            kvns = [souts[1 + gi].reshape(n_seq, 1, 2 * KV_W) for gi in range(ng)]
            kvcs = [souts[2 + ng + gi] for gi in range(ng)]
            o_s, *new_states = _attn_b_sample(souts[0].reshape(n_seq, ng * _SAMPLE_ROWS, KV_W),
                                              states_t, kvns, kvcs, j)
            om_s = _memattn_sample(souts[1 + ng].reshape(n_seq, _SAMPLE_ROWS, KV_W), mem_cache_t, i)
            ys = _outproj(ys, [], [o_s.reshape(n_seq, -1), om_s.reshape(n_seq, -1)],
                          [_expand_out(w_out_b[j][:n_o], _B_SLOT, _SAMPLE_ROWS),
                           _expand_out(w_out_b[j][n_o:], _MEM_SLOT, _SAMPLE_ROWS)], n_seq)
            for gi in range(ng):
                b_s[gi].append(new_states[gi])
        yp = _ffn(yp, ffn_norm[i, 1], w_in_bf, w_out_bf, i, 1, TM)
        ys = _ffn(ys, ffn_norm[i, 1], w_in_bf, w_out_bf, i, 1, n_seq)

    y_prompt = _final_norm(yp, final_norm, TM).reshape(batch, seq, d)
    y_sample = _final_norm(ys, final_norm, n_seq).reshape(n_seq, 1, d)
    return (y_prompt, y_sample,
            _token_major(jnp.stack(kv_a_p), A_KV_HEADS), jnp.swapaxes(jnp.stack(kidx_a_p), 2, 3),
            _token_major(jnp.stack(b_p[0]), B_HEADS), _token_major(jnp.stack(b_p[1]), B_HEADS),
            _token_major(jnp.stack(b_p[2]), B_HEADS), _token_major(jnp.stack(mem_p), MEM_HEADS),
            jnp.stack(kv_a_s), jnp.stack(kidx_a_s),
            _token_major(jnp.stack(b_s[0]), B_HEADS), _token_major(jnp.stack(b_s[1]), B_HEADS),
            _token_major(jnp.stack(b_s[2]), B_HEADS))
```

```python
import functools

import jax
import jax.numpy as jnp
import numpy as np
from jax import lax
from jax.experimental import pallas as pl
from jax.experimental.pallas import tpu as pltpu

F32 = jnp.float32
BF = jnp.bfloat16
I32 = jnp.int32

HEAD_DIM = 64
ROT_DIM = HEAD_DIM // 4
ROT_HALF = ROT_DIM // 2
ROPE_THETA = 500000.0
ATTN_SCALE = HEAD_DIM ** -0.5
A_HEADS = 12
A_KV_HEADS = 4
A_GROUP = A_HEADS // A_KV_HEADS
IDX_HEADS = 8
IDX_DIM = 64
IDX_SCALE = (IDX_HEADS * IDX_DIM) ** -0.5
TOPK_MAX = 256
B_GROUPS = ((128, 1), (512, 4), (2048, 16))
B_HEADS = 4
MEM_HEADS = 4
EPS = 1e-6
PAGE_SIZE = 128

LANES = 128
SUBLANES = 8
PAIR = 2 * HEAD_DIM
KV_W = A_KV_HEADS * HEAD_DIM
VMEM_LIMIT_V7X = 56 * 2 ** 20
NEG_BIG = -1e30
M_INIT = -3e38
INT_MAX = 2 ** 31 - 1
KEY_NEG_INF = -2139095041
MAX_SEARCH_PASSES = 32

TM = 512
TQ = 1024
KC = 512
CLS = 128
PAGES_PER_STEP = 16
ONES_ROWS = 16


def _cp(n_axes):
    return pltpu.CompilerParams(dimension_semantics=("arbitrary",) * n_axes,
                                vmem_limit_bytes=VMEM_LIMIT_V7X)


def _dot(a, b):
    return jnp.dot(a, b, preferred_element_type=F32)


def _dot_nt(a, b):
    return lax.dot_general(a, b, (((1,), (1,)), ((), ())), preferred_element_type=F32)


def _rms(x, g):
    return x * lax.rsqrt(jnp.mean(x * x, axis=-1, keepdims=True) + EPS) * g


def _sort_key(x):
    b = pltpu.bitcast(x + 0.0, I32)
    return b ^ ((b >> 31) & 0x7FFFFFFF)


def _resident(shape):
    return pl.BlockSpec(shape, lambda *_: (0,) * len(shape), pipeline_mode=pl.Buffered(1))


def _ffn_body(y_ref, g_ref, win_ref, wout_ref, o_ref, *, d_ff, fc):
    x = y_ref[...]
    xn = _rms(x, g_ref[...]).astype(BF)
    acc = jnp.zeros(x.shape, F32)
    for c in range(d_ff // fc):
        a = _dot(xn, win_ref[:, c * fc:(c + 1) * fc])
        u = _dot(xn, win_ref[:, d_ff + c * fc:d_ff + (c + 1) * fc])
        h = (a * jax.nn.sigmoid(a) * u).astype(BF)
        acc = acc + _dot(h, wout_ref[c * fc:(c + 1) * fc, :])
    o_ref[...] = x + 0.5 * acc


def _ffn(y, g, w_in_all, w_out_all, layer, half, tm):
    m, d = y.shape
    d_ff = w_out_all.shape[2]
    fc = 256 if d_ff % 256 == 0 else LANES
    assert d_ff % fc == 0 and m % tm == 0
    return pl.pallas_call(
        functools.partial(_ffn_body, d_ff=d_ff, fc=fc),
        grid=(m // tm,),
        in_specs=[pl.BlockSpec((tm, d), lambda i: (i, 0)),
                  pl.BlockSpec((1, d), lambda i: (0, 0)),
                  pl.BlockSpec((None, None, d, 2 * d_ff), lambda i: (layer, half, 0, 0),
                               pipeline_mode=pl.Buffered(1)),
                  pl.BlockSpec((None, None, d_ff, d), lambda i: (layer, half, 0, 0),
                               pipeline_mode=pl.Buffered(1))],
        out_specs=pl.BlockSpec((tm, d), lambda i: (i, 0)),
        out_shape=jax.ShapeDtypeStruct((m, d), F32),
        compiler_params=_cp(1),
        name="ffn",
    )(y, g.reshape(1, d), w_in_all, w_out_all)


def _rope_tables(pos):
    inv = jnp.power(ROPE_THETA, -jnp.arange(ROT_HALF, dtype=F32) / ROT_HALF)
    ang = pos.astype(F32)[:, None] * inv[None, :]
    cos, sin = jnp.cos(ang), jnp.sin(ang)
    t = pos.shape[0]
    ones = jnp.ones((t, HEAD_DIM - ROT_DIM), F32)
    zeros = jnp.zeros((t, HEAD_DIM - ROT_DIM), F32)
    zh = jnp.zeros((t, ROT_HALF), F32)
    cos64 = jnp.concatenate([cos, cos, ones], axis=1)
    sa64 = jnp.concatenate([-sin, zh, zeros], axis=1)
    sb64 = jnp.concatenate([zh, sin, zeros], axis=1)
    row = tuple(jnp.tile(a, (1, LANES // HEAD_DIM)) for a in (cos64, sa64, sb64))
    return row + (cos.T, sin.T)


def _proj_body(*refs, segs, nts, n_in):
    y_ref, g_ref, cos_ref, sa_ref, sb_ref, cost_ref, sint_ref, w_ref = refs[:8]
    wt_refs = refs[8:n_in]
    outs = refs[n_in:]
    xn = _rms(y_ref[...], g_ref[...]).astype(BF)
    for lo, hi, rope, scale, writes in segs:
        p = _dot(xn, w_ref[:, lo:hi])
        if rope:
            cos, sa, sb = cos_ref[...], sa_ref[...], sb_ref[...]
            chunks = []
            for c in range((hi - lo) // LANES):
                pc = p[:, c * LANES:(c + 1) * LANES]
                chunks.append(pc * cos + pltpu.roll(pc, LANES - ROT_HALF, 1) * sa
                              + pltpu.roll(pc, ROT_HALF, 1) * sb)
            p = chunks[0] if len(chunks) == 1 else jnp.concatenate(chunks, axis=1)
        if scale != 1.0:
            p = p * scale
        for oi, slo, shi, dlo in writes:
            if len(outs[oi].shape) == 3:
                for c in range((shi - slo) // LANES):
                    outs[oi][dlo // LANES + c] = p[:, slo + c * LANES:slo + (c + 1) * LANES].astype(outs[oi].dtype)
            else:
                outs[oi][:, dlo:dlo + (shi - slo)] = p[:, slo:shi].astype(outs[oi].dtype)
    for t, (rope_rows, scale, writes) in enumerate(nts):
        r = _dot_nt(wt_refs[t][...], xn)
        if rope_rows:
            ct, st = cost_ref[...], sint_ref[...]
            pieces = []
            for h0 in range(0, rope_rows, HEAD_DIM):
                x1, x2 = r[h0:h0 + ROT_HALF], r[h0 + ROT_HALF:h0 + ROT_DIM]
                pieces += [x1 * ct - x2 * st, x2 * ct + x1 * st, r[h0 + ROT_DIM:h0 + HEAD_DIM]]
            if rope_rows < r.shape[0]:
                pieces.append(r[rope_rows:])
            r = jnp.concatenate(pieces, axis=0)
        if scale != 1.0:
            r = r * scale
        for oi, slo, shi, dlo in writes:
            outs[oi][dlo:dlo + (shi - slo), :] = r[slo:shi].astype(outs[oi].dtype)


def _proj(y, g, tables, n_seqs, w, wts, segs, nts, out_defs, tm):
    m, d = y.shape
    assert m % tm == 0
    per_seq = m // n_seqs // tm if tables[0].shape[0] != tm else 1
    n_in = 8 + len(wts)
    pos_row = lambda i: (i % per_seq, 0)
    pos_col = lambda i: (0, i % per_seq)
    in_specs = [pl.BlockSpec((tm, d), lambda i: (i, 0)),
                pl.BlockSpec((1, d), lambda i: (0, 0)),
                pl.BlockSpec((tm, LANES), pos_row),
                pl.BlockSpec((tm, LANES), pos_row),
                pl.BlockSpec((tm, LANES), pos_row),
                pl.BlockSpec((ROT_HALF, tm), pos_col),
                pl.BlockSpec((ROT_HALF, tm), pos_col),
                _resident(w.shape)] + [_resident(wt.shape) for wt in wts]
    out_specs, out_shapes = [], []
    for kind, size, dtype in out_defs:
        if kind == "row":
            out_specs.append(pl.BlockSpec((tm, size), lambda i: (i, 0)))
            out_shapes.append(jax.ShapeDtypeStruct((m, size), dtype))
        elif kind == "chunks":
            out_specs.append(pl.BlockSpec((size // LANES, tm, LANES), lambda i: (0, i, 0)))
            out_shapes.append(jax.ShapeDtypeStruct((size // LANES, m, LANES), dtype))
        elif kind == "col":
            out_specs.append(pl.BlockSpec((size, tm), lambda i: (0, i)))
            out_shapes.append(jax.ShapeDtypeStruct((size, m), dtype))
        else:
            out_specs.append(pl.BlockSpec((None, size, tm), lambda i: (i // per_seq, 0, i % per_seq)))
            out_shapes.append(jax.ShapeDtypeStruct((n_seqs, size, m // n_seqs), dtype))
    return pl.pallas_call(
        functools.partial(_proj_body, segs=tuple(segs), nts=tuple(nts), n_in=n_in),
        grid=(m // tm,),
        in_specs=in_specs,
        out_specs=out_specs,
        out_shape=out_shapes,
        compiler_params=_cp(1),
        name="proj",
    )(y, g.reshape(1, d), *tables, w, *wts)


def _heads(w, lo, n_heads):
    return w[:, lo:lo + n_heads * HEAD_DIM].reshape(w.shape[0], n_heads, HEAD_DIM)


def _aligned(wh, halves):
    d, h, _ = wh.shape
    z = jnp.zeros_like(wh)
    sel = jnp.asarray(halves, I32).reshape(1, h, 1)
    lo = jnp.where(sel == 0, wh, z)
    hi = jnp.where(sel == 1, wh, z)
    return jnp.concatenate([lo, hi], axis=2).reshape(d, h * PAIR)


def _blocked(wh, slots, n_rows):
    d, h, _ = wh.shape
    onehot = jnp.asarray(np.eye(A_KV_HEADS, dtype=np.float32)[np.asarray(slots)], wh.dtype)
    blk = (wh[:, :, None, :] * onehot[None, :, :, None]).reshape(d, h * KV_W)
    return jnp.pad(blk, ((0, 0), (0, (n_rows - h) * KV_W)))


def _expand_out(w_rows, slots, n_rows):
    h = len(slots)
    d = w_rows.shape[1]
    wh = w_rows.reshape(h, HEAD_DIM, d)
    onehot = jnp.asarray(np.eye(A_KV_HEADS, dtype=np.float32)[np.asarray(slots)], w_rows.dtype)
    blk = (wh[:, None, :, :] * onehot[:, :, None, None]).reshape(h * KV_W, d)
    return jnp.pad(blk, ((0, (n_rows - h) * KV_W), (0, 0)))


_A_OFF = np.cumsum([0, A_HEADS * HEAD_DIM, KV_W, KV_W, IDX_HEADS * IDX_DIM, IDX_DIM, IDX_HEADS]).tolist()
_A_Q_HALF = [(h // A_GROUP) % 2 for h in range(A_HEADS)]
_A_Q_SLOT = [h // A_GROUP for h in range(A_HEADS)]
_MEM_HALF = [h % 2 for h in range(MEM_HEADS)]
_MEM_SLOT = list(range(MEM_HEADS))
_B_HALF = [h % 2 for h in range(B_HEADS)]
_B_SLOT = list(range(B_HEADS))
_SAMPLE_A_ROWS = 16
_SAMPLE_ROWS = 8


def _segments(items):
    segs, cols, lo = [], [], 0
    for wc, rope, scale, writes in items:
        hi = lo + wc.shape[1]
        segs.append((lo, hi, rope, scale, tuple(writes)))
        cols.append(wc)
        lo = hi
    return jnp.concatenate(cols, axis=1).astype(BF), segs


def _proj_a_prompt(y, g, w, tables, batch):
    o = _A_OFF
    zpad = jnp.zeros((w.shape[0], LANES - IDX_DIM), w.dtype)
    items = [
        (_aligned(_heads(w, o[0], A_HEADS), _A_Q_HALF), True, ATTN_SCALE, [(0, 0, A_HEADS * PAIR, 0)]),
        (w[:, o[1]:o[2]], True, 1.0, [(1, 0, KV_W, 0)]),
        (_aligned(_heads(w, o[3], IDX_HEADS), [0] * IDX_HEADS), True, 1.0, [(2, 0, IDX_HEADS * PAIR, 0)]),
        (jnp.concatenate([w[:, o[4]:o[5]], zpad], axis=1), True, 1.0, [(3, 0, LANES, 0)]),
        (_aligned(_heads(w, o[6], MEM_HEADS), _MEM_HALF), False, ATTN_SCALE, [(4, 0, MEM_HEADS * PAIR, 0)]),
    ]
    wcat, segs = _segments(items)
    wts = [w[:, o[1]:o[3]].T.astype(BF), w[:, o[4]:o[5]].T.astype(BF), w[:, o[5]:o[6]].T.astype(BF)]
    nts = [(KV_W, 1.0, ((5, 0, 2 * KV_W, 0), (6, KV_W, 2 * KV_W, 0))),
           (IDX_DIM, 1.0, ((7, 0, IDX_DIM, 0),)),
           (0, IDX_SCALE, ((8, 0, IDX_HEADS, 0),))]
    out_defs = [("row", A_HEADS * PAIR, BF), ("row", KV_W, BF), ("row", IDX_HEADS * PAIR, BF),
                ("row", LANES, BF), ("row", MEM_HEADS * PAIR, BF), ("bcol", 2 * KV_W, F32),
                ("col", KV_W, BF), ("bcol", IDX_DIM, F32), ("col", IDX_HEADS, F32)]
    return _proj(y, g, tables, batch, wcat, wts, segs, nts, out_defs, TM)


def _proj_a_sample(y, g, w, tables):
    o = _A_OFF
    zpad = jnp.zeros((w.shape[0], LANES - IDX_DIM), w.dtype)
    wpad = jnp.zeros((w.shape[0], LANES - IDX_HEADS), w.dtype)
    items = [
        (_blocked(_heads(w, o[0], A_HEADS), _A_Q_SLOT, _SAMPLE_A_ROWS), True, ATTN_SCALE,
         [(0, 0, _SAMPLE_A_ROWS * KV_W, 0)]),
        (w[:, o[1]:o[2]], True, 1.0, [(1, 0, KV_W, 0)]),
        (w[:, o[2]:o[3]], False, 1.0, [(1, 0, KV_W, KV_W)]),
        (w[:, o[3]:o[4]], True, 1.0, [(2, 0, IDX_HEADS * IDX_DIM, 0)]),
        (jnp.concatenate([w[:, o[4]:o[5]], zpad], axis=1), True, 1.0, [(3, 0, IDX_DIM, 0)]),
        (jnp.concatenate([w[:, o[5]:o[6]], wpad], axis=1), False, IDX_SCALE, [(4, 0, LANES, 0)]),
        (_blocked(_heads(w, o[6], MEM_HEADS), _MEM_SLOT, _SAMPLE_ROWS), False, ATTN_SCALE,
         [(5, 0, _SAMPLE_ROWS * KV_W, 0)]),
    ]
    wcat, segs = _segments(items)
    out_defs = [("row", _SAMPLE_A_ROWS * KV_W, BF), ("row", 2 * KV_W, F32), ("row", IDX_HEADS * IDX_DIM, BF),
                ("row", IDX_DIM, F32), ("row", LANES, F32), ("row", _SAMPLE_ROWS * KV_W, BF)]
    return _proj(y, g, tables, 1, wcat, [], segs, [], out_defs, y.shape[0])


def _b_off(g, t):
    return (g * 3 + t) * B_HEADS * HEAD_DIM


def _proj_b_prompt(y, g, w, tables, batch):
    ng = len(B_GROUPS)
    items, out_defs, wts, nts = [], [], [], []
    for gi in range(ng):
        items += [
            (_aligned(_heads(w, _b_off(gi, 0), B_HEADS), _B_HALF), True, ATTN_SCALE,
             [(2 * gi, 0, B_HEADS * PAIR, 0)]),
            (w[:, _b_off(gi, 1):_b_off(gi, 1) + KV_W], True, 1.0, [(2 * gi + 1, 0, KV_W, 0)]),
            (w[:, _b_off(gi, 2):_b_off(gi, 2) + KV_W], False, 1.0, [(2 * gi + 1, 0, KV_W, KV_W)]),
        ]
        out_defs += [("chunks", B_HEADS * PAIR, F32), ("chunks", 2 * KV_W, F32)]
    items.append((_aligned(_heads(w, _b_off(ng, 0), MEM_HEADS), _MEM_HALF), False, ATTN_SCALE,
                  [(2 * ng, 0, MEM_HEADS * PAIR, 0)]))
    out_defs.append(("row", MEM_HEADS * PAIR, BF))
    for gi in range(ng):
        wts.append(w[:, _b_off(gi, 1):_b_off(gi, 1) + 2 * KV_W].T.astype(BF))
        nts.append((KV_W, 1.0, ((2 * ng + 1 + gi, 0, 2 * KV_W, 0),)))
        out_defs.append(("bcol", 2 * KV_W, F32))
    wcat, segs = _segments(items)
    return _proj(y, g, tables, batch, wcat, wts, segs, nts, out_defs, TM)


def _proj_b_sample(y, g, w, tables):
    ng = len(B_GROUPS)
    items, out_defs, wts, nts = [], [("row", ng * _SAMPLE_ROWS * KV_W, BF)], [], []
    for gi in range(ng):
        items += [
            (_blocked(_heads(w, _b_off(gi, 0), B_HEADS), _B_SLOT, _SAMPLE_ROWS), True, ATTN_SCALE,
             [(0, 0, _SAMPLE_ROWS * KV_W, gi * _SAMPLE_ROWS * KV_W)]),
            (w[:, _b_off(gi, 1):_b_off(gi, 1) + KV_W], True, 1.0, [(1 + gi, 0, KV_W, 0)]),
            (w[:, _b_off(gi, 2):_b_off(gi, 2) + KV_W], False, 1.0, [(1 + gi, 0, KV_W, KV_W)]),
        ]
        out_defs.append(("row", 2 * KV_W, F32))
    items.append((_blocked(_heads(w, _b_off(ng, 0), MEM_HEADS), _MEM_SLOT, _SAMPLE_ROWS), False, ATTN_SCALE,
                  [(1 + ng, 0, _SAMPLE_ROWS * KV_W, 0)]))
    out_defs.append(("row", _SAMPLE_ROWS * KV_W, BF))
    for gi in range(ng):
        wts.append(w[:, _b_off(gi, 1):_b_off(gi, 1) + 2 * KV_W].T.astype(BF))
        nts.append((KV_W, 1.0, ((2 + ng + gi, 0, 2 * KV_W, 0),)))
        out_defs.append(("col", 2 * KV_W, F32))
    wcat, segs = _segments(items)
    return _proj(y, g, tables, 1, wcat, wts, segs, nts, out_defs, y.shape[0])


def _memproj_body(y_ref, g_ref, wt_ref, o_ref):
    xn = _rms(y_ref[...], g_ref[...]).astype(BF)
    o_ref[...] = _dot_nt(wt_ref[...], xn)


def _memproj(mem_flat, g, w, batch):
    m, d = mem_flat.shape
    mem_len = m // batch
    n = w.shape[1]
    return pl.pallas_call(
        _memproj_body,
        grid=(batch,),
        in_specs=[pl.BlockSpec((mem_len, d), lambda b: (b, 0)),
                  pl.BlockSpec((1, d), lambda b: (0, 0)),
                  _resident((n, d))],
        out_specs=pl.BlockSpec((None, n, mem_len), lambda b: (b, 0, 0)),
        out_shape=jax.ShapeDtypeStruct((batch, n, mem_len), F32),
        compiler_params=_cp(1),
        name="memproj",
    )(mem_flat, g.reshape(1, d), w.T.astype(BF))


def _select_bias(count, read, write, n_chunks, rows, k_sel, red_shape, key_axis):
    lo0 = KEY_NEG_INF + 1
    c_valid = count(lambda x: x >= lo0)
    c_zero = count(lambda x: x >= 0)
    pos = c_zero >= k_sel
    few = c_valid <= k_sel
    lo = jnp.where(pos, 0, lo0).astype(I32)
    hi = jnp.where(pos, INT_MAX, 0).astype(I32)
    c_lo = jnp.where(pos, c_zero, c_valid)
    c_hi = jnp.where(pos, 0, c_zero)
    done = jnp.where(jnp.logical_or(few, c_zero == k_sel), 1, 0).astype(I32)
    thr = jnp.where(few, lo0, 0).astype(I32)
    full = jnp.full(red_shape, k_sel, I32)

    def cond(st):
        return jnp.logical_and(st[0] < MAX_SEARCH_PASSES, st[1] == 0)

    def body(st):
        it, _, lo, hi, c_lo, c_hi, thr, need, n_ge, done = st
        cand = lo + jnp.maximum((hi - lo) >> 1, 1)
        c = count(lambda x: x >= cand)
        hit = c == k_sel
        up = c > k_sel
        lo_n = jnp.where(up, cand, lo)
        hi_n = jnp.where(up, hi, cand)
        c_lo_n = jnp.where(up, c, c_lo)
        c_hi_n = jnp.where(up, c_hi, c)
        newly = jnp.logical_and(done == 0, jnp.logical_or(hit, hi_n - lo_n == 1))
        thr_n = jnp.where(newly, jnp.where(hit, cand, lo_n), thr)
        need_n = jnp.where(jnp.logical_and(newly, jnp.logical_not(hit)), k_sel - c_hi_n, need)
        n_ge_n = jnp.where(jnp.logical_and(newly, jnp.logical_not(hit)), c_lo_n, n_ge)
        done_n = jnp.where(newly, 1, done)
        return (it + 1, jnp.min(done_n), lo_n, hi_n, c_lo_n, c_hi_n, thr_n, need_n, n_ge_n, done_n)

    st = lax.while_loop(cond, body, (jnp.int32(0), jnp.min(done), lo, hi, c_lo, c_hi, thr, full, full, done))
    thr, need_i, n_ge = st[6], st[7], st[8]
    has_tie = jnp.max(n_ge) > k_sel

    @pl.when(jnp.logical_not(has_tie))
    def _():
        def wbody(c, _):
            x = read(c)
            sel = jnp.logical_and(x >= thr, x > KEY_NEG_INF)
            write(c, jnp.where(sel, 0.0, NEG_BIG).astype(F32))
            return 0
        lax.fori_loop(0, n_chunks, wbody, 0)

    @pl.when(has_tie)
    def _():
        need = need_i.astype(F32)
        r = lax.broadcasted_iota(I32, (rows, rows), 0)
        cidx = lax.broadcasted_iota(I32, (rows, rows), 1)
        if key_axis == 0:
            tri = jnp.where(cidx < r, 1.0, 0.0).astype(BF)
        else:
            tri = jnp.where(r < cidx, 1.0, 0.0).astype(BF)

        def wbody(c, seen):
            x = read(c)
            eq = jnp.logical_and(x == thr, x > KEY_NEG_INF)
            eqf = jnp.where(eq, 1.0, 0.0).astype(BF)
            rank = (_dot(tri, eqf) if key_axis == 0 else _dot(eqf, tri)) + seen
            sel = jnp.logical_and(jnp.logical_or(x > thr, jnp.logical_and(eq, rank < need)), x > KEY_NEG_INF)
            write(c, jnp.where(sel, 0.0, NEG_BIG).astype(F32))
            return seen + jnp.sum(eqf.astype(F32), axis=key_axis, keepdims=True)
        lax.fori_loop(0, n_chunks, wbody, jnp.zeros(red_shape, F32))


def _attn_a_body(q_ref, iq_ref, iw_ref, k_ref, ik_ref, vt_ref, o_ref, sc_ref, acc_ref, *, tq, kc, k_sel):
    qi = pl.program_id(1)
    per_q = tq // kc
    iw = iw_ref[...]

    def chunk_rows(c):
        return pl.ds(pl.multiple_of(c * kc, kc), kc)

    def idx_chunk(c, diag_off):
        ikc = ik_ref[chunk_rows(c), :]
        acc = jnp.zeros((kc, tq), F32)
        for h in range(IDX_HEADS):
            s = _dot_nt(ikc, iq_ref[:, h * PAIR:(h + 1) * PAIR])
            acc = acc + jnp.maximum(s, 0.0) * iw[h:h + 1, :]
        if diag_off is not None:
            kpos = lax.broadcasted_iota(I32, (kc, tq), 0) + diag_off
            qpos = lax.broadcasted_iota(I32, (kc, tq), 1)
            acc = jnp.where(kpos <= qpos, acc, -jnp.inf)
        sc_ref[chunk_rows(c), :] = _sort_key(acc)

    def full_chunk(c, _):
        idx_chunk(c, None)
        return 0
    lax.fori_loop(0, qi * per_q, full_chunk, 0)
    for j in range(per_q):
        idx_chunk(qi * per_q + j, j * kc)
    n_chunks = (qi + 1) * per_q

    def read(c):
        return sc_ref[chunk_rows(c), :]

    def write(c, v):
        sc_ref[chunk_rows(c), :] = pltpu.bitcast(v, I32)

    def count(pred):
        def body(c, cnt):
            ones = jnp.where(pred(read(c)), 1, 0).astype(I32)
            return cnt + jnp.sum(ones.reshape(kc // SUBLANES, SUBLANES, tq), axis=0)
        part = lax.fori_loop(0, n_chunks, body, jnp.zeros((SUBLANES, tq), I32))
        return jnp.sum(part, axis=0, keepdims=True)

    _select_bias(count, read, write, n_chunks, kc, k_sel, (1, tq), 0)

    acc_ref[...] = jnp.zeros(acc_ref.shape, F32)
    ones_rows = jnp.ones((ONES_ROWS, kc), BF)
    pad_rows = jnp.zeros((2 * SUBLANES - A_HEADS, tq), F32)

    def att_chunk(c, carry):
        m_all, l_all = carry
        rows_c = chunk_rows(c)
        bias = pltpu.bitcast(sc_ref[rows_c, :], F32)
        kcs = [k_ref[rows_c, pair * PAIR:(pair + 1) * PAIR] for pair in range(A_KV_HEADS // 2)]
        vts = [jnp.concatenate([vt_ref[n * HEAD_DIM:(n + 1) * HEAD_DIM, rows_c], ones_rows], axis=0)
               for n in range(A_KV_HEADS)]
        m_rows, l_rows = [], []
        for h in range(A_HEADS):
            n = h // A_GROUP
            s = _dot_nt(kcs[n // 2], q_ref[:, h * PAIR:(h + 1) * PAIR]) + bias
            m_old = m_all[h:h + 1]
            m_new = jnp.maximum(m_old, jnp.max(s, axis=0, keepdims=True))
            alpha = jnp.exp(m_old - m_new)
            p = jnp.exp(s - m_new).astype(BF)
            res = _dot(vts[n], p)
            rows = slice(h * HEAD_DIM, (h + 1) * HEAD_DIM)
            m_rows.append(m_new)
            l_rows.append(alpha * l_all[h:h + 1] + res[HEAD_DIM:HEAD_DIM + 1])
            acc_ref[rows, :] = alpha * acc_ref[rows, :] + res[:HEAD_DIM]
        return (jnp.concatenate(m_rows + [pad_rows], axis=0), jnp.concatenate(l_rows + [pad_rows], axis=0))

    init = (jnp.full((2 * SUBLANES, tq), M_INIT, F32), jnp.zeros((2 * SUBLANES, tq), F32))
    _, l_all = lax.fori_loop(0, n_chunks, att_chunk, init)

    for hp in range(A_HEADS // 2):
        outs = [acc_ref[h * HEAD_DIM:(h + 1) * HEAD_DIM, :] / l_all[h:h + 1] for h in (2 * hp, 2 * hp + 1)]
        o_ref[:, hp * PAIR:(hp + 1) * PAIR] = jnp.concatenate(outs, axis=0).T.astype(o_ref.dtype)


def _attn_a_prompt(q_al, iq_al, iw_t, k_bf, ik128, v_t, batch, seq):
    tq = min(TQ, seq)
    kc = min(KC, tq)
    nq = seq // tq
    m = batch * seq
    k_sel = min(TOPK_MAX, seq // 4)
    return pl.pallas_call(
        functools.partial(_attn_a_body, tq=tq, kc=kc, k_sel=k_sel),
        grid=(batch, nq),
        in_specs=[pl.BlockSpec((tq, A_HEADS * PAIR), lambda b, i: (b * nq + i, 0)),
                  pl.BlockSpec((tq, IDX_HEADS * PAIR), lambda b, i: (b * nq + i, 0)),
                  pl.BlockSpec((IDX_HEADS, tq), lambda b, i: (0, b * nq + i)),
                  pl.BlockSpec((seq, KV_W), lambda b, i: (b, 0)),
                  pl.BlockSpec((seq, LANES), lambda b, i: (b, 0)),
                  pl.BlockSpec((KV_W, seq), lambda b, i: (0, b))],
        out_specs=pl.BlockSpec((tq, A_HEADS * HEAD_DIM), lambda b, i: (b * nq + i, 0)),
        out_shape=jax.ShapeDtypeStruct((m, A_HEADS * HEAD_DIM), BF),
        scratch_shapes=[pltpu.VMEM((seq, tq), I32),
                        pltpu.VMEM((A_HEADS * HEAD_DIM, tq), F32)],
        compiler_params=_cp(2),
        name="attn_a_prompt",
    )(q_al, iq_al, iw_t, k_bf, ik128, v_t)


def _attn_b_body(q_ref, kvc_ref, kvp_ref, o_ref, m_ref, d_ref, *, dil, span):
    i = pl.program_id(1)
    qrow = lax.broadcasted_iota(I32, (CLS, 2 * CLS), 0)
    kcol = lax.broadcasted_iota(I32, (CLS, 2 * CLS), 1)
    rel = qrow + CLS - kcol
    valid = jnp.logical_and(jnp.logical_and(rel >= 0, rel <= span), jnp.logical_or(kcol >= CLS, i > 0))
    lo_half = lax.broadcasted_iota(I32, (CLS, PAIR), 1) < HEAD_DIM

    n_pairs = B_HEADS // 2

    def one_class(rows):
        for pair in range(n_pairs):
            kp = jnp.concatenate([kvp_ref[pair, rows, :], kvc_ref[pair, rows, :]], axis=0).astype(BF)
            vp = jnp.concatenate([kvp_ref[n_pairs + pair, rows, :], kvc_ref[n_pairs + pair, rows, :]],
                                 axis=0).astype(BF)
            res = []
            for h in (2 * pair, 2 * pair + 1):
                s = jnp.where(valid, _dot_nt(q_ref[h, rows, :].astype(BF), kp), -jnp.inf)
                mx = jnp.max(s, axis=1, keepdims=True)
                e = jnp.exp(s - mx)
                den = jnp.sum(e, axis=1, keepdims=True)
                res.append((_dot(e.astype(BF), vp) / den, mx, den))
            o_ref[pair, rows, :] = jnp.where(lo_half, res[0][0], res[1][0])
            m_ref[pair, rows, :] = jnp.where(lo_half, res[0][1], res[1][1])
            d_ref[pair, rows, :] = jnp.where(lo_half, res[0][2], res[1][2])

    if dil == 1:
        one_class(pl.ds(0, CLS))
    else:
        def body(r, _):
            one_class(pl.ds(r, CLS, stride=dil))
            return 0
        lax.fori_loop(0, dil, body, 0)


def _attn_b_prompt(q_al, kv, batch, seq, window, dil):
    rows = CLS * dil
    span = window // dil
    assert span <= CLS and seq % rows == 0
    nr = seq // rows
    m = batch * seq
    n_pairs = B_HEADS // 2
    stat = jax.ShapeDtypeStruct((n_pairs, m, PAIR), F32)
    ospec = pl.BlockSpec((n_pairs, rows, PAIR), lambda b, i: (0, b * nr + i, 0))
    return pl.pallas_call(
        functools.partial(_attn_b_body, dil=dil, span=span),
        grid=(batch, nr),
        in_specs=[pl.BlockSpec((B_HEADS, rows, PAIR), lambda b, i: (0, b * nr + i, 0)),
                  pl.BlockSpec((2 * n_pairs, rows, PAIR), lambda b, i: (0, b * nr + i, 0)),
                  pl.BlockSpec((2 * n_pairs, rows, PAIR), lambda b, i: (0, b * nr + jnp.maximum(i - 1, 0), 0))],
        out_specs=[ospec, ospec, ospec],
        out_shape=[stat, stat, stat],
        compiler_params=_cp(2),
        name="attn_b_prompt",
    )(q_al, kv, kv)


def _memattn_body(mq_ref, mem_ref, o_ref):
    mq = mq_ref[...]
    mem = mem_ref[...].astype(BF)
    lo_half = lax.broadcasted_iota(I32, (mq.shape[0], PAIR), 1) < HEAD_DIM
    outs = []
    for pair in range(MEM_HEADS // 2):
        mk = mem[pair * PAIR:(pair + 1) * PAIR]
        mv = mem[KV_W + pair * PAIR:KV_W + (pair + 1) * PAIR]
        res = []
        for h in (2 * pair, 2 * pair + 1):
            s = _dot(mq[:, h * PAIR:(h + 1) * PAIR], mk)
            e = jnp.exp(s - jnp.max(s, axis=1, keepdims=True))
            p = e / jnp.sum(e, axis=1, keepdims=True)
            res.append(_dot_nt(p.astype(BF), mv))
        outs.append(jnp.where(lo_half, res[0], res[1]))
    o_ref[...] = jnp.concatenate(outs, axis=1).astype(o_ref.dtype)


def _memattn_prompt(mq_al, mem_t, seq, tm):
    m = mq_al.shape[0]
    mem_len = mem_t.shape[2]
    tiles_per_seq = seq // tm
    return pl.pallas_call(
        _memattn_body,
        grid=(m // tm,),
        in_specs=[pl.BlockSpec((tm, MEM_HEADS * PAIR), lambda i: (i, 0)),
                  pl.BlockSpec((None, 2 * KV_W, mem_len), lambda i: (i // tiles_per_seq, 0, 0))],
        out_specs=pl.BlockSpec((tm, KV_W), lambda i: (i, 0)),
        out_shape=jax.ShapeDtypeStruct((m, KV_W), BF),
        compiler_params=_cp(1),
        name="memattn_prompt",
    )(mq_al, mem_t)


def _outproj_body(*refs, n_x, combine):
    y_ref = refs[0]
    pos = 1
    xs = []
    if combine:
        og, mg, dg = refs[1:4], refs[4:7], refs[7:10]
        pos = 10
        ms = [r[...] for r in mg]
        mmax = jnp.maximum(jnp.maximum(ms[0], ms[1]), ms[2])
        wgt = [dg[k][...] * jnp.exp(ms[k] - mmax) for k in range(3)]
        num = wgt[0] * og[0][...] + wgt[1] * og[1][...] + wgt[2] * og[2][...]
        merged = num / (wgt[0] + wgt[1] + wgt[2])
        xs.append(jnp.concatenate([merged[c] for c in range(merged.shape[0])], axis=1))
    xs += [refs[pos + k][...] for k in range(n_x)]
    w_refs = refs[pos + n_x:pos + n_x + len(xs)]
    o_ref = refs[-1]
    acc = y_ref[...]
    for x, w_ref in zip(xs, w_refs):
        acc = acc + _dot(x.astype(BF), w_ref[...])
    o_ref[...] = acc


def _outproj(y, stats, xs, ws, tm):
    m, d = y.shape
    row = lambda width: pl.BlockSpec((tm, width), lambda i: (i, 0))
    chunked = lambda a: pl.BlockSpec((a.shape[0], tm, LANES), lambda i: (0, i, 0))
    in_specs = [row(d)] + [chunked(a) for a in stats] + [row(x.shape[1]) for x in xs]
    in_specs += [_resident(w.shape) for w in ws]
    return pl.pallas_call(
        functools.partial(_outproj_body, n_x=len(xs), combine=bool(stats)),
        grid=(m // tm,),
        in_specs=in_specs,
        out_specs=row(d),
        out_shape=jax.ShapeDtypeStruct((m, d), F32),
        compiler_params=_cp(1),
        name="outproj",
    )(y, *stats, *xs, *[w.astype(BF) for w in ws])


def _final_norm_body(y_ref, g_ref, o_ref):
    o_ref[...] = _rms(y_ref[...], g_ref[...])


def _final_norm(y, g, tm):
    m, d = y.shape
    return pl.pallas_call(
        _final_norm_body,
        grid=(m // tm,),
        in_specs=[pl.BlockSpec((tm, d), lambda i: (i, 0)), pl.BlockSpec((1, d), lambda i: (0, 0))],
        out_specs=pl.BlockSpec((tm, d), lambda i: (i, 0)),
        out_shape=jax.ShapeDtypeStruct((m, d), F32),
        compiler_params=_cp(1),
        name="final_norm",
    )(y, g.reshape(1, d))


def _page_copies(pool_ref, buf_ref, sem_ref, pt_ref, layer, b, first_page, n_pages, slot):
    return [pltpu.make_async_copy(pool_ref.at[layer, pt_ref[b, first_page + p]],
                                  buf_ref.at[slot, :, pl.ds(p * PAGE_SIZE, PAGE_SIZE)],
                                  sem_ref.at[slot]) for p in range(n_pages)]


def _idx_sample_body(pt_ref, iq_ref, iw_ref, ikn_ref, pool_ref, bias_ref, kbuf, sem, sc_ref,
                     *, layer, n_seq, n_pages, k_sel):
    b = pl.program_id(0)
    past = n_pages * PAGE_SIZE
    slot = b % 2

    @pl.when(b == 0)
    def _():
        sc_ref[...] = jnp.zeros(sc_ref.shape, I32)
        for cp in _page_copies(pool_ref, kbuf, sem, pt_ref, layer, 0, 0, n_pages, 0):
            cp.start()

    @pl.when(b + 1 < n_seq)
    def _():
        for cp in _page_copies(pool_ref, kbuf, sem, pt_ref, layer, b + 1, 0, n_pages, 1 - slot):
            cp.start()

    for cp in _page_copies(pool_ref, kbuf, sem, pt_ref, layer, b, 0, n_pages, slot):
        cp.wait()

    iq = iq_ref[0]
    iw = iw_ref[0]
    s = _dot(iq, kbuf[slot].astype(BF))
    row = jnp.sum(jnp.maximum(s, 0.0) * iw, axis=0, keepdims=True)
    ikn = ikn_ref[0].astype(BF).astype(F32)
    s_new = jnp.sum(iq.astype(F32) * ikn, axis=1, keepdims=True)
    r_new = jnp.sum(jnp.maximum(s_new, 0.0) * iw, axis=0, keepdims=True)
    lane = lax.broadcasted_iota(I32, (1, LANES), 1)
    tail = jnp.where(lane == 0, r_new, -jnp.inf)
    is_b = lax.broadcasted_iota(I32, (n_seq, 1), 0) == b
    sc_ref[:, :past] = jnp.where(is_b, _sort_key(row), sc_ref[:, :past])
    sc_ref[:, past:] = jnp.where(is_b, _sort_key(tail), sc_ref[:, past:])

    @pl.when(b == n_seq - 1)
    def _():
        def read(c):
            return sc_ref[:, pl.ds(pl.multiple_of(c * LANES, LANES), LANES)]

        def write(c, v):
            bias_ref[:, pl.ds(pl.multiple_of(c * LANES, LANES), LANES)] = v

        def count(pred):
            return jnp.sum(jnp.where(pred(sc_ref[...]), 1, 0).astype(I32), axis=1, keepdims=True)

        _select_bias(count, read, write, past // LANES + 1, LANES, k_sel, (n_seq, 1), 1)


def _idx_sample(page_table, iq3, iw3, ikn3, pool_t, layer):
    n_seq, n_pages = page_table.shape
    past = n_pages * PAGE_SIZE
    width = past + LANES
    k_sel = min(TOPK_MAX, (past + 1) // 4)
    grid_spec = pltpu.PrefetchScalarGridSpec(
        num_scalar_prefetch=1,
        grid=(n_seq,),
        in_specs=[pl.BlockSpec((1, IDX_HEADS, IDX_DIM), lambda b, pt: (b, 0, 0)),
                  pl.BlockSpec((1, IDX_HEADS, 1), lambda b, pt: (b, 0, 0)),
                  pl.BlockSpec((1, 1, IDX_DIM), lambda b, pt: (b, 0, 0)),
                  pl.BlockSpec(memory_space=pl.ANY)],
        out_specs=pl.BlockSpec((n_seq, width), lambda b, pt: (0, 0)),
        scratch_shapes=[pltpu.VMEM((2, IDX_DIM, past), F32),
                        pltpu.SemaphoreType.DMA((2,)),
                        pltpu.VMEM((n_seq, width), I32)])
    return pl.pallas_call(
        functools.partial(_idx_sample_body, layer=layer, n_seq=n_seq, n_pages=n_pages, k_sel=k_sel),
        grid_spec=grid_spec,
        out_shape=jax.ShapeDtypeStruct((n_seq, width), F32),
        compiler_params=_cp(1),
        name="idx_sample",
    )(page_table, iq3, iw3, ikn3, pool_t)


def _attn_a_sample_body(pt_ref, q_ref, bias_ref, kvn_ref, pool_ref, o_ref, kvbuf, sem, m_s, l_s, acc_s,
                        *, layer, n_seq, n_steps, pg):
    b, s = pl.program_id(0), pl.program_id(1)
    t = b * n_steps + s
    slot = t % 2
    keys = pg * PAGE_SIZE
    past = n_steps * keys

    @pl.when(t == 0)
    def _():
        for cp in _page_copies(pool_ref, kvbuf, sem, pt_ref, layer, 0, 0, pg, 0):
            cp.start()

    @pl.when(t + 1 < n_seq * n_steps)
    def _():
        wrap = s + 1 == n_steps
        nb = jnp.where(wrap, b + 1, b)
        ns = jnp.where(wrap, 0, s + 1)
        for cp in _page_copies(pool_ref, kvbuf, sem, pt_ref, layer, nb, ns * pg, pg, 1 - slot):
            cp.start()

    for cp in _page_copies(pool_ref, kvbuf, sem, pt_ref, layer, b, s * pg, pg, slot):
        cp.wait()

    @pl.when(s == 0)
    def _():
        m_s[...] = jnp.full(m_s.shape, M_INIT, F32)
        l_s[...] = jnp.zeros(l_s.shape, F32)
        acc_s[...] = jnp.zeros(acc_s.shape, F32)

    qb = q_ref[0]
    kv = kvbuf[slot]
    bias = bias_ref[0, :, pl.ds(pl.multiple_of(s * keys, keys), keys)]
    sc = _dot(qb, kv[:KV_W].astype(BF)) + bias
    m_old = m_s[...]
    m_new = jnp.maximum(m_old, jnp.max(sc, axis=1, keepdims=True))
    alpha = jnp.exp(m_old - m_new)
    p = jnp.exp(sc - m_new)
    l_new = alpha * l_s[...] + jnp.sum(p, axis=1, keepdims=True)
    acc_new = alpha * acc_s[...] + _dot_nt(p.astype(BF), kv[KV_W:].astype(BF))
    m_s[...] = m_new
    l_s[...] = l_new
    acc_s[...] = acc_new

    @pl.when(s == n_steps - 1)
    def _():
        kn = kvn_ref[0].astype(BF).astype(F32)
        b_new = bias_ref[0, :, past:past + LANES][:, 0:1]
        s_new = jnp.sum(qb.astype(F32) * kn[:, :KV_W], axis=1, keepdims=True) + b_new
        m_fin = jnp.maximum(m_new, s_new)
        a_fin = jnp.exp(m_new - m_fin)
        p_new = jnp.exp(s_new - m_fin)
        l_fin = a_fin * l_new + p_new
        o_ref[0] = (a_fin * acc_new + p_new.astype(BF).astype(F32) * kn[:, KV_W:]) / l_fin


def _attn_a_sample(page_table, qblk, bias3, kvn3, pool_t, layer):
    n_seq, n_pages = page_table.shape
    pg = min(PAGES_PER_STEP, n_pages)
    assert n_pages % pg == 0
    n_steps = n_pages // pg
    width = bias3.shape[2]
    grid_spec = pltpu.PrefetchScalarGridSpec(
        num_scalar_prefetch=1,
        grid=(n_seq, n_steps),
        in_specs=[pl.BlockSpec((1, _SAMPLE_A_ROWS, KV_W), lambda b, s, pt: (b, 0, 0)),
                  pl.BlockSpec((1, 1, width), lambda b, s, pt: (b, 0, 0)),
                  pl.BlockSpec((1, 1, 2 * KV_W), lambda b, s, pt: (b, 0, 0)),
                  pl.BlockSpec(memory_space=pl.ANY)],
        out_specs=pl.BlockSpec((1, _SAMPLE_A_ROWS, KV_W), lambda b, s, pt: (b, 0, 0)),
        scratch_shapes=[pltpu.VMEM((2, 2 * KV_W, pg * PAGE_SIZE), F32),
                        pltpu.SemaphoreType.DMA((2,)),
                        pltpu.VMEM((_SAMPLE_A_ROWS, 1), F32),
                        pltpu.VMEM((_SAMPLE_A_ROWS, 1), F32),
                        pltpu.VMEM((_SAMPLE_A_ROWS, KV_W), F32)])
    return pl.pallas_call(
        functools.partial(_attn_a_sample_body, layer=layer, n_seq=n_seq, n_steps=n_steps, pg=pg),
        grid_spec=grid_spec,
        out_shape=jax.ShapeDtypeStruct((n_seq, _SAMPLE_A_ROWS, KV_W), F32),
        compiler_params=_cp(2),
        name="attn_a_sample",
    )(page_table, qblk, bias3, kvn3, pool_t)


def _attn_b_sample_body(q_ref, st0, st1, st2, kn0, kn1, kn2, kc0, kc1, kc2, o_ref, ns0, ns1, ns2, *, n_seq):
    b = pl.program_id(0)
    res = []
    groups = ((st0, kn0, kc0, ns0), (st1, kn1, kc1, ns1), (st2, kn2, kc2, ns2))
    for gi, (st_ref, kn_ref, kc_ref, ns_ref) in enumerate(groups):
        window, dil = B_GROUPS[gi]
        q = q_ref[0, gi * _SAMPLE_ROWS:(gi + 1) * _SAMPLE_ROWS, :]
        st = st_ref[...]
        kn = kn_ref[0].astype(BF).astype(F32)
        s = _dot(q, st[:KV_W].astype(BF))
        if dil > 1:
            lane = lax.broadcasted_iota(I32, s.shape, 1)
            s = jnp.where((lane & (dil - 1)) == 0, s, -jnp.inf)
        s_new = jnp.sum(q.astype(F32) * kn[:, :KV_W], axis=1, keepdims=True)
        mx = jnp.maximum(jnp.max(s, axis=1, keepdims=True), s_new)
        e = jnp.exp(s - mx)
        e_new = jnp.exp(s_new - mx)
        den = jnp.sum(e, axis=1, keepdims=True) + e_new
        o = (_dot_nt(e.astype(BF), st[KV_W:].astype(BF)) + e_new.astype(BF).astype(F32) * kn[:, KV_W:]) / den
        res.append((o, mx, den))
        kc = kc_ref[...]
        col = jnp.sum(jnp.where(lax.broadcasted_iota(I32, kc.shape, 1) == b, kc, 0.0), axis=1, keepdims=True)
        last = lax.broadcasted_iota(I32, st.shape, 1) == window - 1
        ns_ref[...] = jnp.where(last, col, pltpu.roll(st, window - 1, 1))
    mmax = jnp.maximum(jnp.maximum(res[0][1], res[1][1]), res[2][1])
    wgt = [r[2] * jnp.exp(r[1] - mmax) for r in res]
    num = wgt[0] * res[0][0] + wgt[1] * res[1][0] + wgt[2] * res[2][0]
    o_ref[0] = num / (wgt[0] + wgt[1] + wgt[2])


def _attn_b_sample(qblk3, states_t, kvns, kvcs, layer):
    n_seq = qblk3.shape[0]
    in_specs = [pl.BlockSpec((1, len(B_GROUPS) * _SAMPLE_ROWS, KV_W), lambda b: (b, 0, 0))]
    out_specs = [pl.BlockSpec((1, _SAMPLE_ROWS, KV_W), lambda b: (b, 0, 0))]
    out_shapes = [jax.ShapeDtypeStruct((n_seq, _SAMPLE_ROWS, KV_W), F32)]
    for st, (window, _) in zip(states_t, B_GROUPS):
        assert st.shape[2:] == (2 * KV_W, window)
        in_specs.append(pl.BlockSpec((None, None, 2 * KV_W, window), lambda b: (layer, b, 0, 0)))
        out_specs.append(pl.BlockSpec((None, 2 * KV_W, window), lambda b: (b, 0, 0)))
        out_shapes.append(jax.ShapeDtypeStruct((n_seq, 2 * KV_W, window), F32))
    in_specs += [pl.BlockSpec((1, 1, 2 * KV_W), lambda b: (b, 0, 0))] * len(B_GROUPS)
    in_specs += [pl.BlockSpec((2 * KV_W, n_seq), lambda b: (0, 0))] * len(B_GROUPS)
    return pl.pallas_call(
        functools.partial(_attn_b_sample_body, n_seq=n_seq),
        grid=(n_seq,),
        in_specs=in_specs,
        out_specs=out_specs,
        out_shape=out_shapes,
        compiler_params=_cp(1),
        name="attn_b_sample",
    )(qblk3, *states_t, *kvns, *kvcs)


def _memattn_sample_body(q_ref, mem_ref, o_ref):
    q = q_ref[0]
    mem = mem_ref[...].astype(BF)
    s = _dot(q, mem[:KV_W])
    e = jnp.exp(s - jnp.max(s, axis=1, keepdims=True))
    p = e / jnp.sum(e, axis=1, keepdims=True)
    o_ref[0] = _dot_nt(p.astype(BF), mem[KV_W:])


def _memattn_sample(mq3, mem_t, layer):
    n_seq, mem_len = mem_t.shape[1], mem_t.shape[3]
    return pl.pallas_call(
        _memattn_sample_body,
        grid=(n_seq,),
        in_specs=[pl.BlockSpec((1, _SAMPLE_ROWS, KV_W), lambda b: (b, 0, 0)),
                  pl.BlockSpec((None, None, 2 * KV_W, mem_len), lambda b: (layer, b, 0, 0))],
        out_specs=pl.BlockSpec((1, _SAMPLE_ROWS, KV_W), lambda b: (b, 0, 0)),
        out_shape=jax.ShapeDtypeStruct((n_seq, _SAMPLE_ROWS, KV_W), F32),
        compiler_params=_cp(1),
        name="memattn_sample",
    )(mq3, mem_t)


def _feature_major(x):
    nd = x.ndim
    xt = jnp.transpose(x, tuple(range(nd - 4)) + (nd - 3, nd - 2, nd - 1, nd - 4))
    return xt.reshape(x.shape[:nd - 4] + (2 * KV_W, x.shape[nd - 4]))


def _token_major(x, heads):
    nd = x.ndim
    xr = x.reshape(x.shape[:nd - 2] + (2, heads, HEAD_DIM, x.shape[nd - 1]))
    return jnp.transpose(xr, tuple(range(nd - 2)) + (nd + 1, nd - 2, nd - 1, nd))


def kernel(x_prompt, x_sample, cache_kv_a, cache_kidx_a, state_kv_b1, state_kv_b2, state_kv_b3, cache_mem_kv, page_table, mem_prompt, ffn_norm, w_ffn_in, w_ffn_out, mix_norm, mem_norm, w_mem_kv, w_in_a, w_out_a, w_in_b, w_out_b, final_norm):
    batch, seq, d = x_prompt.shape
    n_seq = x_sample.shape[0]
    depth = ffn_norm.shape[0]
    n_pages = page_table.shape[1]
    past = n_pages * PAGE_SIZE
    mem_len = mem_prompt.shape[1]
    assert x_sample.shape[1] == 1 and seq % TM == 0
    ng = len(B_GROUPS)

    yp = x_prompt.reshape(batch * seq, d)
    ys = x_sample.reshape(n_seq, d)
    tab_p = _rope_tables(jnp.arange(seq))
    tab_s = _rope_tables(jnp.full((n_seq,), past, I32))
    w_in_bf = w_ffn_in.astype(BF)
    w_out_bf = w_ffn_out.astype(BF)
    mem_flat = mem_prompt.reshape(batch * mem_len, d)
    kv_pool_t = _feature_major(cache_kv_a)
    kidx_pool_t = jnp.swapaxes(cache_kidx_a, 2, 3)
    states_t = [_feature_major(s) for s in (state_kv_b1, state_kv_b2, state_kv_b3)]
    mem_cache_t = _feature_major(cache_mem_kv)

    kv_a_p, kidx_a_p, kv_a_s, kidx_a_s, mem_p = [], [], [], [], []
    b_p = ([], [], [])
    b_s = ([], [], [])
    for i in range(depth):
        j = i // 2
        yp = _ffn(yp, ffn_norm[i, 0], w_in_bf, w_out_bf, i, 0, TM)
        ys = _ffn(ys, ffn_norm[i, 0], w_in_bf, w_out_bf, i, 0, n_seq)
        mem_t = _memproj(mem_flat, mem_norm[i], w_mem_kv[i], batch)
        mem_p.append(mem_t)
        if i % 2 == 0:
            w = w_in_a[j]
            q_al, k_bf, iq_al, ik128, mq_al, kv_t, v_t, ik_t, iw_t = _proj_a_prompt(yp, mix_norm[i], w, tab_p, batch)
            o = _attn_a_prompt(q_al, iq_al, iw_t, k_bf, ik128, v_t, batch, seq)
            om = _memattn_prompt(mq_al, mem_t, seq, TM)
            n_o = A_HEADS * HEAD_DIM
            yp = _outproj(yp, [], [o, om], [w_out_a[j][:n_o], w_out_a[j][n_o:]], TM)
            kv_a_p.append(kv_t)
            kidx_a_p.append(ik_t)

            qblk, kvn, iq_s, ikn, iw_s, mq_blk = _proj_a_sample(ys, mix_norm[i], w, tab_s)
            bias = _idx_sample(page_table, iq_s.reshape(n_seq, IDX_HEADS, IDX_DIM),
                               iw_s[:, :IDX_HEADS].reshape(n_seq, IDX_HEADS, 1),
                               ikn.reshape(n_seq, 1, IDX_DIM), kidx_pool_t, j)
            o_s = _attn_a_sample(page_table, qblk.reshape(n_seq, _SAMPLE_A_ROWS, KV_W),
                                 bias.reshape(n_seq, 1, past + LANES), kvn.reshape(n_seq, 1, 2 * KV_W),
                                 kv_pool_t, j)
            om_s = _memattn_sample(mq_blk.reshape(n_seq, _SAMPLE_ROWS, KV_W), mem_cache_t, i)
            ys = _outproj(ys, [], [o_s.reshape(n_seq, -1), om_s.reshape(n_seq, -1)],
                          [_expand_out(w_out_a[j][:n_o], _A_Q_SLOT, _SAMPLE_A_ROWS),
                           _expand_out(w_out_a[j][n_o:], _MEM_SLOT, _SAMPLE_ROWS)], n_seq)
            kv_a_s.append(kvn.reshape(n_seq, 1, 2, A_KV_HEADS, HEAD_DIM))
            kidx_a_s.append(ikn.reshape(n_seq, 1, IDX_DIM))
        else:
            w = w_in_b[j]
            outs = _proj_b_prompt(yp, mix_norm[i], w, tab_p, batch)
            stats_o, stats_m, stats_d = [], [], []
            for gi, (window, dil) in enumerate(B_GROUPS):
                og, mg, dg = _attn_b_prompt(outs[2 * gi], outs[2 * gi + 1], batch, seq, window, dil)
                stats_o.append(og)
                stats_m.append(mg)
                stats_d.append(dg)
                b_p[gi].append(outs[2 * ng + 1 + gi][:, :, seq - min(window, seq):])
            om = _memattn_prompt(outs[2 * ng], mem_t, seq, TM)
            n_o = B_HEADS * HEAD_DIM
            yp = _outproj(yp, stats_o + stats_m + stats_d, [om], [w_out_b[j][:n_o], w_out_b[j][n_o:]], TM)

            souts = _proj_b_sample(ys, mix_norm[i], w, tab_s)
            kvns = [souts[1 + gi].reshape(n_seq, 1, 2 * KV_W) for gi in range(ng)]
            kvcs = [souts[2 + ng + gi] for gi in range(ng)]
            o_s, *new_states = _attn_b_sample(souts[0].reshape(n_seq, ng * _SAMPLE_ROWS, KV_W),
                                              states_t, kvns, kvcs, j)
            om_s = _memattn_sample(souts[1 + ng].reshape(n_seq, _SAMPLE_ROWS, KV_W), mem_cache_t, i)
            ys = _outproj(ys, [], [o_s.reshape(n_seq, -1), om_s.reshape(n_seq, -1)],
                          [_expand_out(w_out_b[j][:n_o], _B_SLOT, _SAMPLE_ROWS),
                           _expand_out(w_out_b[j][n_o:], _MEM_SLOT, _SAMPLE_ROWS)], n_seq)
            for gi in range(ng):
                b_s[gi].append(new_states[gi])
        yp = _ffn(yp, ffn_norm[i, 1], w_in_bf, w_out_bf, i, 1, TM)
        ys = _ffn(ys, ffn_norm[i, 1], w_in_bf, w_out_bf, i, 1, n_seq)

    y_prompt = _final_norm(yp, final_norm, TM).reshape(batch, seq, d)
    y_sample = _final_norm(ys, final_norm, n_seq).reshape(n_seq, 1, d)
    return (y_prompt, y_sample,
            _token_major(jnp.stack(kv_a_p), A_KV_HEADS), jnp.swapaxes(jnp.stack(kidx_a_p), 2, 3),
            _token_major(jnp.stack(b_p[0]), B_HEADS), _token_major(jnp.stack(b_p[1]), B_HEADS),
            _token_major(jnp.stack(b_p[2]), B_HEADS), _token_major(jnp.stack(mem_p), MEM_HEADS),
            jnp.stack(kv_a_s), jnp.stack(kidx_a_s),
            _token_major(jnp.stack(b_s[0]), B_HEADS), _token_major(jnp.stack(b_s[1]), B_HEADS),
            _token_major(jnp.stack(b_s[2]), B_HEADS))
```

```python
import functools

import jax
import jax.numpy as jnp
import numpy as np
from jax import lax
from jax.experimental import pallas as pl
from jax.experimental.pallas import tpu as pltpu

F32 = jnp.float32
BF = jnp.bfloat16
I32 = jnp.int32

HEAD_DIM = 64
ROT_DIM = HEAD_DIM // 4
ROT_HALF = ROT_DIM // 2
ROPE_THETA = 500000.0
ATTN_SCALE = HEAD_DIM ** -0.5
A_HEADS = 12
A_KV_HEADS = 4
A_GROUP = A_HEADS // A_KV_HEADS
IDX_HEADS = 8
IDX_DIM = 64
IDX_SCALE = (IDX_HEADS * IDX_DIM) ** -0.5
TOPK_MAX = 256
B_GROUPS = ((128, 1), (512, 4), (2048, 16))
B_HEADS = 4
MEM_HEADS = 4
EPS = 1e-6
PAGE_SIZE = 128

LANES = 128
SUBLANES = 8
PAIR = 2 * HEAD_DIM
KV_W = A_KV_HEADS * HEAD_DIM
VMEM_LIMIT_V7X = 56 * 2 ** 20
NEG_BIG = -1e30
M_INIT = -3e38
INT_MAX = 2 ** 31 - 1
KEY_NEG_INF = -2139095041
MAX_SEARCH_PASSES = 32

TM = 512
TQ = 1024
KC = 512
CLS = 128
PAGES_PER_STEP = 16
ONES_ROWS = 16


def _cp(n_axes):
    return pltpu.CompilerParams(dimension_semantics=("arbitrary",) * n_axes,
                                vmem_limit_bytes=VMEM_LIMIT_V7X)


def _dot(a, b):
    return jnp.dot(a, b, preferred_element_type=F32)


def _dot_nt(a, b):
    return lax.dot_general(a, b, (((1,), (1,)), ((), ())), preferred_element_type=F32)


def _rms(x, g):
    return x * lax.rsqrt(jnp.mean(x * x, axis=-1, keepdims=True) + EPS) * g


def _sort_key(x):
    b = pltpu.bitcast(x + 0.0, I32)
    return b ^ ((b >> 31) & 0x7FFFFFFF)


def _resident(shape):
    return pl.BlockSpec(shape, lambda *_: (0,) * len(shape), pipeline_mode=pl.Buffered(1))


def _ffn_body(y_ref, g_ref, win_ref, wout_ref, o_ref, *, d_ff, fc):
    x = y_ref[...]
    xn = _rms(x, g_ref[...]).astype(BF)
    acc = jnp.zeros(x.shape, F32)
    for c in range(d_ff // fc):
        a = _dot(xn, win_ref[:, c * fc:(c + 1) * fc])
        u = _dot(xn, win_ref[:, d_ff + c * fc:d_ff + (c + 1) * fc])
        h = (a * jax.nn.sigmoid(a) * u).astype(BF)
        acc = acc + _dot(h, wout_ref[c * fc:(c + 1) * fc, :])
    o_ref[...] = x + 0.5 * acc


def _ffn(y, g, w_in_all, w_out_all, layer, half, tm):
    m, d = y.shape
    d_ff = w_out_all.shape[2]
    fc = 256 if d_ff % 256 == 0 else LANES
    assert d_ff % fc == 0 and m % tm == 0
    return pl.pallas_call(
        functools.partial(_ffn_body, d_ff=d_ff, fc=fc),
        grid=(m // tm,),
        in_specs=[pl.BlockSpec((tm, d), lambda i: (i, 0)),
                  pl.BlockSpec((1, d), lambda i: (0, 0)),
                  pl.BlockSpec((None, None, d, 2 * d_ff), lambda i: (layer, half, 0, 0),
                               pipeline_mode=pl.Buffered(1)),
                  pl.BlockSpec((None, None, d_ff, d), lambda i: (layer, half, 0, 0),
                               pipeline_mode=pl.Buffered(1))],
        out_specs=pl.BlockSpec((tm, d), lambda i: (i, 0)),
        out_shape=jax.ShapeDtypeStruct((m, d), F32),
        compiler_params=_cp(1),
        name="ffn",
    )(y, g.reshape(1, d), w_in_all, w_out_all)


def _rope_tables(pos):
    inv = jnp.power(ROPE_THETA, -jnp.arange(ROT_HALF, dtype=F32) / ROT_HALF)
    ang = pos.astype(F32)[:, None] * inv[None, :]
    cos, sin = jnp.cos(ang), jnp.sin(ang)
    t = pos.shape[0]
    ones = jnp.ones((t, HEAD_DIM - ROT_DIM), F32)
    zeros = jnp.zeros((t, HEAD_DIM - ROT_DIM), F32)
    zh = jnp.zeros((t, ROT_HALF), F32)
    cos64 = jnp.concatenate([cos, cos, ones], axis=1)
    sa64 = jnp.concatenate([-sin, zh, zeros], axis=1)
    sb64 = jnp.concatenate([zh, sin, zeros], axis=1)
    row = tuple(jnp.tile(a, (1, LANES // HEAD_DIM)) for a in (cos64, sa64, sb64))
    return row + (cos.T, sin.T)


def _proj_body(*refs, segs, nts, n_in):
    y_ref, g_ref, cos_ref, sa_ref, sb_ref, cost_ref, sint_ref = refs[:7]
    w_refs = refs[7:7 + len(segs)]
    wt_refs = refs[7 + len(segs):n_in]
    outs = refs[n_in:]
    xn = _rms(y_ref[...], g_ref[...]).astype(BF)
    for w_ref, (lo, hi, rope, scale, writes) in zip(w_refs, segs):
        p = _dot(xn, w_ref[...])
        if rope:
            cos, sa, sb = cos_ref[...], sa_ref[...], sb_ref[...]
            chunks = []
            for c in range((hi - lo) // LANES):
                pc = p[:, c * LANES:(c + 1) * LANES]
                chunks.append(pc * cos + pltpu.roll(pc, LANES - ROT_HALF, 1) * sa
                              + pltpu.roll(pc, ROT_HALF, 1) * sb)
            p = chunks[0] if len(chunks) == 1 else jnp.concatenate(chunks, axis=1)
        if scale != 1.0:
            p = p * scale
        for oi, slo, shi, dlo in writes:
            if len(outs[oi].shape) == 3:
                for c in range((shi - slo) // LANES):
                    outs[oi][dlo // LANES + c] = p[:, slo + c * LANES:slo + (c + 1) * LANES].astype(outs[oi].dtype)
            else:
                outs[oi][:, dlo:dlo + (shi - slo)] = p[:, slo:shi].astype(outs[oi].dtype)
    for t, (rope_rows, scale, writes) in enumerate(nts):
        r = _dot_nt(wt_refs[t][...], xn)
        if rope_rows:
            ct, st = cost_ref[...], sint_ref[...]
            pieces = []
            for h0 in range(0, rope_rows, HEAD_DIM):
                x1, x2 = r[h0:h0 + ROT_HALF], r[h0 + ROT_HALF:h0 + ROT_DIM]
                pieces += [x1 * ct - x2 * st, x2 * ct + x1 * st, r[h0 + ROT_DIM:h0 + HEAD_DIM]]
            if rope_rows < r.shape[0]:
                pieces.append(r[rope_rows:])
            r = jnp.concatenate(pieces, axis=0)
        if scale != 1.0:
            r = r * scale
        for oi, slo, shi, dlo in writes:
            outs[oi][dlo:dlo + (shi - slo), :] = r[slo:shi].astype(outs[oi].dtype)


def _proj(y, g, tables, n_seqs, w, wts, segs, nts, out_defs, tm):
    m, d = y.shape
    assert m % tm == 0
    per_seq = m // n_seqs // tm if tables[0].shape[0] != tm else 1
    n_in = 7 + len(w) + len(wts)
    pos_row = lambda i: (i % per_seq, 0)
    pos_col = lambda i: (0, i % per_seq)
    in_specs = [pl.BlockSpec((tm, d), lambda i: (i, 0)),
                pl.BlockSpec((1, d), lambda i: (0, 0)),
                pl.BlockSpec((tm, LANES), pos_row),
                pl.BlockSpec((tm, LANES), pos_row),
                pl.BlockSpec((tm, LANES), pos_row),
                pl.BlockSpec((ROT_HALF, tm), pos_col),
                pl.BlockSpec((ROT_HALF, tm), pos_col)]
    in_specs += [_resident(a.shape) for a in w] + [_resident(wt.shape) for wt in wts]
    out_specs, out_shapes = [], []
    for kind, size, dtype in out_defs:
        if kind == "row":
            out_specs.append(pl.BlockSpec((tm, size), lambda i: (i, 0)))
            out_shapes.append(jax.ShapeDtypeStruct((m, size), dtype))
        elif kind == "chunks":
            out_specs.append(pl.BlockSpec((size // LANES, tm, LANES), lambda i: (0, i, 0)))
            out_shapes.append(jax.ShapeDtypeStruct((size // LANES, m, LANES), dtype))
        elif kind == "col":
            out_specs.append(pl.BlockSpec((size, tm), lambda i: (0, i)))
            out_shapes.append(jax.ShapeDtypeStruct((size, m), dtype))
        else:
            out_specs.append(pl.BlockSpec((None, size, tm), lambda i: (i // per_seq, 0, i % per_seq)))
            out_shapes.append(jax.ShapeDtypeStruct((n_seqs, size, m // n_seqs), dtype))
    return pl.pallas_call(
        functools.partial(_proj_body, segs=tuple(segs), nts=tuple(nts), n_in=n_in),
        grid=(m // tm,),
        in_specs=in_specs,
        out_specs=out_specs,
        out_shape=out_shapes,
        compiler_params=_cp(1),
        name="proj",
    )(y, g.reshape(1, d), *tables, *w, *wts)


def _heads(w, lo, n_heads):
    return w[:, lo:lo + n_heads * HEAD_DIM].reshape(w.shape[0], n_heads, HEAD_DIM)


def _aligned(wh, halves):
    d, h, _ = wh.shape
    z = jnp.zeros_like(wh)
    sel = jnp.asarray(halves, I32).reshape(1, h, 1)
    lo = jnp.where(sel == 0, wh, z)
    hi = jnp.where(sel == 1, wh, z)
    return jnp.concatenate([lo, hi], axis=2).reshape(d, h * PAIR)


def _blocked(wh, slots, n_rows):
    d, h, _ = wh.shape
    onehot = jnp.asarray(np.eye(A_KV_HEADS, dtype=np.float32)[np.asarray(slots)], wh.dtype)
    blk = (wh[:, :, None, :] * onehot[None, :, :, None]).reshape(d, h * KV_W)
    return jnp.pad(blk, ((0, 0), (0, (n_rows - h) * KV_W)))


def _expand_out(w_rows, slots, n_rows):
    h = len(slots)
    d = w_rows.shape[1]
    wh = w_rows.reshape(h, HEAD_DIM, d)
    onehot = jnp.asarray(np.eye(A_KV_HEADS, dtype=np.float32)[np.asarray(slots)], w_rows.dtype)
    blk = (wh[:, None, :, :] * onehot[:, :, None, None]).reshape(h * KV_W, d)
    return jnp.pad(blk, ((0, (n_rows - h) * KV_W), (0, 0)))


_A_OFF = np.cumsum([0, A_HEADS * HEAD_DIM, KV_W, KV_W, IDX_HEADS * IDX_DIM, IDX_DIM, IDX_HEADS]).tolist()
_A_Q_HALF = [(h // A_GROUP) % 2 for h in range(A_HEADS)]
_A_Q_SLOT = [h // A_GROUP for h in range(A_HEADS)]
_MEM_HALF = [h % 2 for h in range(MEM_HEADS)]
_MEM_SLOT = list(range(MEM_HEADS))
_B_HALF = [h % 2 for h in range(B_HEADS)]
_B_SLOT = list(range(B_HEADS))
_SAMPLE_A_ROWS = 16
_SAMPLE_ROWS = 8


def _segments(items):
    segs, cols, lo = [], [], 0
    for wc, rope, scale, writes in items:
        hi = lo + wc.shape[1]
        segs.append((lo, hi, rope, scale, tuple(writes)))
        cols.append(wc.astype(BF))
        lo = hi
    return cols, segs


def _proj_a_prompt(y, g, w, tables, batch):
    o = _A_OFF
    zpad = jnp.zeros((w.shape[0], LANES - IDX_DIM), w.dtype)
    items = [
        (_aligned(_heads(w, o[0], A_HEADS), _A_Q_HALF), True, ATTN_SCALE, [(0, 0, A_HEADS * PAIR, 0)]),
        (w[:, o[1]:o[2]], True, 1.0, [(1, 0, KV_W, 0)]),
        (_aligned(_heads(w, o[3], IDX_HEADS), [0] * IDX_HEADS), True, 1.0, [(2, 0, IDX_HEADS * PAIR, 0)]),
        (jnp.concatenate([w[:, o[4]:o[5]], zpad], axis=1), True, 1.0, [(3, 0, LANES, 0)]),
        (_aligned(_heads(w, o[6], MEM_HEADS), _MEM_HALF), False, ATTN_SCALE, [(4, 0, MEM_HEADS * PAIR, 0)]),
    ]
    wcat, segs = _segments(items)
    wts = [w[:, o[1]:o[3]].T.astype(BF), w[:, o[4]:o[5]].T.astype(BF), w[:, o[5]:o[6]].T.astype(BF)]
    nts = [(KV_W, 1.0, ((5, 0, 2 * KV_W, 0), (6, KV_W, 2 * KV_W, 0))),
           (IDX_DIM, 1.0, ((7, 0, IDX_DIM, 0),)),
           (0, IDX_SCALE, ((8, 0, IDX_HEADS, 0),))]
    out_defs = [("row", A_HEADS * PAIR, BF), ("row", KV_W, BF), ("row", IDX_HEADS * PAIR, BF),
                ("row", LANES, BF), ("row", MEM_HEADS * PAIR, BF), ("bcol", 2 * KV_W, F32),
                ("col", KV_W, BF), ("bcol", IDX_DIM, F32), ("col", IDX_HEADS, F32)]
    return _proj(y, g, tables, batch, wcat, wts, segs, nts, out_defs, TM)


def _proj_a_sample(y, g, w, tables):
    o = _A_OFF
    zpad = jnp.zeros((w.shape[0], LANES - IDX_DIM), w.dtype)
    wpad = jnp.zeros((w.shape[0], LANES - IDX_HEADS), w.dtype)
    items = [
        (_blocked(_heads(w, o[0], A_HEADS), _A_Q_SLOT, _SAMPLE_A_ROWS), True, ATTN_SCALE,
         [(0, 0, _SAMPLE_A_ROWS * KV_W, 0)]),
        (w[:, o[1]:o[2]], True, 1.0, [(1, 0, KV_W, 0)]),
        (w[:, o[2]:o[3]], False, 1.0, [(1, 0, KV_W, KV_W)]),
        (w[:, o[3]:o[4]], True, 1.0, [(2, 0, IDX_HEADS * IDX_DIM, 0)]),
        (jnp.concatenate([w[:, o[4]:o[5]], zpad], axis=1), True, 1.0, [(3, 0, IDX_DIM, 0)]),
        (jnp.concatenate([w[:, o[5]:o[6]], wpad], axis=1), False, IDX_SCALE, [(4, 0, LANES, 0)]),
        (_blocked(_heads(w, o[6], MEM_HEADS), _MEM_SLOT, _SAMPLE_ROWS), False, ATTN_SCALE,
         [(5, 0, _SAMPLE_ROWS * KV_W, 0)]),
    ]
    wcat, segs = _segments(items)
    out_defs = [("row", _SAMPLE_A_ROWS * KV_W, BF), ("row", 2 * KV_W, F32), ("row", IDX_HEADS * IDX_DIM, BF),
                ("row", IDX_DIM, F32), ("row", LANES, F32), ("row", _SAMPLE_ROWS * KV_W, BF)]
    return _proj(y, g, tables, 1, wcat, [], segs, [], out_defs, y.shape[0])


def _b_off(g, t):
    return (g * 3 + t) * B_HEADS * HEAD_DIM


def _proj_b_prompt(y, g, w, tables, batch):
    ng = len(B_GROUPS)
    items, out_defs, wts, nts = [], [], [], []
    for gi in range(ng):
        items += [
            (_aligned(_heads(w, _b_off(gi, 0), B_HEADS), _B_HALF), True, ATTN_SCALE,
             [(2 * gi, 0, B_HEADS * PAIR, 0)]),
            (w[:, _b_off(gi, 1):_b_off(gi, 1) + KV_W], True, 1.0, [(2 * gi + 1, 0, KV_W, 0)]),
            (w[:, _b_off(gi, 2):_b_off(gi, 2) + KV_W], False, 1.0, [(2 * gi + 1, 0, KV_W, KV_W)]),
        ]
        out_defs += [("chunks", B_HEADS * PAIR, F32), ("chunks", 2 * KV_W, F32)]
    items.append((_aligned(_heads(w, _b_off(ng, 0), MEM_HEADS), _MEM_HALF), False, ATTN_SCALE,
                  [(2 * ng, 0, MEM_HEADS * PAIR, 0)]))
    out_defs.append(("row", MEM_HEADS * PAIR, BF))
    for gi in range(ng):
        wts.append(w[:, _b_off(gi, 1):_b_off(gi, 1) + 2 * KV_W].T.astype(BF))
        nts.append((KV_W, 1.0, ((2 * ng + 1 + gi, 0, 2 * KV_W, 0),)))
        out_defs.append(("bcol", 2 * KV_W, F32))
    wcat, segs = _segments(items)
    return _proj(y, g, tables, batch, wcat, wts, segs, nts, out_defs, TM)


def _proj_b_sample(y, g, w, tables):
    ng = len(B_GROUPS)
    items, out_defs, wts, nts = [], [("row", ng * _SAMPLE_ROWS * KV_W, BF)], [], []
    for gi in range(ng):
        items += [
            (_blocked(_heads(w, _b_off(gi, 0), B_HEADS), _B_SLOT, _SAMPLE_ROWS), True, ATTN_SCALE,
             [(0, 0, _SAMPLE_ROWS * KV_W, gi * _SAMPLE_ROWS * KV_W)]),
            (w[:, _b_off(gi, 1):_b_off(gi, 1) + KV_W], True, 1.0, [(1 + gi, 0, KV_W, 0)]),
            (w[:, _b_off(gi, 2):_b_off(gi, 2) + KV_W], False, 1.0, [(1 + gi, 0, KV_W, KV_W)]),
        ]
        out_defs.append(("row", 2 * KV_W, F32))
    items.append((_blocked(_heads(w, _b_off(ng, 0), MEM_HEADS), _MEM_SLOT, _SAMPLE_ROWS), False, ATTN_SCALE,
                  [(1 + ng, 0, _SAMPLE_ROWS * KV_W, 0)]))
    out_defs.append(("row", _SAMPLE_ROWS * KV_W, BF))
    for gi in range(ng):
        wts.append(w[:, _b_off(gi, 1):_b_off(gi, 1) + 2 * KV_W].T.astype(BF))
        nts.append((KV_W, 1.0, ((2 + ng + gi, 0, 2 * KV_W, 0),)))
        out_defs.append(("col", 2 * KV_W, F32))
    wcat, segs = _segments(items)
    return _proj(y, g, tables, 1, wcat, wts, segs, nts, out_defs, y.shape[0])


def _memproj_body(y_ref, g_ref, wt_ref, o_ref):
    xn = _rms(y_ref[...], g_ref[...]).astype(BF)
    o_ref[...] = _dot_nt(wt_ref[...], xn)


def _memproj(mem_flat, g_all, w_all, batch):
    m, d = mem_flat.shape
    mem_len = m // batch
    depth, _, n = w_all.shape
    return pl.pallas_call(
        _memproj_body,
        grid=(depth, batch),
        in_specs=[pl.BlockSpec((mem_len, d), lambda l, b: (b, 0)),
                  pl.BlockSpec((None, 1, d), lambda l, b: (l, 0, 0)),
                  pl.BlockSpec((None, n, d), lambda l, b: (l, 0, 0))],
        out_specs=pl.BlockSpec((None, None, n, mem_len), lambda l, b: (l, b, 0, 0)),
        out_shape=jax.ShapeDtypeStruct((depth, batch, n, mem_len), F32),
        compiler_params=_cp(2),
        name="memproj",
    )(mem_flat, g_all.reshape(depth, 1, d), jnp.swapaxes(w_all, 1, 2).astype(BF))


def _select_bias(count, read, write, n_chunks, rows, k_sel, red_shape, key_axis):
    lo0 = KEY_NEG_INF + 1
    c_valid = count(lambda x: x >= lo0)
    c_zero = count(lambda x: x >= 0)
    pos = c_zero >= k_sel
    few = c_valid <= k_sel
    lo = jnp.where(pos, 0, lo0).astype(I32)
    hi = jnp.where(pos, INT_MAX, 0).astype(I32)
    c_lo = jnp.where(pos, c_zero, c_valid)
    c_hi = jnp.where(pos, 0, c_zero)
    done = jnp.where(jnp.logical_or(few, c_zero == k_sel), 1, 0).astype(I32)
    thr = jnp.where(few, lo0, 0).astype(I32)
    full = jnp.full(red_shape, k_sel, I32)

    def cond(st):
        return jnp.logical_and(st[0] < MAX_SEARCH_PASSES, st[1] == 0)

    def body(st):
        it, _, lo, hi, c_lo, c_hi, thr, need, n_ge, done = st
        cand = lo + jnp.maximum((hi - lo) >> 1, 1)
        c = count(lambda x: x >= cand)
        hit = c == k_sel
        up = c > k_sel
        lo_n = jnp.where(up, cand, lo)
        hi_n = jnp.where(up, hi, cand)
        c_lo_n = jnp.where(up, c, c_lo)
        c_hi_n = jnp.where(up, c_hi, c)
        newly = jnp.logical_and(done == 0, jnp.logical_or(hit, hi_n - lo_n == 1))
        thr_n = jnp.where(newly, jnp.where(hit, cand, lo_n), thr)
        need_n = jnp.where(jnp.logical_and(newly, jnp.logical_not(hit)), k_sel - c_hi_n, need)
        n_ge_n = jnp.where(jnp.logical_and(newly, jnp.logical_not(hit)), c_lo_n, n_ge)
        done_n = jnp.where(newly, 1, done)
        return (it + 1, jnp.min(done_n), lo_n, hi_n, c_lo_n, c_hi_n, thr_n, need_n, n_ge_n, done_n)

    st = lax.while_loop(cond, body, (jnp.int32(0), jnp.min(done), lo, hi, c_lo, c_hi, thr, full, full, done))
    thr, need_i, n_ge = st[6], st[7], st[8]
    has_tie = jnp.max(n_ge) > k_sel

    @pl.when(jnp.logical_not(has_tie))
    def _():
        def wbody(c, _):
            x = read(c)
            sel = jnp.logical_and(x >= thr, x > KEY_NEG_INF)
            write(c, jnp.where(sel, 0.0, NEG_BIG).astype(F32))
            return 0
        lax.fori_loop(0, n_chunks, wbody, 0)

    @pl.when(has_tie)
    def _():
        need = need_i.astype(F32)
        r = lax.broadcasted_iota(I32, (rows, rows), 0)
        cidx = lax.broadcasted_iota(I32, (rows, rows), 1)
        if key_axis == 0:
            tri = jnp.where(cidx < r, 1.0, 0.0).astype(BF)
        else:
            tri = jnp.where(r < cidx, 1.0, 0.0).astype(BF)

        def wbody(c, seen):
            x = read(c)
            eq = jnp.logical_and(x == thr, x > KEY_NEG_INF)
            eqf = jnp.where(eq, 1.0, 0.0).astype(BF)
            rank = (_dot(tri, eqf) if key_axis == 0 else _dot(eqf, tri)) + seen
            sel = jnp.logical_and(jnp.logical_or(x > thr, jnp.logical_and(eq, rank < need)), x > KEY_NEG_INF)
            write(c, jnp.where(sel, 0.0, NEG_BIG).astype(F32))
            return seen + jnp.sum(eqf.astype(F32), axis=key_axis, keepdims=True)
        lax.fori_loop(0, n_chunks, wbody, jnp.zeros(red_shape, F32))


def _attn_a_body(q_ref, iq_ref, iw_ref, k_ref, ik_ref, vt_ref, o_ref, sc_ref, acc_ref, *, tq, kc, k_sel):
    qi = pl.program_id(1)
    per_q = tq // kc
    iw = iw_ref[...]

    def chunk_rows(c):
        return pl.ds(pl.multiple_of(c * kc, kc), kc)

    def idx_chunk(c, diag_off):
        ikc = ik_ref[chunk_rows(c), :]
        acc = jnp.zeros((kc, tq), F32)
        for h in range(IDX_HEADS):
            s = _dot_nt(ikc, iq_ref[:, h * PAIR:(h + 1) * PAIR])
            acc = acc + jnp.maximum(s, 0.0) * iw[h:h + 1, :]
        if diag_off is not None:
            kpos = lax.broadcasted_iota(I32, (kc, tq), 0) + diag_off
            qpos = lax.broadcasted_iota(I32, (kc, tq), 1)
            acc = jnp.where(kpos <= qpos, acc, -jnp.inf)
        sc_ref[chunk_rows(c), :] = _sort_key(acc)

    def full_chunk(c, _):
        idx_chunk(c, None)
        return 0
    lax.fori_loop(0, qi * per_q, full_chunk, 0)
    for j in range(per_q):
        idx_chunk(qi * per_q + j, j * kc)
    n_chunks = (qi + 1) * per_q

    def read(c):
        return sc_ref[chunk_rows(c), :]

    def write(c, v):
        sc_ref[chunk_rows(c), :] = pltpu.bitcast(v, I32)

    def count(pred):
        def body(c, cnts):
            x = read(c)
            cnts = list(cnts)
            for r in range(kc // SUBLANES):
                a = r % len(cnts)
                cnts[a] = jnp.where(pred(x[r * SUBLANES:(r + 1) * SUBLANES]), cnts[a] + 1, cnts[a])
            return tuple(cnts)
        parts = lax.fori_loop(0, n_chunks, body, (jnp.zeros((SUBLANES, tq), I32),) * 4)
        return jnp.sum(parts[0] + parts[1] + parts[2] + parts[3], axis=0, keepdims=True)

    _select_bias(count, read, write, n_chunks, kc, k_sel, (1, tq), 0)

    acc_ref[...] = jnp.zeros(acc_ref.shape, F32)
    ones_rows = jnp.ones((ONES_ROWS, kc), BF)
    pad_rows = jnp.zeros((2 * SUBLANES - A_HEADS, tq), F32)

    def att_chunk(c, carry):
        m_all, l_all = carry
        rows_c = chunk_rows(c)
        bias = pltpu.bitcast(sc_ref[rows_c, :], F32)
        kcs = [k_ref[rows_c, pair * PAIR:(pair + 1) * PAIR] for pair in range(A_KV_HEADS // 2)]
        vts = [jnp.concatenate([vt_ref[n * HEAD_DIM:(n + 1) * HEAD_DIM, rows_c], ones_rows], axis=0)
               for n in range(A_KV_HEADS)]
        m_rows, l_rows = [], []
        for h in range(A_HEADS):
            n = h // A_GROUP
            s = _dot_nt(kcs[n // 2], q_ref[:, h * PAIR:(h + 1) * PAIR]) + bias
            m_old = m_all[h:h + 1]
            m_new = jnp.maximum(m_old, jnp.max(s, axis=0, keepdims=True))
            alpha = jnp.exp(m_old - m_new)
            p = jnp.exp(s - m_new).astype(BF)
            res = _dot(vts[n], p)
            rows = slice(h * HEAD_DIM, (h + 1) * HEAD_DIM)
            m_rows.append(m_new)
            l_rows.append(alpha * l_all[h:h + 1] + res[HEAD_DIM:HEAD_DIM + 1])
            acc_ref[rows, :] = alpha * acc_ref[rows, :] + res[:HEAD_DIM]
        return (jnp.concatenate(m_rows + [pad_rows], axis=0), jnp.concatenate(l_rows + [pad_rows], axis=0))

    init = (jnp.full((2 * SUBLANES, tq), M_INIT, F32), jnp.zeros((2 * SUBLANES, tq), F32))
    _, l_all = lax.fori_loop(0, n_chunks, att_chunk, init)

    for hp in range(A_HEADS // 2):
        outs = [acc_ref[h * HEAD_DIM:(h + 1) * HEAD_DIM, :] / l_all[h:h + 1] for h in (2 * hp, 2 * hp + 1)]
        o_ref[:, hp * PAIR:(hp + 1) * PAIR] = jnp.concatenate(outs, axis=0).T.astype(o_ref.dtype)


def _attn_a_prompt(q_al, iq_al, iw_t, k_bf, ik128, v_t, batch, seq):
    tq = min(TQ, seq)
    kc = min(KC, tq)
    nq = seq // tq
    m = batch * seq
    k_sel = min(TOPK_MAX, seq // 4)
    return pl.pallas_call(
        functools.partial(_attn_a_body, tq=tq, kc=kc, k_sel=k_sel),
        grid=(batch, nq),
        in_specs=[pl.BlockSpec((tq, A_HEADS * PAIR), lambda b, i: (b * nq + i, 0)),
                  pl.BlockSpec((tq, IDX_HEADS * PAIR), lambda b, i: (b * nq + i, 0)),
                  pl.BlockSpec((IDX_HEADS, tq), lambda b, i: (0, b * nq + i)),
                  pl.BlockSpec((seq, KV_W), lambda b, i: (b, 0)),
                  pl.BlockSpec((seq, LANES), lambda b, i: (b, 0)),
                  pl.BlockSpec((KV_W, seq), lambda b, i: (0, b))],
        out_specs=pl.BlockSpec((tq, A_HEADS * HEAD_DIM), lambda b, i: (b * nq + i, 0)),
        out_shape=jax.ShapeDtypeStruct((m, A_HEADS * HEAD_DIM), BF),
        scratch_shapes=[pltpu.VMEM((seq, tq), I32),
                        pltpu.VMEM((A_HEADS * HEAD_DIM, tq), F32)],
        compiler_params=_cp(2),
        name="attn_a_prompt",
    )(q_al, iq_al, iw_t, k_bf, ik128, v_t)


def _attn_b_body(q_ref, kvc_ref, kvp_ref, o_ref, m_ref, d_ref, *, dil, span):
    i = pl.program_id(1)
    qrow = lax.broadcasted_iota(I32, (CLS, 2 * CLS), 0)
    kcol = lax.broadcasted_iota(I32, (CLS, 2 * CLS), 1)
    rel = qrow + CLS - kcol
    valid = jnp.logical_and(jnp.logical_and(rel >= 0, rel <= span), jnp.logical_or(kcol >= CLS, i > 0))
    n_pairs = B_HEADS // 2
    valid4 = jnp.concatenate([valid] * B_HEADS, axis=0)
    slot = lax.broadcasted_iota(I32, (CLS, KV_W), 1) >> (HEAD_DIM.bit_length() - 1)
    zeros = jnp.zeros((CLS, PAIR), BF)

    def one_class(rows):
        def cat(base):
            return jnp.concatenate(
                [jnp.concatenate([kvp_ref[base + p, rows, :], kvc_ref[base + p, rows, :]], axis=0)
                 for p in range(n_pairs)], axis=1).astype(BF)
        kcat, vcat = cat(0), cat(n_pairs)
        qrows = []
        for h in range(B_HEADS):
            qh = q_ref[h, rows, :].astype(BF)
            qrows.append(jnp.concatenate([qh, zeros] if h // 2 == 0 else [zeros, qh], axis=1))
        s = jnp.where(valid4, _dot_nt(jnp.concatenate(qrows, axis=0), kcat), -jnp.inf)
        mx = jnp.max(s, axis=1, keepdims=True)
        e = jnp.exp(s - mx)
        den = jnp.sum(e, axis=1, keepdims=True)
        o = _dot(e.astype(BF), vcat) / den
        o_nat = jnp.zeros((CLS, KV_W), F32)
        m_nat = jnp.zeros((CLS, KV_W), F32)
        d_nat = jnp.zeros((CLS, KV_W), F32)
        for h in range(B_HEADS):
            blk = slice(h * CLS, (h + 1) * CLS)
            o_nat = jnp.where(slot == h, o[blk], o_nat)
            m_nat = jnp.where(slot == h, mx[blk], m_nat)
            d_nat = jnp.where(slot == h, den[blk], d_nat)
        for p in range(n_pairs):
            o_ref[p, rows, :] = o_nat[:, p * PAIR:(p + 1) * PAIR]
            m_ref[p, rows, :] = m_nat[:, p * PAIR:(p + 1) * PAIR]
            d_ref[p, rows, :] = d_nat[:, p * PAIR:(p + 1) * PAIR]

    if dil == 1:
        one_class(pl.ds(0, CLS))
    else:
        def body(r, _):
            one_class(pl.ds(r, CLS, stride=dil))
            return 0
        lax.fori_loop(0, dil, body, 0)


def _attn_b_prompt(q_al, kv, batch, seq, window, dil):
    rows = CLS * dil
    span = window // dil
    assert span <= CLS and seq % rows == 0
    nr = seq // rows
    m = batch * seq
    n_pairs = B_HEADS // 2
    stat = jax.ShapeDtypeStruct((n_pairs, m, PAIR), F32)
    ospec = pl.BlockSpec((n_pairs, rows, PAIR), lambda b, i: (0, b * nr + i, 0))
    return pl.pallas_call(
        functools.partial(_attn_b_body, dil=dil, span=span),
        grid=(batch, nr),
        in_specs=[pl.BlockSpec((B_HEADS, rows, PAIR), lambda b, i: (0, b * nr + i, 0)),
                  pl.BlockSpec((2 * n_pairs, rows, PAIR), lambda b, i: (0, b * nr + i, 0)),
                  pl.BlockSpec((2 * n_pairs, rows, PAIR), lambda b, i: (0, b * nr + jnp.maximum(i - 1, 0), 0))],
        out_specs=[ospec, ospec, ospec],
        out_shape=[stat, stat, stat],
        compiler_params=_cp(2),
        name="attn_b_prompt",
    )(q_al, kv, kv)


def _memattn_body(mq_ref, mem_ref, o_ref):
    mq = mq_ref[...]
    mem = mem_ref[...].astype(BF)
    lo_half = lax.broadcasted_iota(I32, (mq.shape[0], PAIR), 1) < HEAD_DIM
    outs = []
    for pair in range(MEM_HEADS // 2):
        mk = mem[pair * PAIR:(pair + 1) * PAIR]
        mv = mem[KV_W + pair * PAIR:KV_W + (pair + 1) * PAIR]
        res = []
        for h in (2 * pair, 2 * pair + 1):
            s = _dot(mq[:, h * PAIR:(h + 1) * PAIR], mk)
            e = jnp.exp(s - jnp.max(s, axis=1, keepdims=True))
            p = e / jnp.sum(e, axis=1, keepdims=True)
            res.append(_dot_nt(p.astype(BF), mv))
        outs.append(jnp.where(lo_half, res[0], res[1]))
    o_ref[...] = jnp.concatenate(outs, axis=1).astype(o_ref.dtype)


def _memattn_prompt(mq_al, mem_t, layer, seq, tm):
    m = mq_al.shape[0]
    mem_len = mem_t.shape[3]
    tiles_per_seq = seq // tm
    return pl.pallas_call(
        _memattn_body,
        grid=(m // tm,),
        in_specs=[pl.BlockSpec((tm, MEM_HEADS * PAIR), lambda i: (i, 0)),
                  pl.BlockSpec((None, None, 2 * KV_W, mem_len), lambda i: (layer, i // tiles_per_seq, 0, 0))],
        out_specs=pl.BlockSpec((tm, KV_W), lambda i: (i, 0)),
        out_shape=jax.ShapeDtypeStruct((m, KV_W), BF),
        compiler_params=_cp(1),
        name="memattn_prompt",
    )(mq_al, mem_t)


def _outproj_body(*refs, n_x, combine):
    y_ref = refs[0]
    pos = 1
    xs = []
    if combine:
        og, mg, dg = refs[1:4], refs[4:7], refs[7:10]
        pos = 10
        ms = [r[...] for r in mg]
        mmax = jnp.maximum(jnp.maximum(ms[0], ms[1]), ms[2])
        wgt = [dg[k][...] * jnp.exp(ms[k] - mmax) for k in range(3)]
        num = wgt[0] * og[0][...] + wgt[1] * og[1][...] + wgt[2] * og[2][...]
        merged = num / (wgt[0] + wgt[1] + wgt[2])
        xs.append(jnp.concatenate([merged[c] for c in range(merged.shape[0])], axis=1))
    xs += [refs[pos + k][...] for k in range(n_x)]
    w_refs = refs[pos + n_x:pos + n_x + len(xs)]
    o_ref = refs[-1]
    acc = y_ref[...]
    for x, w_ref in zip(xs, w_refs):
        acc = acc + _dot(x.astype(BF), w_ref[...])
    o_ref[...] = acc


def _outproj(y, stats, xs, ws, tm):
    m, d = y.shape
    row = lambda width: pl.BlockSpec((tm, width), lambda i: (i, 0))
    chunked = lambda a: pl.BlockSpec((a.shape[0], tm, LANES), lambda i: (0, i, 0))
    in_specs = [row(d)] + [chunked(a) for a in stats] + [row(x.shape[1]) for x in xs]
    in_specs += [_resident(w.shape) for w in ws]
    return pl.pallas_call(
        functools.partial(_outproj_body, n_x=len(xs), combine=bool(stats)),
        grid=(m // tm,),
        in_specs=in_specs,
        out_specs=row(d),
        out_shape=jax.ShapeDtypeStruct((m, d), F32),
        compiler_params=_cp(1),
        name="outproj",
    )(y, *stats, *xs, *[w.astype(BF) for w in ws])


def _final_norm_body(y_ref, g_ref, o_ref):
    o_ref[...] = _rms(y_ref[...], g_ref[...])


def _final_norm(y, g, tm):
    m, d = y.shape
    return pl.pallas_call(
        _final_norm_body,
        grid=(m // tm,),
        in_specs=[pl.BlockSpec((tm, d), lambda i: (i, 0)), pl.BlockSpec((1, d), lambda i: (0, 0))],
        out_specs=pl.BlockSpec((tm, d), lambda i: (i, 0)),
        out_shape=jax.ShapeDtypeStruct((m, d), F32),
        compiler_params=_cp(1),
        name="final_norm",
    )(y, g.reshape(1, d))


def _page_copies(pool_ref, buf_ref, sem_ref, pt_ref, layer, b, first_page, n_pages, slot):
    return [pltpu.make_async_copy(pool_ref.at[layer, pt_ref[b, first_page + p]],
                                  buf_ref.at[slot, :, pl.ds(p * PAGE_SIZE, PAGE_SIZE)],
                                  sem_ref.at[slot]) for p in range(n_pages)]


def _idx_sample_body(pt_ref, iq_ref, iw_ref, ikn_ref, pool_ref, bias_ref, kbuf, sem, sc_ref,
                     *, layer, n_seq, n_pages, k_sel):
    b = pl.program_id(0)
    past = n_pages * PAGE_SIZE
    slot = b % 2

    @pl.when(b == 0)
    def _():
        sc_ref[...] = jnp.zeros(sc_ref.shape, I32)
        for cp in _page_copies(pool_ref, kbuf, sem, pt_ref, layer, 0, 0, n_pages, 0):
            cp.start()

    @pl.when(b + 1 < n_seq)
    def _():
        for cp in _page_copies(pool_ref, kbuf, sem, pt_ref, layer, b + 1, 0, n_pages, 1 - slot):
            cp.start()

    for cp in _page_copies(pool_ref, kbuf, sem, pt_ref, layer, b, 0, n_pages, slot):
        cp.wait()

    iq = iq_ref[0]
    iw = iw_ref[0]
    s = _dot(iq, kbuf[slot].astype(BF))
    row = jnp.sum(jnp.maximum(s, 0.0) * iw, axis=0, keepdims=True)
    ikn = ikn_ref[0].astype(BF).astype(F32)
    s_new = jnp.sum(iq.astype(F32) * ikn, axis=1, keepdims=True)
    r_new = jnp.sum(jnp.maximum(s_new, 0.0) * iw, axis=0, keepdims=True)
    lane = lax.broadcasted_iota(I32, (1, LANES), 1)
    tail = jnp.where(lane == 0, r_new, -jnp.inf)
    is_b = lax.broadcasted_iota(I32, (n_seq, 1), 0) == b
    sc_ref[:, :past] = jnp.where(is_b, _sort_key(row), sc_ref[:, :past])
    sc_ref[:, past:] = jnp.where(is_b, _sort_key(tail), sc_ref[:, past:])

    @pl.when(b == n_seq - 1)
    def _():
        def read(c):
            return sc_ref[:, pl.ds(pl.multiple_of(c * LANES, LANES), LANES)]

        def write(c, v):
            bias_ref[:, pl.ds(pl.multiple_of(c * LANES, LANES), LANES)] = v

        def count(pred):
            return jnp.sum(jnp.where(pred(sc_ref[...]), 1, 0).astype(I32), axis=1, keepdims=True)

        _select_bias(count, read, write, past // LANES + 1, LANES, k_sel, (n_seq, 1), 1)


def _idx_sample(page_table, iq3, iw3, ikn3, pool_t, layer):
    n_seq, n_pages = page_table.shape
    past = n_pages * PAGE_SIZE
    width = past + LANES
    k_sel = min(TOPK_MAX, (past + 1) // 4)
    grid_spec = pltpu.PrefetchScalarGridSpec(
        num_scalar_prefetch=1,
        grid=(n_seq,),
        in_specs=[pl.BlockSpec((1, IDX_HEADS, IDX_DIM), lambda b, pt: (b, 0, 0)),
                  pl.BlockSpec((1, IDX_HEADS, 1), lambda b, pt: (b, 0, 0)),
                  pl.BlockSpec((1, 1, IDX_DIM), lambda b, pt: (b, 0, 0)),
                  pl.BlockSpec(memory_space=pl.ANY)],
        out_specs=pl.BlockSpec((n_seq, width), lambda b, pt: (0, 0)),
        scratch_shapes=[pltpu.VMEM((2, IDX_DIM, past), F32),
                        pltpu.SemaphoreType.DMA((2,)),
                        pltpu.VMEM((n_seq, width), I32)])
    return pl.pallas_call(
        functools.partial(_idx_sample_body, layer=layer, n_seq=n_seq, n_pages=n_pages, k_sel=k_sel),
        grid_spec=grid_spec,
        out_shape=jax.ShapeDtypeStruct((n_seq, width), F32),
        compiler_params=_cp(1),
        name="idx_sample",
    )(page_table, iq3, iw3, ikn3, pool_t)


def _attn_a_sample_body(pt_ref, q_ref, bias_ref, kvn_ref, pool_ref, o_ref, kvbuf, sem, m_s, l_s, acc_s,
                        *, layer, n_seq, n_steps, pg):
    b, s = pl.program_id(0), pl.program_id(1)
    t = b * n_steps + s
    slot = t % 2
    keys = pg * PAGE_SIZE
    past = n_steps * keys

    @pl.when(t == 0)
    def _():
        for cp in _page_copies(pool_ref, kvbuf, sem, pt_ref, layer, 0, 0, pg, 0):
            cp.start()

    @pl.when(t + 1 < n_seq * n_steps)
    def _():
        wrap = s + 1 == n_steps
        nb = jnp.where(wrap, b + 1, b)
        ns = jnp.where(wrap, 0, s + 1)
        for cp in _page_copies(pool_ref, kvbuf, sem, pt_ref, layer, nb, ns * pg, pg, 1 - slot):
            cp.start()

    for cp in _page_copies(pool_ref, kvbuf, sem, pt_ref, layer, b, s * pg, pg, slot):
        cp.wait()

    @pl.when(s == 0)
    def _():
        m_s[...] = jnp.full(m_s.shape, M_INIT, F32)
        l_s[...] = jnp.zeros(l_s.shape, F32)
        acc_s[...] = jnp.zeros(acc_s.shape, F32)

    qb = q_ref[0]
    kv = kvbuf[slot]
    bias = bias_ref[0, :, pl.ds(pl.multiple_of(s * keys, keys), keys)]
    sc = _dot(qb, kv[:KV_W].astype(BF)) + bias
    m_old = m_s[...]
    m_new = jnp.maximum(m_old, jnp.max(sc, axis=1, keepdims=True))
    alpha = jnp.exp(m_old - m_new)
    p = jnp.exp(sc - m_new)
    l_new = alpha * l_s[...] + jnp.sum(p, axis=1, keepdims=True)
    acc_new = alpha * acc_s[...] + _dot_nt(p.astype(BF), kv[KV_W:].astype(BF))
    m_s[...] = m_new
    l_s[...] = l_new
    acc_s[...] = acc_new

    @pl.when(s == n_steps - 1)
    def _():
        kn = kvn_ref[0].astype(BF).astype(F32)
        b_new = bias_ref[0, :, past:past + LANES][:, 0:1]
        s_new = jnp.sum(qb.astype(F32) * kn[:, :KV_W], axis=1, keepdims=True) + b_new
        m_fin = jnp.maximum(m_new, s_new)
        a_fin = jnp.exp(m_new - m_fin)
        p_new = jnp.exp(s_new - m_fin)
        l_fin = a_fin * l_new + p_new
        o_ref[0] = (a_fin * acc_new + p_new.astype(BF).astype(F32) * kn[:, KV_W:]) / l_fin


def _attn_a_sample(page_table, qblk, bias3, kvn3, pool_t, layer):
    n_seq, n_pages = page_table.shape
    pg = min(PAGES_PER_STEP, n_pages)
    assert n_pages % pg == 0
    n_steps = n_pages // pg
    width = bias3.shape[2]
    grid_spec = pltpu.PrefetchScalarGridSpec(
        num_scalar_prefetch=1,
        grid=(n_seq, n_steps),
        in_specs=[pl.BlockSpec((1, _SAMPLE_A_ROWS, KV_W), lambda b, s, pt: (b, 0, 0)),
                  pl.BlockSpec((1, 1, width), lambda b, s, pt: (b, 0, 0)),
                  pl.BlockSpec((1, 1, 2 * KV_W), lambda b, s, pt: (b, 0, 0)),
                  pl.BlockSpec(memory_space=pl.ANY)],
        out_specs=pl.BlockSpec((1, _SAMPLE_A_ROWS, KV_W), lambda b, s, pt: (b, 0, 0)),
        scratch_shapes=[pltpu.VMEM((2, 2 * KV_W, pg * PAGE_SIZE), F32),
                        pltpu.SemaphoreType.DMA((2,)),
                        pltpu.VMEM((_SAMPLE_A_ROWS, 1), F32),
                        pltpu.VMEM((_SAMPLE_A_ROWS, 1), F32),
                        pltpu.VMEM((_SAMPLE_A_ROWS, KV_W), F32)])
    return pl.pallas_call(
        functools.partial(_attn_a_sample_body, layer=layer, n_seq=n_seq, n_steps=n_steps, pg=pg),
        grid_spec=grid_spec,
        out_shape=jax.ShapeDtypeStruct((n_seq, _SAMPLE_A_ROWS, KV_W), F32),
        compiler_params=_cp(2),
        name="attn_a_sample",
    )(page_table, qblk, bias3, kvn3, pool_t)


def _attn_b_sample_body(*refs, n_seq, n_alias, layer, n_fill):
    ng = len(B_GROUPS)
    nss = refs[2 + 3 * ng + n_alias:]
    if n_fill == 1:
        _attn_b_sample_compute(refs, n_alias)
        return
    fill = pl.program_id(1)

    @pl.when(fill == layer)
    def _():
        _attn_b_sample_compute(refs, n_alias)

    @pl.when(fill != layer)
    def _():
        for ns_ref in nss:
            ns_ref[...] = jnp.zeros(ns_ref.shape, F32)


def _attn_b_sample_compute(refs, n_alias):
    ng = len(B_GROUPS)
    q_ref = refs[0]
    sts, kns, kcs = refs[1:1 + ng], refs[1 + ng:1 + 2 * ng], refs[1 + 2 * ng:1 + 3 * ng]
    o_ref = refs[1 + 3 * ng + n_alias]
    nss = refs[2 + 3 * ng + n_alias:]
    b = pl.program_id(0)
    res = []
    groups = tuple(zip(sts, kns, kcs, nss))
    for gi, (st_ref, kn_ref, kc_ref, ns_ref) in enumerate(groups):
        window, dil = B_GROUPS[gi]
        q = q_ref[0, gi * _SAMPLE_ROWS:(gi + 1) * _SAMPLE_ROWS, :]
        st = st_ref[...]
        kn = kn_ref[0].astype(BF).astype(F32)
        s = _dot(q, st[:KV_W].astype(BF))
        if dil > 1:
            lane = lax.broadcasted_iota(I32, s.shape, 1)
            s = jnp.where((lane & (dil - 1)) == 0, s, -jnp.inf)
        s_new = jnp.sum(q.astype(F32) * kn[:, :KV_W], axis=1, keepdims=True)
        mx = jnp.maximum(jnp.max(s, axis=1, keepdims=True), s_new)
        e = jnp.exp(s - mx)
        e_new = jnp.exp(s_new - mx)
        den = jnp.sum(e, axis=1, keepdims=True) + e_new
        o = (_dot_nt(e.astype(BF), st[KV_W:].astype(BF)) + e_new.astype(BF).astype(F32) * kn[:, KV_W:]) / den
        res.append((o, mx, den))
        kc = kc_ref[...]
        col = jnp.sum(jnp.where(lax.broadcasted_iota(I32, kc.shape, 1) == b, kc, 0.0), axis=1, keepdims=True)
        last = lax.broadcasted_iota(I32, st.shape, 1) == window - 1
        ns_ref[...] = jnp.where(last, col, pltpu.roll(st, window - 1, 1))
    mmax = jnp.maximum(jnp.maximum(res[0][1], res[1][1]), res[2][1])
    wgt = [r[2] * jnp.exp(r[1] - mmax) for r in res]
    num = wgt[0] * res[0][0] + wgt[1] * res[1][0] + wgt[2] * res[2][0]
    o_ref[0] = num / (wgt[0] + wgt[1] + wgt[2])


def _attn_b_sample(qblk3, states_t, kvns, kvcs, layer, prev_new_states):
    n_seq = qblk3.shape[0]
    ng = len(B_GROUPS)
    n_fill = 1 if prev_new_states else states_t[0].shape[0]
    in_specs = [pl.BlockSpec((1, ng * _SAMPLE_ROWS, KV_W), lambda b, f: (b, 0, 0))]
    out_specs = [pl.BlockSpec((1, _SAMPLE_ROWS, KV_W), lambda b, f: (b, 0, 0))]
    out_shapes = [jax.ShapeDtypeStruct((n_seq, _SAMPLE_ROWS, KV_W), F32)]
    out_map = (lambda b, f: (layer, b, 0, 0)) if n_fill == 1 else (lambda b, f: (f, b, 0, 0))
    for st, (window, _) in zip(states_t, B_GROUPS):
        assert st.shape[2:] == (2 * KV_W, window)
        in_specs.append(pl.BlockSpec((None, None, 2 * KV_W, window), lambda b, f: (layer, b, 0, 0)))
        out_specs.append(pl.BlockSpec((None, None, 2 * KV_W, window), out_map))
        out_shapes.append(jax.ShapeDtypeStruct(st.shape, F32))
    in_specs += [pl.BlockSpec((1, 1, 2 * KV_W), lambda b, f: (b, 0, 0))] * ng
    in_specs += [pl.BlockSpec((2 * KV_W, n_seq), lambda b, f: (0, 0))] * ng
    aliases = {}
    if prev_new_states:
        first = len(in_specs)
        in_specs += [pl.BlockSpec(memory_space=pl.ANY)] * ng
        aliases = {first + gi: 1 + gi for gi in range(ng)}
    outs = pl.pallas_call(
        functools.partial(_attn_b_sample_body, n_seq=n_seq, n_alias=len(aliases), layer=layer, n_fill=n_fill),
        grid=(n_seq, n_fill),
        in_specs=in_specs,
        out_specs=out_specs,
        out_shape=out_shapes,
        input_output_aliases=aliases,
        compiler_params=_cp(2),
        name="attn_b_sample",
    )(qblk3, *states_t, *kvns, *kvcs, *prev_new_states)
    return outs[0], outs[1:]


def _memattn_sample_body(q_ref, mem_ref, o_ref):
    q = q_ref[0]
    mem = mem_ref[...].astype(BF)
    s = _dot(q, mem[:KV_W])
    e = jnp.exp(s - jnp.max(s, axis=1, keepdims=True))
    p = e / jnp.sum(e, axis=1, keepdims=True)
    o_ref[0] = _dot_nt(p.astype(BF), mem[KV_W:])


def _memattn_sample(mq3, mem_t, layer):
    n_seq, mem_len = mem_t.shape[1], mem_t.shape[3]
    return pl.pallas_call(
        _memattn_sample_body,
        grid=(n_seq,),
        in_specs=[pl.BlockSpec((1, _SAMPLE_ROWS, KV_W), lambda b: (b, 0, 0)),
                  pl.BlockSpec((None, None, 2 * KV_W, mem_len), lambda b: (layer, b, 0, 0))],
        out_specs=pl.BlockSpec((1, _SAMPLE_ROWS, KV_W), lambda b: (b, 0, 0)),
        out_shape=jax.ShapeDtypeStruct((n_seq, _SAMPLE_ROWS, KV_W), F32),
        compiler_params=_cp(1),
        name="memattn_sample",
    )(mq3, mem_t)


def _feature_major(x):
    nd = x.ndim
    xt = jnp.transpose(x, tuple(range(nd - 4)) + (nd - 3, nd - 2, nd - 1, nd - 4))
    return xt.reshape(x.shape[:nd - 4] + (2 * KV_W, x.shape[nd - 4]))


def _token_major(x, heads):
    nd = x.ndim
    xr = x.reshape(x.shape[:nd - 2] + (2, heads, HEAD_DIM, x.shape[nd - 1]))
    return jnp.transpose(xr, tuple(range(nd - 2)) + (nd + 1, nd - 2, nd - 1, nd))


def kernel(x_prompt, x_sample, cache_kv_a, cache_kidx_a, state_kv_b1, state_kv_b2, state_kv_b3, cache_mem_kv, page_table, mem_prompt, ffn_norm, w_ffn_in, w_ffn_out, mix_norm, mem_norm, w_mem_kv, w_in_a, w_out_a, w_in_b, w_out_b, final_norm):
    batch, seq, d = x_prompt.shape
    n_seq = x_sample.shape[0]
    depth = ffn_norm.shape[0]
    n_pages = page_table.shape[1]
    past = n_pages * PAGE_SIZE
    mem_len = mem_prompt.shape[1]
    assert x_sample.shape[1] == 1 and seq % TM == 0
    ng = len(B_GROUPS)

    yp = x_prompt.reshape(batch * seq, d)
    ys = x_sample.reshape(n_seq, d)
    tab_p = _rope_tables(jnp.arange(seq))
    tab_s = _rope_tables(jnp.full((n_seq,), past, I32))
    w_in_bf = w_ffn_in.astype(BF)
    w_out_bf = w_ffn_out.astype(BF)
    mem_flat = mem_prompt.reshape(batch * mem_len, d)
    kv_pool_t = _feature_major(cache_kv_a)
    kidx_pool_t = jnp.swapaxes(cache_kidx_a, 2, 3)
    states_t = [_feature_major(s) for s in (state_kv_b1, state_kv_b2, state_kv_b3)]
    mem_cache_t = _feature_major(cache_mem_kv)

    kv_a_p, kidx_a_p, kv_a_s, kidx_a_s = [], [], [], []
    b_p = ([], [], [])
    new_states = []
    mem_t = _memproj(mem_flat, mem_norm, w_mem_kv, batch)
    for i in range(depth):
        j = i // 2
        yp = _ffn(yp, ffn_norm[i, 0], w_in_bf, w_out_bf, i, 0, TM)
        ys = _ffn(ys, ffn_norm[i, 0], w_in_bf, w_out_bf, i, 0, n_seq)
        if i % 2 == 0:
            w = w_in_a[j]
            q_al, k_bf, iq_al, ik128, mq_al, kv_t, v_t, ik_t, iw_t = _proj_a_prompt(yp, mix_norm[i], w, tab_p, batch)
            o = _attn_a_prompt(q_al, iq_al, iw_t, k_bf, ik128, v_t, batch, seq)
            om = _memattn_prompt(mq_al, mem_t, i, seq, TM)
            n_o = A_HEADS * HEAD_DIM
            yp = _outproj(yp, [], [o, om], [w_out_a[j][:n_o], w_out_a[j][n_o:]], TM)
            kv_a_p.append(kv_t)
            kidx_a_p.append(ik_t)

            qblk, kvn, iq_s, ikn, iw_s, mq_blk = _proj_a_sample(ys, mix_norm[i], w, tab_s)
            bias = _idx_sample(page_table, iq_s.reshape(n_seq, IDX_HEADS, IDX_DIM),
                               iw_s[:, :IDX_HEADS].reshape(n_seq, IDX_HEADS, 1),
                               ikn.reshape(n_seq, 1, IDX_DIM), kidx_pool_t, j)
            o_s = _attn_a_sample(page_table, qblk.reshape(n_seq, _SAMPLE_A_ROWS, KV_W),
                                 bias.reshape(n_seq, 1, past + LANES), kvn.reshape(n_seq, 1, 2 * KV_W),
                                 kv_pool_t, j)
            om_s = _memattn_sample(mq_blk.reshape(n_seq, _SAMPLE_ROWS, KV_W), mem_cache_t, i)
            ys = _outproj(ys, [], [o_s.reshape(n_seq, -1), om_s.reshape(n_seq, -1)],
                          [_expand_out(w_out_a[j][:n_o], _A_Q_SLOT, _SAMPLE_A_ROWS),
                           _expand_out(w_out_a[j][n_o:], _MEM_SLOT, _SAMPLE_ROWS)], n_seq)
            kv_a_s.append(kvn.reshape(n_seq, 1, 2, A_KV_HEADS, HEAD_DIM))
            kidx_a_s.append(ikn.reshape(n_seq, 1, IDX_DIM))
        else:
            w = w_in_b[j]
            outs = _proj_b_prompt(yp, mix_norm[i], w, tab_p, batch)
            stats_o, stats_m, stats_d = [], [], []
            for gi, (window, dil) in enumerate(B_GROUPS):
                og, mg, dg = _attn_b_prompt(outs[2 * gi], outs[2 * gi + 1], batch, seq, window, dil)
                stats_o.append(og)
                stats_m.append(mg)
                stats_d.append(dg)
                b_p[gi].append(outs[2 * ng + 1 + gi][:, :, seq - min(window, seq):])
            om = _memattn_prompt(outs[2 * ng], mem_t, i, seq, TM)
            n_o = B_HEADS * HEAD_DIM
            yp = _outproj(yp, stats_o + stats_m + stats_d, [om], [w_out_b[j][:n_o], w_out_b[j][n_o:]], TM)

            souts = _proj_b_sample(ys, mix_norm[i], w, tab_s)
            kvns = [souts[1 + gi].reshape(n_seq, 1, 2 * KV_W) for gi in range(ng)]
            kvcs = [souts[2 + ng + gi] for gi in range(ng)]
            o_s, new_states = _attn_b_sample(souts[0].reshape(n_seq, ng * _SAMPLE_ROWS, KV_W),
                                             states_t, kvns, kvcs, j, new_states)
            om_s = _memattn_sample(souts[1 + ng].reshape(n_seq, _SAMPLE_ROWS, KV_W), mem_cache_t, i)
            ys = _outproj(ys, [], [o_s.reshape(n_seq, -1), om_s.reshape(n_seq, -1)],
                          [_expand_out(w_out_b[j][:n_o], _B_SLOT, _SAMPLE_ROWS),
                           _expand_out(w_out_b[j][n_o:], _MEM_SLOT, _SAMPLE_ROWS)], n_seq)
        yp = _ffn(yp, ffn_norm[i, 1], w_in_bf, w_out_bf, i, 1, TM)
        ys = _ffn(ys, ffn_norm[i, 1], w_in_bf, w_out_bf, i, 1, n_seq)

    y_prompt = _final_norm(yp, final_norm, TM).reshape(batch, seq, d)
    y_sample = _final_norm(ys, final_norm, n_seq).reshape(n_seq, 1, d)
    return (y_prompt, y_sample,
            _token_major(jnp.stack(kv_a_p), A_KV_HEADS), jnp.swapaxes(jnp.stack(kidx_a_p), 2, 3),
            _token_major(jnp.stack(b_p[0]), B_HEADS), _token_major(jnp.stack(b_p[1]), B_HEADS),
            _token_major(jnp.stack(b_p[2]), B_HEADS), _token_major(mem_t, MEM_HEADS),
            jnp.stack(kv_a_s), jnp.stack(kidx_a_s),
            _token_major(new_states[0], B_HEADS), _token_major(new_states[1], B_HEADS),
            _token_major(new_states[2], B_HEADS))
```

```python
import functools

import jax
import jax.numpy as jnp
import numpy as np
from jax import lax
from jax.experimental import pallas as pl
from jax.experimental.pallas import tpu as pltpu

F32 = jnp.float32
BF = jnp.bfloat16
I32 = jnp.int32

HEAD_DIM = 64
ROT_DIM = HEAD_DIM // 4
ROT_HALF = ROT_DIM // 2
ROPE_THETA = 500000.0
ATTN_SCALE = HEAD_DIM ** -0.5
A_HEADS = 12
A_KV_HEADS = 4
A_GROUP = A_HEADS // A_KV_HEADS
IDX_HEADS = 8
IDX_DIM = 64
IDX_SCALE = (IDX_HEADS * IDX_DIM) ** -0.5
TOPK_MAX = 256
B_GROUPS = ((128, 1), (512, 4), (2048, 16))
B_HEADS = 4
MEM_HEADS = 4
EPS = 1e-6
PAGE_SIZE = 128

LANES = 128
SUBLANES = 8
PAIR = 2 * HEAD_DIM
KV_W = A_KV_HEADS * HEAD_DIM
VMEM_LIMIT_V7X = 56 * 2 ** 20
NEG_BIG = -1e30
M_INIT = -3e38
INT_MAX = 2 ** 31 - 1
KEY_NEG_INF = -2139095041
MAX_SEARCH_PASSES = 32

TM = 512
TQ = 1024
KC = 512
CLS = 128
PAGES_PER_STEP = 16
ONES_ROWS = 16


def _cp(n_axes):
    return pltpu.CompilerParams(dimension_semantics=("arbitrary",) * n_axes,
                                vmem_limit_bytes=VMEM_LIMIT_V7X)


def _dot(a, b):
    return jnp.dot(a, b, preferred_element_type=F32)


def _dot_nt(a, b):
    return lax.dot_general(a, b, (((1,), (1,)), ((), ())), preferred_element_type=F32)


def _rms(x, g):
    return x * lax.rsqrt(jnp.mean(x * x, axis=-1, keepdims=True) + EPS) * g


def _sort_key(x):
    b = pltpu.bitcast(x + 0.0, I32)
    return b ^ ((b >> 31) & 0x7FFFFFFF)


def _resident(shape):
    return pl.BlockSpec(shape, lambda *_: (0,) * len(shape), pipeline_mode=pl.Buffered(1))


def _ffn_body(y_ref, g_ref, win_ref, wout_ref, o_ref, *, d_ff, fc):
    x = y_ref[...]
    xn = _rms(x, g_ref[...]).astype(BF)
    acc = jnp.zeros(x.shape, F32)
    for c in range(d_ff // fc):
        a = _dot(xn, win_ref[:, c * fc:(c + 1) * fc])
        u = _dot(xn, win_ref[:, d_ff + c * fc:d_ff + (c + 1) * fc])
        h = (a * jax.nn.sigmoid(a) * u).astype(BF)
        acc = acc + _dot(h, wout_ref[c * fc:(c + 1) * fc, :])
    o_ref[...] = x + 0.5 * acc


def _ffn(y, g, w_in_all, w_out_all, layer, half, tm):
    m, d = y.shape
    d_ff = w_out_all.shape[2]
    fc = 256 if d_ff % 256 == 0 else LANES
    assert d_ff % fc == 0 and m % tm == 0
    return pl.pallas_call(
        functools.partial(_ffn_body, d_ff=d_ff, fc=fc),
        grid=(m // tm,),
        in_specs=[pl.BlockSpec((tm, d), lambda i: (i, 0)),
                  pl.BlockSpec((1, d), lambda i: (0, 0)),
                  pl.BlockSpec((None, None, d, 2 * d_ff), lambda i: (layer, half, 0, 0),
                               pipeline_mode=pl.Buffered(1)),
                  pl.BlockSpec((None, None, d_ff, d), lambda i: (layer, half, 0, 0),
                               pipeline_mode=pl.Buffered(1))],
        out_specs=pl.BlockSpec((tm, d), lambda i: (i, 0)),
        out_shape=jax.ShapeDtypeStruct((m, d), F32),
        compiler_params=_cp(1),
        name="ffn",
    )(y, g.reshape(1, d), w_in_all, w_out_all)


def _rope_tables(pos):
    inv = jnp.power(ROPE_THETA, -jnp.arange(ROT_HALF, dtype=F32) / ROT_HALF)
    ang = pos.astype(F32)[:, None] * inv[None, :]
    cos, sin = jnp.cos(ang), jnp.sin(ang)
    t = pos.shape[0]
    ones = jnp.ones((t, HEAD_DIM - ROT_DIM), F32)
    zeros = jnp.zeros((t, HEAD_DIM - ROT_DIM), F32)
    zh = jnp.zeros((t, ROT_HALF), F32)
    cos64 = jnp.concatenate([cos, cos, ones], axis=1)
    sa64 = jnp.concatenate([-sin, zh, zeros], axis=1)
    sb64 = jnp.concatenate([zh, sin, zeros], axis=1)
    row = tuple(jnp.tile(a, (1, LANES // HEAD_DIM)) for a in (cos64, sa64, sb64))
    return row + (cos.T, sin.T)


def _proj_body(*refs, segs, nts, n_in):
    y_ref, g_ref, cos_ref, sa_ref, sb_ref, cost_ref, sint_ref = refs[:7]
    w_refs = refs[7:7 + len(segs)]
    wt_refs = refs[7 + len(segs):n_in]
    outs = refs[n_in:]
    xn = _rms(y_ref[...], g_ref[...]).astype(BF)
    for w_ref, (lo, hi, rope, scale, writes) in zip(w_refs, segs):
        p = _dot(xn, w_ref[...])
        if rope:
            cos, sa, sb = cos_ref[...], sa_ref[...], sb_ref[...]
            chunks = []
            for c in range((hi - lo) // LANES):
                pc = p[:, c * LANES:(c + 1) * LANES]
                chunks.append(pc * cos + pltpu.roll(pc, LANES - ROT_HALF, 1) * sa
                              + pltpu.roll(pc, ROT_HALF, 1) * sb)
            p = chunks[0] if len(chunks) == 1 else jnp.concatenate(chunks, axis=1)
        if scale != 1.0:
            p = p * scale
        for oi, slo, shi, dlo in writes:
            if len(outs[oi].shape) == 3:
                for c in range((shi - slo) // LANES):
                    outs[oi][dlo // LANES + c] = p[:, slo + c * LANES:slo + (c + 1) * LANES].astype(outs[oi].dtype)
            else:
                outs[oi][:, dlo:dlo + (shi - slo)] = p[:, slo:shi].astype(outs[oi].dtype)
    for t, (rope_rows, scale, writes) in enumerate(nts):
        r = _dot_nt(wt_refs[t][...], xn)
        if rope_rows:
            ct, st = cost_ref[...], sint_ref[...]
            pieces = []
            for h0 in range(0, rope_rows, HEAD_DIM):
                x1, x2 = r[h0:h0 + ROT_HALF], r[h0 + ROT_HALF:h0 + ROT_DIM]
                pieces += [x1 * ct - x2 * st, x2 * ct + x1 * st, r[h0 + ROT_DIM:h0 + HEAD_DIM]]
            if rope_rows < r.shape[0]:
                pieces.append(r[rope_rows:])
            r = jnp.concatenate(pieces, axis=0)
        if scale != 1.0:
            r = r * scale
        for oi, slo, shi, dlo in writes:
            outs[oi][dlo:dlo + (shi - slo), :] = r[slo:shi].astype(outs[oi].dtype)


def _proj(y, g, tables, n_seqs, w, wts, segs, nts, out_defs, tm):
    m, d = y.shape
    assert m % tm == 0
    per_seq = m // n_seqs // tm if tables[0].shape[0] != tm else 1
    n_in = 7 + len(w) + len(wts)
    pos_row = lambda i: (i % per_seq, 0)
    pos_col = lambda i: (0, i % per_seq)
    in_specs = [pl.BlockSpec((tm, d), lambda i: (i, 0)),
                pl.BlockSpec((1, d), lambda i: (0, 0)),
                pl.BlockSpec((tm, LANES), pos_row),
                pl.BlockSpec((tm, LANES), pos_row),
                pl.BlockSpec((tm, LANES), pos_row),
                pl.BlockSpec((ROT_HALF, tm), pos_col),
                pl.BlockSpec((ROT_HALF, tm), pos_col)]
    in_specs += [_resident(a.shape) for a in w] + [_resident(wt.shape) for wt in wts]
    out_specs, out_shapes = [], []
    for kind, size, dtype in out_defs:
        if kind == "row":
            out_specs.append(pl.BlockSpec((tm, size), lambda i: (i, 0)))
            out_shapes.append(jax.ShapeDtypeStruct((m, size), dtype))
        elif kind == "chunks":
            out_specs.append(pl.BlockSpec((size // LANES, tm, LANES), lambda i: (0, i, 0)))
            out_shapes.append(jax.ShapeDtypeStruct((size // LANES, m, LANES), dtype))
        elif kind == "col":
            out_specs.append(pl.BlockSpec((size, tm), lambda i: (0, i)))
            out_shapes.append(jax.ShapeDtypeStruct((size, m), dtype))
        else:
            out_specs.append(pl.BlockSpec((None, size, tm), lambda i: (i // per_seq, 0, i % per_seq)))
            out_shapes.append(jax.ShapeDtypeStruct((n_seqs, size, m // n_seqs), dtype))
    return pl.pallas_call(
        functools.partial(_proj_body, segs=tuple(segs), nts=tuple(nts), n_in=n_in),
        grid=(m // tm,),
        in_specs=in_specs,
        out_specs=out_specs,
        out_shape=out_shapes,
        compiler_params=_cp(1),
        name="proj",
    )(y, g.reshape(1, d), *tables, *w, *wts)


def _heads(w, lo, n_heads):
    return w[:, lo:lo + n_heads * HEAD_DIM].reshape(w.shape[0], n_heads, HEAD_DIM)


def _aligned(wh, halves):
    d, h, _ = wh.shape
    z = jnp.zeros_like(wh)
    sel = jnp.asarray(halves, I32).reshape(1, h, 1)
    lo = jnp.where(sel == 0, wh, z)
    hi = jnp.where(sel == 1, wh, z)
    return jnp.concatenate([lo, hi], axis=2).reshape(d, h * PAIR)


def _blocked(wh, slots, n_rows):
    d, h, _ = wh.shape
    onehot = jnp.asarray(np.eye(A_KV_HEADS, dtype=np.float32)[np.asarray(slots)], wh.dtype)
    blk = (wh[:, :, None, :] * onehot[None, :, :, None]).reshape(d, h * KV_W)
    return jnp.pad(blk, ((0, 0), (0, (n_rows - h) * KV_W)))


def _expand_out(w_rows, slots, n_rows):
    h = len(slots)
    d = w_rows.shape[1]
    wh = w_rows.reshape(h, HEAD_DIM, d)
    onehot = jnp.asarray(np.eye(A_KV_HEADS, dtype=np.float32)[np.asarray(slots)], w_rows.dtype)
    blk = (wh[:, None, :, :] * onehot[:, :, None, None]).reshape(h * KV_W, d)
    return jnp.pad(blk, ((0, (n_rows - h) * KV_W), (0, 0)))


_A_OFF = np.cumsum([0, A_HEADS * HEAD_DIM, KV_W, KV_W, IDX_HEADS * IDX_DIM, IDX_DIM, IDX_HEADS]).tolist()
_A_Q_HALF = [(h // A_GROUP) % 2 for h in range(A_HEADS)]
_A_Q_SLOT = [h // A_GROUP for h in range(A_HEADS)]
_MEM_HALF = [h % 2 for h in range(MEM_HEADS)]
_MEM_SLOT = list(range(MEM_HEADS))
_B_HALF = [h % 2 for h in range(B_HEADS)]
_B_SLOT = list(range(B_HEADS))
_SAMPLE_A_ROWS = 16
_SAMPLE_ROWS = 8


def _segments(items):
    segs, cols, lo = [], [], 0
    for wc, rope, scale, writes in items:
        hi = lo + wc.shape[1]
        segs.append((lo, hi, rope, scale, tuple(writes)))
        cols.append(wc.astype(BF))
        lo = hi
    return cols, segs


def _proj_a_prompt(y, g, w, tables, batch):
    o = _A_OFF
    zpad = jnp.zeros((w.shape[0], LANES - IDX_DIM), w.dtype)
    items = [
        (_aligned(_heads(w, o[0], A_HEADS), _A_Q_HALF), True, ATTN_SCALE, [(0, 0, A_HEADS * PAIR, 0)]),
        (w[:, o[1]:o[2]], True, 1.0, [(1, 0, KV_W, 0)]),
        (_aligned(_heads(w, o[3], IDX_HEADS), [0] * IDX_HEADS), True, 1.0, [(2, 0, IDX_HEADS * PAIR, 0)]),
        (jnp.concatenate([w[:, o[4]:o[5]], zpad], axis=1), True, 1.0, [(3, 0, LANES, 0)]),
        (_aligned(_heads(w, o[6], MEM_HEADS), _MEM_HALF), False, ATTN_SCALE, [(4, 0, MEM_HEADS * PAIR, 0)]),
    ]
    wcat, segs = _segments(items)
    wts = [w[:, o[1]:o[3]].T.astype(BF), w[:, o[4]:o[5]].T.astype(BF), w[:, o[5]:o[6]].T.astype(BF)]
    nts = [(KV_W, 1.0, ((5, 0, 2 * KV_W, 0), (6, KV_W, 2 * KV_W, 0))),
           (IDX_DIM, 1.0, ((7, 0, IDX_DIM, 0),)),
           (0, IDX_SCALE, ((8, 0, IDX_HEADS, 0),))]
    out_defs = [("row", A_HEADS * PAIR, BF), ("row", KV_W, BF), ("row", IDX_HEADS * PAIR, BF),
                ("row", LANES, BF), ("row", MEM_HEADS * PAIR, BF), ("bcol", 2 * KV_W, F32),
                ("col", KV_W, BF), ("bcol", IDX_DIM, F32), ("col", IDX_HEADS, F32)]
    return _proj(y, g, tables, batch, wcat, wts, segs, nts, out_defs, TM)


def _proj_a_sample(y, g, w, tables):
    o = _A_OFF
    zpad = jnp.zeros((w.shape[0], LANES - IDX_DIM), w.dtype)
    wpad = jnp.zeros((w.shape[0], LANES - IDX_HEADS), w.dtype)
    items = [
        (_blocked(_heads(w, o[0], A_HEADS), _A_Q_SLOT, _SAMPLE_A_ROWS), True, ATTN_SCALE,
         [(0, 0, _SAMPLE_A_ROWS * KV_W, 0)]),
        (w[:, o[1]:o[2]], True, 1.0, [(1, 0, KV_W, 0)]),
        (w[:, o[2]:o[3]], False, 1.0, [(1, 0, KV_W, KV_W)]),
        (w[:, o[3]:o[4]], True, 1.0, [(2, 0, IDX_HEADS * IDX_DIM, 0)]),
        (jnp.concatenate([w[:, o[4]:o[5]], zpad], axis=1), True, 1.0, [(3, 0, IDX_DIM, 0)]),
        (jnp.concatenate([w[:, o[5]:o[6]], wpad], axis=1), False, IDX_SCALE, [(4, 0, LANES, 0)]),
        (_blocked(_heads(w, o[6], MEM_HEADS), _MEM_SLOT, _SAMPLE_ROWS), False, ATTN_SCALE,
         [(5, 0, _SAMPLE_ROWS * KV_W, 0)]),
    ]
    wcat, segs = _segments(items)
    out_defs = [("row", _SAMPLE_A_ROWS * KV_W, BF), ("row", 2 * KV_W, F32), ("row", IDX_HEADS * IDX_DIM, BF),
                ("row", IDX_DIM, F32), ("row", LANES, F32), ("row", _SAMPLE_ROWS * KV_W, BF)]
    return _proj(y, g, tables, 1, wcat, [], segs, [], out_defs, y.shape[0])


def _b_off(g, t):
    return (g * 3 + t) * B_HEADS * HEAD_DIM


def _proj_b_prompt(y, g, w, tables, batch):
    ng = len(B_GROUPS)
    items, out_defs, wts, nts = [], [], [], []
    for gi in range(ng):
        items += [
            (_aligned(_heads(w, _b_off(gi, 0), B_HEADS), _B_HALF), True, ATTN_SCALE,
             [(2 * gi, 0, B_HEADS * PAIR, 0)]),
            (w[:, _b_off(gi, 1):_b_off(gi, 1) + KV_W], True, 1.0, [(2 * gi + 1, 0, KV_W, 0)]),
            (w[:, _b_off(gi, 2):_b_off(gi, 2) + KV_W], False, 1.0, [(2 * gi + 1, 0, KV_W, KV_W)]),
        ]
        out_defs += [("chunks", B_HEADS * PAIR, F32), ("chunks", 2 * KV_W, F32)]
    items.append((_aligned(_heads(w, _b_off(ng, 0), MEM_HEADS), _MEM_HALF), False, ATTN_SCALE,
                  [(2 * ng, 0, MEM_HEADS * PAIR, 0)]))
    out_defs.append(("row", MEM_HEADS * PAIR, BF))
    for gi in range(ng):
        wts.append(w[:, _b_off(gi, 1):_b_off(gi, 1) + 2 * KV_W].T.astype(BF))
        nts.append((KV_W, 1.0, ((2 * ng + 1 + gi, 0, 2 * KV_W, 0),)))
        out_defs.append(("bcol", 2 * KV_W, F32))
    wcat, segs = _segments(items)
    return _proj(y, g, tables, batch, wcat, wts, segs, nts, out_defs, TM)


def _proj_b_sample(y, g, w, tables):
    ng = len(B_GROUPS)
    items, out_defs, wts, nts = [], [("row", ng * _SAMPLE_ROWS * KV_W, BF)], [], []
    for gi in range(ng):
        items += [
            (_blocked(_heads(w, _b_off(gi, 0), B_HEADS), _B_SLOT, _SAMPLE_ROWS), True, ATTN_SCALE,
             [(0, 0, _SAMPLE_ROWS * KV_W, gi * _SAMPLE_ROWS * KV_W)]),
            (w[:, _b_off(gi, 1):_b_off(gi, 1) + KV_W], True, 1.0, [(1 + gi, 0, KV_W, 0)]),
            (w[:, _b_off(gi, 2):_b_off(gi, 2) + KV_W], False, 1.0, [(1 + gi, 0, KV_W, KV_W)]),
        ]
        out_defs.append(("row", 2 * KV_W, F32))
    items.append((_blocked(_heads(w, _b_off(ng, 0), MEM_HEADS), _MEM_SLOT, _SAMPLE_ROWS), False, ATTN_SCALE,
                  [(1 + ng, 0, _SAMPLE_ROWS * KV_W, 0)]))
    out_defs.append(("row", _SAMPLE_ROWS * KV_W, BF))
    for gi in range(ng):
        wts.append(w[:, _b_off(gi, 1):_b_off(gi, 1) + 2 * KV_W].T.astype(BF))
        nts.append((KV_W, 1.0, ((2 + ng + gi, 0, 2 * KV_W, 0),)))
        out_defs.append(("col", 2 * KV_W, F32))
    wcat, segs = _segments(items)
    return _proj(y, g, tables, 1, wcat, wts, segs, nts, out_defs, y.shape[0])


def _memproj_body(y_ref, g_ref, wt_ref, o_ref):
    xn = _rms(y_ref[...], g_ref[...]).astype(BF)
    o_ref[...] = _dot_nt(wt_ref[...], xn)


def _memproj(mem_flat, g_all, w_all, batch):
    m, d = mem_flat.shape
    mem_len = m // batch
    depth, _, n = w_all.shape
    return pl.pallas_call(
        _memproj_body,
        grid=(depth, batch),
        in_specs=[pl.BlockSpec((mem_len, d), lambda l, b: (b, 0)),
                  pl.BlockSpec((None, 1, d), lambda l, b: (l, 0, 0)),
                  pl.BlockSpec((None, n, d), lambda l, b: (l, 0, 0))],
        out_specs=pl.BlockSpec((None, None, n, mem_len), lambda l, b: (l, b, 0, 0)),
        out_shape=jax.ShapeDtypeStruct((depth, batch, n, mem_len), F32),
        compiler_params=_cp(2),
        name="memproj",
    )(mem_flat, g_all.reshape(depth, 1, d), jnp.swapaxes(w_all, 1, 2).astype(BF))


def _select_bias(count, read, write, n_chunks, rows, k_sel, red_shape, key_axis):
    lo0 = KEY_NEG_INF + 1
    c_valid = count(lo0)
    c_zero = count(0)
    pos = c_zero >= k_sel
    few = c_valid <= k_sel
    lo = jnp.where(pos, 0, lo0).astype(I32)
    hi = jnp.where(pos, INT_MAX, 0).astype(I32)
    c_lo = jnp.where(pos, c_zero, c_valid)
    c_hi = jnp.where(pos, 0, c_zero)
    done = jnp.where(jnp.logical_or(few, c_zero == k_sel), 1, 0).astype(I32)
    thr = jnp.where(few, lo0, 0).astype(I32)
    full = jnp.full(red_shape, k_sel, I32)

    def cond(st):
        return jnp.logical_and(st[0] < MAX_SEARCH_PASSES, st[1] == 0)

    def body(st):
        it, _, lo, hi, c_lo, c_hi, thr, need, n_ge, done = st
        cand = lo + jnp.maximum((hi - lo) >> 1, 1)
        c = count(cand)
        hit = c == k_sel
        up = c > k_sel
        lo_n = jnp.where(up, cand, lo)
        hi_n = jnp.where(up, hi, cand)
        c_lo_n = jnp.where(up, c, c_lo)
        c_hi_n = jnp.where(up, c_hi, c)
        newly = jnp.logical_and(done == 0, jnp.logical_or(hit, hi_n - lo_n == 1))
        thr_n = jnp.where(newly, jnp.where(hit, cand, lo_n), thr)
        need_n = jnp.where(jnp.logical_and(newly, jnp.logical_not(hit)), k_sel - c_hi_n, need)
        n_ge_n = jnp.where(jnp.logical_and(newly, jnp.logical_not(hit)), c_lo_n, n_ge)
        done_n = jnp.where(newly, 1, done)
        return (it + 1, jnp.min(done_n), lo_n, hi_n, c_lo_n, c_hi_n, thr_n, need_n, n_ge_n, done_n)

    st = lax.while_loop(cond, body, (jnp.int32(0), jnp.min(done), lo, hi, c_lo, c_hi, thr, full, full, done))
    thr, need_i, n_ge = st[6], st[7], st[8]
    has_tie = jnp.max(n_ge) > k_sel

    @pl.when(jnp.logical_not(has_tie))
    def _():
        def wbody(c, _):
            x = read(c)
            sel = jnp.logical_and(x >= thr, x > KEY_NEG_INF)
            write(c, jnp.where(sel, 0.0, NEG_BIG).astype(F32))
            return 0
        lax.fori_loop(0, n_chunks, wbody, 0)

    @pl.when(has_tie)
    def _():
        need = need_i.astype(F32)
        r = lax.broadcasted_iota(I32, (rows, rows), 0)
        cidx = lax.broadcasted_iota(I32, (rows, rows), 1)
        if key_axis == 0:
            tri = jnp.where(cidx < r, 1.0, 0.0).astype(BF)
        else:
            tri = jnp.where(r < cidx, 1.0, 0.0).astype(BF)

        def wbody(c, seen):
            x = read(c)
            eq = jnp.logical_and(x == thr, x > KEY_NEG_INF)
            eqf = jnp.where(eq, 1.0, 0.0).astype(BF)
            rank = (_dot(tri, eqf) if key_axis == 0 else _dot(eqf, tri)) + seen
            sel = jnp.logical_and(jnp.logical_or(x > thr, jnp.logical_and(eq, rank < need)), x > KEY_NEG_INF)
            write(c, jnp.where(sel, 0.0, NEG_BIG).astype(F32))
            return seen + jnp.sum(eqf.astype(F32), axis=key_axis, keepdims=True)
        lax.fori_loop(0, n_chunks, wbody, jnp.zeros(red_shape, F32))


def _attn_a_body(q_ref, iq_ref, iw_ref, k_ref, ik_ref, vt_ref, o_ref, sc_ref, acc_ref, *, tq, kc, k_sel):
    qi = pl.program_id(1)
    per_q = tq // kc
    n_full = qi * per_q
    n_chunks = n_full + per_q
    iw = iw_ref[...]

    def chunk_rows(c):
        return pl.ds(pl.multiple_of(c * kc, kc), kc)

    def idx_chunk(c, j):
        l0 = 0 if j is None else j * kc
        ikc = ik_ref[chunk_rows(c), :]
        acc = jnp.zeros((kc, tq - l0), F32)
        for h in range(IDX_HEADS):
            s = _dot_nt(ikc, iq_ref[l0:, h * PAIR:(h + 1) * PAIR])
            acc = acc + jnp.maximum(s, 0.0) * iw[h:h + 1, l0:]
        if j is not None:
            kpos = lax.broadcasted_iota(I32, (kc, tq - l0), 0)
            qpos = lax.broadcasted_iota(I32, (kc, tq - l0), 1)
            acc = jnp.where(kpos <= qpos, acc, -jnp.inf)
            if l0:
                sc_ref[chunk_rows(c), :l0] = jnp.full((kc, l0), KEY_NEG_INF, I32)
        sc_ref[chunk_rows(c), l0:] = _sort_key(acc)

    def full_chunk(c, _):
        idx_chunk(c, None)
        return 0
    lax.fori_loop(0, n_full, full_chunk, 0)
    for j in range(per_q):
        idx_chunk(n_full + j, j)

    def read(c):
        return sc_ref[chunk_rows(c), :]

    def write(c, v):
        sc_ref[chunk_rows(c), :] = pltpu.bitcast(v, I32)

    def count(t):
        def step(x, cnts, l0):
            tt = t if isinstance(t, int) else t[:, l0:]
            cnts = list(cnts)
            for r in range(kc // SUBLANES):
                a = r % len(cnts)
                cnts[a] = jnp.where(x[r * SUBLANES:(r + 1) * SUBLANES] >= tt, cnts[a] + 1, cnts[a])
            return tuple(cnts)

        def zeros(l0):
            return (jnp.zeros((SUBLANES, tq - l0), I32),) * 4
        parts = lax.fori_loop(0, n_full + 1, lambda c, cn: step(read(c), cn, 0), zeros(0))
        total = parts[0] + parts[1] + parts[2] + parts[3]
        for j in range(1, per_q):
            l0 = j * kc
            pj = step(sc_ref[chunk_rows(n_full + j), l0:], zeros(l0), l0)
            total = total + jnp.concatenate([jnp.zeros((SUBLANES, l0), I32), pj[0] + pj[1] + pj[2] + pj[3]], axis=1)
        return jnp.sum(total, axis=0, keepdims=True)

    _select_bias(count, read, write, n_chunks, kc, k_sel, (1, tq), 0)

    acc_ref[...] = jnp.zeros(acc_ref.shape, F32)
    ones_rows = jnp.ones((ONES_ROWS, kc), BF)
    pad_rows = jnp.zeros((2 * SUBLANES - A_HEADS, tq), F32)

    def att_chunk(c, carry, l0):
        m_all, l_all = carry
        rows_c = chunk_rows(c)
        bias = pltpu.bitcast(sc_ref[rows_c, l0:], F32)
        kcs = [k_ref[rows_c, pair * PAIR:(pair + 1) * PAIR] for pair in range(A_KV_HEADS // 2)]
        vts = [jnp.concatenate([vt_ref[n * HEAD_DIM:(n + 1) * HEAD_DIM, rows_c], ones_rows], axis=0)
               for n in range(A_KV_HEADS)]
        m_rows, l_rows = [], []
        for h in range(A_HEADS):
            n = h // A_GROUP
            s = _dot_nt(kcs[n // 2], q_ref[l0:, h * PAIR:(h + 1) * PAIR]) + bias
            m_old = m_all[h:h + 1, l0:]
            m_new = jnp.maximum(m_old, jnp.max(s, axis=0, keepdims=True))
            alpha = jnp.exp(m_old - m_new)
            p = jnp.exp(s - m_new).astype(BF)
            res = _dot(vts[n], p)
            rows = slice(h * HEAD_DIM, (h + 1) * HEAD_DIM)
            l_new = alpha * l_all[h:h + 1, l0:] + res[HEAD_DIM:HEAD_DIM + 1]
            m_rows.append(m_new if l0 == 0 else jnp.concatenate([m_all[h:h + 1, :l0], m_new], axis=1))
            l_rows.append(l_new if l0 == 0 else jnp.concatenate([l_all[h:h + 1, :l0], l_new], axis=1))
            acc_ref[rows, l0:] = alpha * acc_ref[rows, l0:] + res[:HEAD_DIM]
        return (jnp.concatenate(m_rows + [pad_rows], axis=0), jnp.concatenate(l_rows + [pad_rows], axis=0))

    carry = (jnp.full((2 * SUBLANES, tq), M_INIT, F32), jnp.zeros((2 * SUBLANES, tq), F32))
    carry = lax.fori_loop(0, n_full + 1, lambda c, cr: att_chunk(c, cr, 0), carry)
    for j in range(1, per_q):
        carry = att_chunk(n_full + j, carry, j * kc)
    l_all = carry[1]

    for hp in range(A_HEADS // 2):
        outs = [acc_ref[h * HEAD_DIM:(h + 1) * HEAD_DIM, :] / l_all[h:h + 1] for h in (2 * hp, 2 * hp + 1)]
        o_ref[:, hp * PAIR:(hp + 1) * PAIR] = jnp.concatenate(outs, axis=0).T.astype(o_ref.dtype)


def _attn_a_prompt(q_al, iq_al, iw_t, k_bf, ik128, v_t, batch, seq):
    tq = min(TQ, seq)
    kc = min(KC, tq)
    nq = seq // tq
    m = batch * seq
    k_sel = min(TOPK_MAX, seq // 4)
    return pl.pallas_call(
        functools.partial(_attn_a_body, tq=tq, kc=kc, k_sel=k_sel),
        grid=(batch, nq),
        in_specs=[pl.BlockSpec((tq, A_HEADS * PAIR), lambda b, i: (b * nq + i, 0)),
                  pl.BlockSpec((tq, IDX_HEADS * PAIR), lambda b, i: (b * nq + i, 0)),
                  pl.BlockSpec((IDX_HEADS, tq), lambda b, i: (0, b * nq + i)),
                  pl.BlockSpec((seq, KV_W), lambda b, i: (b, 0)),
                  pl.BlockSpec((seq, LANES), lambda b, i: (b, 0)),
                  pl.BlockSpec((KV_W, seq), lambda b, i: (0, b))],
        out_specs=pl.BlockSpec((tq, A_HEADS * HEAD_DIM), lambda b, i: (b * nq + i, 0)),
        out_shape=jax.ShapeDtypeStruct((m, A_HEADS * HEAD_DIM), BF),
        scratch_shapes=[pltpu.VMEM((seq, tq), I32),
                        pltpu.VMEM((A_HEADS * HEAD_DIM, tq), F32)],
        compiler_params=_cp(2),
        name="attn_a_prompt",
    )(q_al, iq_al, iw_t, k_bf, ik128, v_t)


def _attn_b_body(q_ref, kvc_ref, kvp_ref, o_ref, m_ref, d_ref, *, dil, span):
    i = pl.program_id(1)
    qrow = lax.broadcasted_iota(I32, (CLS, 2 * CLS), 0)
    kcol = lax.broadcasted_iota(I32, (CLS, 2 * CLS), 1)
    rel = qrow + CLS - kcol
    valid = jnp.logical_and(jnp.logical_and(rel >= 0, rel <= span), jnp.logical_or(kcol >= CLS, i > 0))
    n_pairs = B_HEADS // 2
    valid4 = jnp.concatenate([valid] * B_HEADS, axis=0)
    slot = lax.broadcasted_iota(I32, (CLS, KV_W), 1) >> (HEAD_DIM.bit_length() - 1)
    zeros = jnp.zeros((CLS, PAIR), BF)

    def one_class(rows):
        def cat(base):
            return jnp.concatenate(
                [jnp.concatenate([kvp_ref[base + p, rows, :], kvc_ref[base + p, rows, :]], axis=0)
                 for p in range(n_pairs)], axis=1).astype(BF)
        kcat, vcat = cat(0), cat(n_pairs)
        qrows = []
        for h in range(B_HEADS):
            qh = q_ref[h, rows, :].astype(BF)
            qrows.append(jnp.concatenate([qh, zeros] if h // 2 == 0 else [zeros, qh], axis=1))
        s = jnp.where(valid4, _dot_nt(jnp.concatenate(qrows, axis=0), kcat), -jnp.inf)
        mx = jnp.max(s, axis=1, keepdims=True)
        e = jnp.exp(s - mx)
        den = jnp.sum(e, axis=1, keepdims=True)
        o = _dot(e.astype(BF), vcat) / den
        o_nat = jnp.zeros((CLS, KV_W), F32)
        m_nat = jnp.zeros((CLS, KV_W), F32)
        d_nat = jnp.zeros((CLS, KV_W), F32)
        for h in range(B_HEADS):
            blk = slice(h * CLS, (h + 1) * CLS)
            o_nat = jnp.where(slot == h, o[blk], o_nat)
            m_nat = jnp.where(slot == h, mx[blk], m_nat)
            d_nat = jnp.where(slot == h, den[blk], d_nat)
        for p in range(n_pairs):
            o_ref[p, rows, :] = o_nat[:, p * PAIR:(p + 1) * PAIR]
            m_ref[p, rows, :] = m_nat[:, p * PAIR:(p + 1) * PAIR]
            d_ref[p, rows, :] = d_nat[:, p * PAIR:(p + 1) * PAIR]

    if dil == 1:
        one_class(pl.ds(0, CLS))
    else:
        def body(r, _):
            one_class(pl.ds(r, CLS, stride=dil))
            return 0
        lax.fori_loop(0, dil, body, 0)


def _attn_b_prompt(q_al, kv, batch, seq, window, dil):
    rows = CLS * dil
    span = window // dil
    assert span <= CLS and seq % rows == 0
    nr = seq // rows
    m = batch * seq
    n_pairs = B_HEADS // 2
    stat = jax.ShapeDtypeStruct((n_pairs, m, PAIR), F32)
    ospec = pl.BlockSpec((n_pairs, rows, PAIR), lambda b, i: (0, b * nr + i, 0))
    return pl.pallas_call(
        functools.partial(_attn_b_body, dil=dil, span=span),
        grid=(batch, nr),
        in_specs=[pl.BlockSpec((B_HEADS, rows, PAIR), lambda b, i: (0, b * nr + i, 0)),
                  pl.BlockSpec((2 * n_pairs, rows, PAIR), lambda b, i: (0, b * nr + i, 0)),
                  pl.BlockSpec((2 * n_pairs, rows, PAIR), lambda b, i: (0, b * nr + jnp.maximum(i - 1, 0), 0))],
        out_specs=[ospec, ospec, ospec],
        out_shape=[stat, stat, stat],
        compiler_params=_cp(2),
        name="attn_b_prompt",
    )(q_al, kv, kv)


def _memattn_body(mq_ref, mem_ref, o_ref):
    mq = mq_ref[...]
    mem = mem_ref[...].astype(BF)
    lo_half = lax.broadcasted_iota(I32, (mq.shape[0], PAIR), 1) < HEAD_DIM
    outs = []
    for pair in range(MEM_HEADS // 2):
        mk = mem[pair * PAIR:(pair + 1) * PAIR]
        mv = mem[KV_W + pair * PAIR:KV_W + (pair + 1) * PAIR]
        res = []
        for h in (2 * pair, 2 * pair + 1):
            s = _dot(mq[:, h * PAIR:(h + 1) * PAIR], mk)
            e = jnp.exp(s - jnp.max(s, axis=1, keepdims=True))
            p = e / jnp.sum(e, axis=1, keepdims=True)
            res.append(_dot_nt(p.astype(BF), mv))
        outs.append(jnp.where(lo_half, res[0], res[1]))
    o_ref[...] = jnp.concatenate(outs, axis=1).astype(o_ref.dtype)


def _memattn_prompt(mq_al, mem_t, layer, seq, tm):
    m = mq_al.shape[0]
    mem_len = mem_t.shape[3]
    tiles_per_seq = seq // tm
    return pl.pallas_call(
        _memattn_body,
        grid=(m // tm,),
        in_specs=[pl.BlockSpec((tm, MEM_HEADS * PAIR), lambda i: (i, 0)),
                  pl.BlockSpec((None, None, 2 * KV_W, mem_len), lambda i: (layer, i // tiles_per_seq, 0, 0))],
        out_specs=pl.BlockSpec((tm, KV_W), lambda i: (i, 0)),
        out_shape=jax.ShapeDtypeStruct((m, KV_W), BF),
        compiler_params=_cp(1),
        name="memattn_prompt",
    )(mq_al, mem_t)


def _outproj_body(*refs, n_x, combine):
    y_ref = refs[0]
    pos = 1
    xs = []
    if combine:
        og, mg, dg = refs[1:4], refs[4:7], refs[7:10]
        pos = 10
        ms = [r[...] for r in mg]
        mmax = jnp.maximum(jnp.maximum(ms[0], ms[1]), ms[2])
        wgt = [dg[k][...] * jnp.exp(ms[k] - mmax) for k in range(3)]
        num = wgt[0] * og[0][...] + wgt[1] * og[1][...] + wgt[2] * og[2][...]
        merged = num / (wgt[0] + wgt[1] + wgt[2])
        xs.append(jnp.concatenate([merged[c] for c in range(merged.shape[0])], axis=1))
    xs += [refs[pos + k][...] for k in range(n_x)]
    w_refs = refs[pos + n_x:pos + n_x + len(xs)]
    o_ref = refs[-1]
    acc = y_ref[...]
    for x, w_ref in zip(xs, w_refs):
        acc = acc + _dot(x.astype(BF), w_ref[...])
    o_ref[...] = acc


def _outproj(y, stats, xs, ws, tm):
    m, d = y.shape
    row = lambda width: pl.BlockSpec((tm, width), lambda i: (i, 0))
    chunked = lambda a: pl.BlockSpec((a.shape[0], tm, LANES), lambda i: (0, i, 0))
    in_specs = [row(d)] + [chunked(a) for a in stats] + [row(x.shape[1]) for x in xs]
    in_specs += [_resident(w.shape) for w in ws]
    return pl.pallas_call(
        functools.partial(_outproj_body, n_x=len(xs), combine=bool(stats)),
        grid=(m // tm,),
        in_specs=in_specs,
        out_specs=row(d),
        out_shape=jax.ShapeDtypeStruct((m, d), F32),
        compiler_params=_cp(1),
        name="outproj",
    )(y, *stats, *xs, *[w.astype(BF) for w in ws])


def _final_norm_body(y_ref, g_ref, o_ref):
    o_ref[...] = _rms(y_ref[...], g_ref[...])


def _final_norm(y, g, tm):
    m, d = y.shape
    return pl.pallas_call(
        _final_norm_body,
        grid=(m // tm,),
        in_specs=[pl.BlockSpec((tm, d), lambda i: (i, 0)), pl.BlockSpec((1, d), lambda i: (0, 0))],
        out_specs=pl.BlockSpec((tm, d), lambda i: (i, 0)),
        out_shape=jax.ShapeDtypeStruct((m, d), F32),
        compiler_params=_cp(1),
        name="final_norm",
    )(y, g.reshape(1, d))


def _page_copies(pool_ref, buf_ref, sem_ref, pt_ref, layer, b, first_page, n_pages, slot):
    return [pltpu.make_async_copy(pool_ref.at[layer, pt_ref[b, first_page + p]],
                                  buf_ref.at[slot, :, pl.ds(p * PAGE_SIZE, PAGE_SIZE)],
                                  sem_ref.at[slot]) for p in range(n_pages)]


def _idx_sample_body(pt_ref, iq_ref, iw_ref, ikn_ref, pool_ref, bias_ref, kbuf, sem, sc_ref,
                     *, layer, n_seq, n_pages, k_sel):
    b = pl.program_id(0)
    past = n_pages * PAGE_SIZE
    slot = b % 2

    @pl.when(b == 0)
    def _():
        sc_ref[...] = jnp.zeros(sc_ref.shape, I32)
        for cp in _page_copies(pool_ref, kbuf, sem, pt_ref, layer, 0, 0, n_pages, 0):
            cp.start()

    @pl.when(b + 1 < n_seq)
    def _():
        for cp in _page_copies(pool_ref, kbuf, sem, pt_ref, layer, b + 1, 0, n_pages, 1 - slot):
            cp.start()

    for cp in _page_copies(pool_ref, kbuf, sem, pt_ref, layer, b, 0, n_pages, slot):
        cp.wait()

    iq = iq_ref[0]
    iw = iw_ref[0]
    s = _dot(iq, kbuf[slot].astype(BF))
    row = jnp.sum(jnp.maximum(s, 0.0) * iw, axis=0, keepdims=True)
    ikn = ikn_ref[0].astype(BF).astype(F32)
    s_new = jnp.sum(iq.astype(F32) * ikn, axis=1, keepdims=True)
    r_new = jnp.sum(jnp.maximum(s_new, 0.0) * iw, axis=0, keepdims=True)
    lane = lax.broadcasted_iota(I32, (1, LANES), 1)
    tail = jnp.where(lane == 0, r_new, -jnp.inf)
    is_b = lax.broadcasted_iota(I32, (n_seq, 1), 0) == b
    sc_ref[:, :past] = jnp.where(is_b, _sort_key(row), sc_ref[:, :past])
    sc_ref[:, past:] = jnp.where(is_b, _sort_key(tail), sc_ref[:, past:])

    @pl.when(b == n_seq - 1)
    def _():
        def read(c):
            return sc_ref[:, pl.ds(pl.multiple_of(c * LANES, LANES), LANES)]

        def write(c, v):
            bias_ref[:, pl.ds(pl.multiple_of(c * LANES, LANES), LANES)] = v

        def count(t):
            return jnp.sum(jnp.where(sc_ref[...] >= t, 1, 0).astype(I32), axis=1, keepdims=True)

        _select_bias(count, read, write, past // LANES + 1, LANES, k_sel, (n_seq, 1), 1)


def _idx_sample(page_table, iq3, iw3, ikn3, pool_t, layer):
    n_seq, n_pages = page_table.shape
    past = n_pages * PAGE_SIZE
    width = past + LANES
    k_sel = min(TOPK_MAX, (past + 1) // 4)
    grid_spec = pltpu.PrefetchScalarGridSpec(
        num_scalar_prefetch=1,
        grid=(n_seq,),
        in_specs=[pl.BlockSpec((1, IDX_HEADS, IDX_DIM), lambda b, pt: (b, 0, 0)),
                  pl.BlockSpec((1, IDX_HEADS, 1), lambda b, pt: (b, 0, 0)),
                  pl.BlockSpec((1, 1, IDX_DIM), lambda b, pt: (b, 0, 0)),
                  pl.BlockSpec(memory_space=pl.ANY)],
        out_specs=pl.BlockSpec((n_seq, width), lambda b, pt: (0, 0)),
        scratch_shapes=[pltpu.VMEM((2, IDX_DIM, past), F32),
                        pltpu.SemaphoreType.DMA((2,)),
                        pltpu.VMEM((n_seq, width), I32)])
    return pl.pallas_call(
        functools.partial(_idx_sample_body, layer=layer, n_seq=n_seq, n_pages=n_pages, k_sel=k_sel),
        grid_spec=grid_spec,
        out_shape=jax.ShapeDtypeStruct((n_seq, width), F32),
        compiler_params=_cp(1),
        name="idx_sample",
    )(page_table, iq3, iw3, ikn3, pool_t)


def _attn_a_sample_body(pt_ref, q_ref, bias_ref, kvn_ref, pool_ref, o_ref, kvbuf, sem, m_s, l_s, acc_s,
                        *, layer, n_seq, n_steps, pg):
    b, s = pl.program_id(0), pl.program_id(1)
    t = b * n_steps + s
    slot = t % 2
    keys = pg * PAGE_SIZE
    past = n_steps * keys

    @pl.when(t == 0)
    def _():
        for cp in _page_copies(pool_ref, kvbuf, sem, pt_ref, layer, 0, 0, pg, 0):
            cp.start()

    @pl.when(t + 1 < n_seq * n_steps)
    def _():
        wrap = s + 1 == n_steps
        nb = jnp.where(wrap, b + 1, b)
        ns = jnp.where(wrap, 0, s + 1)
        for cp in _page_copies(pool_ref, kvbuf, sem, pt_ref, layer, nb, ns * pg, pg, 1 - slot):
            cp.start()

    for cp in _page_copies(pool_ref, kvbuf, sem, pt_ref, layer, b, s * pg, pg, slot):
        cp.wait()

    @pl.when(s == 0)
    def _():
        m_s[...] = jnp.full(m_s.shape, M_INIT, F32)
        l_s[...] = jnp.zeros(l_s.shape, F32)
        acc_s[...] = jnp.zeros(acc_s.shape, F32)

    qb = q_ref[0]
    kv = kvbuf[slot]
    bias = bias_ref[0, :, pl.ds(pl.multiple_of(s * keys, keys), keys)]
    sc = _dot(qb, kv[:KV_W].astype(BF)) + bias
    m_old = m_s[...]
    m_new = jnp.maximum(m_old, jnp.max(sc, axis=1, keepdims=True))
    alpha = jnp.exp(m_old - m_new)
    p = jnp.exp(sc - m_new)
    l_new = alpha * l_s[...] + jnp.sum(p, axis=1, keepdims=True)
    acc_new = alpha * acc_s[...] + _dot_nt(p.astype(BF), kv[KV_W:].astype(BF))
    m_s[...] = m_new
    l_s[...] = l_new
    acc_s[...] = acc_new

    @pl.when(s == n_steps - 1)
    def _():
        kn = kvn_ref[0].astype(BF).astype(F32)
        b_new = bias_ref[0, :, past:past + LANES][:, 0:1]
        s_new = jnp.sum(qb.astype(F32) * kn[:, :KV_W], axis=1, keepdims=True) + b_new
        m_fin = jnp.maximum(m_new, s_new)
        a_fin = jnp.exp(m_new - m_fin)
        p_new = jnp.exp(s_new - m_fin)
        l_fin = a_fin * l_new + p_new
        o_ref[0] = (a_fin * acc_new + p_new.astype(BF).astype(F32) * kn[:, KV_W:]) / l_fin


def _attn_a_sample(page_table, qblk, bias3, kvn3, pool_t, layer):
    n_seq, n_pages = page_table.shape
    pg = min(PAGES_PER_STEP, n_pages)
    assert n_pages % pg == 0
    n_steps = n_pages // pg
    width = bias3.shape[2]
    grid_spec = pltpu.PrefetchScalarGridSpec(
        num_scalar_prefetch=1,
        grid=(n_seq, n_steps),
        in_specs=[pl.BlockSpec((1, _SAMPLE_A_ROWS, KV_W), lambda b, s, pt: (b, 0, 0)),
                  pl.BlockSpec((1, 1, width), lambda b, s, pt: (b, 0, 0)),
                  pl.BlockSpec((1, 1, 2 * KV_W), lambda b, s, pt: (b, 0, 0)),
                  pl.BlockSpec(memory_space=pl.ANY)],
        out_specs=pl.BlockSpec((1, _SAMPLE_A_ROWS, KV_W), lambda b, s, pt: (b, 0, 0)),
        scratch_shapes=[pltpu.VMEM((2, 2 * KV_W, pg * PAGE_SIZE), F32),
                        pltpu.SemaphoreType.DMA((2,)),
                        pltpu.VMEM((_SAMPLE_A_ROWS, 1), F32),
                        pltpu.VMEM((_SAMPLE_A_ROWS, 1), F32),
                        pltpu.VMEM((_SAMPLE_A_ROWS, KV_W), F32)])
    return pl.pallas_call(
        functools.partial(_attn_a_sample_body, layer=layer, n_seq=n_seq, n_steps=n_steps, pg=pg),
        grid_spec=grid_spec,
        out_shape=jax.ShapeDtypeStruct((n_seq, _SAMPLE_A_ROWS, KV_W), F32),
        compiler_params=_cp(2),
        name="attn_a_sample",
    )(page_table, qblk, bias3, kvn3, pool_t)


def _attn_b_sample_body(*refs, n_seq, n_alias, layer, n_fill):
    ng = len(B_GROUPS)
    nss = refs[2 + 3 * ng + n_alias:]
    if n_fill == 1:
        _attn_b_sample_compute(refs, n_alias)
        return
    fill = pl.program_id(1)

    @pl.when(fill == layer)
    def _():
        _attn_b_sample_compute(refs, n_alias)

    @pl.when(fill != layer)
    def _():
        for ns_ref in nss:
            ns_ref[...] = jnp.zeros(ns_ref.shape, F32)


def _attn_b_sample_compute(refs, n_alias):
    ng = len(B_GROUPS)
    q_ref = refs[0]
    sts, kns, kcs = refs[1:1 + ng], refs[1 + ng:1 + 2 * ng], refs[1 + 2 * ng:1 + 3 * ng]
    o_ref = refs[1 + 3 * ng + n_alias]
    nss = refs[2 + 3 * ng + n_alias:]
    b = pl.program_id(0)
    res = []
    groups = tuple(zip(sts, kns, kcs, nss))
    for gi, (st_ref, kn_ref, kc_ref, ns_ref) in enumerate(groups):
        window, dil = B_GROUPS[gi]
        q = q_ref[0, gi * _SAMPLE_ROWS:(gi + 1) * _SAMPLE_ROWS, :]
        st = st_ref[...]
        kn = kn_ref[0].astype(BF).astype(F32)
        s = _dot(q, st[:KV_W].astype(BF))
        if dil > 1:
            lane = lax.broadcasted_iota(I32, s.shape, 1)
            s = jnp.where((lane & (dil - 1)) == 0, s, -jnp.inf)
        s_new = jnp.sum(q.astype(F32) * kn[:, :KV_W], axis=1, keepdims=True)
        mx = jnp.maximum(jnp.max(s, axis=1, keepdims=True), s_new)
        e = jnp.exp(s - mx)
        e_new = jnp.exp(s_new - mx)
        den = jnp.sum(e, axis=1, keepdims=True) + e_new
        o = (_dot_nt(e.astype(BF), st[KV_W:].astype(BF)) + e_new.astype(BF).astype(F32) * kn[:, KV_W:]) / den
        res.append((o, mx, den))
        kc = kc_ref[...]
        col = jnp.sum(jnp.where(lax.broadcasted_iota(I32, kc.shape, 1) == b, kc, 0.0), axis=1, keepdims=True)
        last = lax.broadcasted_iota(I32, st.shape, 1) == window - 1
        ns_ref[...] = jnp.where(last, col, pltpu.roll(st, window - 1, 1))
    mmax = jnp.maximum(jnp.maximum(res[0][1], res[1][1]), res[2][1])
    wgt = [r[2] * jnp.exp(r[1] - mmax) for r in res]
    num = wgt[0] * res[0][0] + wgt[1] * res[1][0] + wgt[2] * res[2][0]
    o_ref[0] = num / (wgt[0] + wgt[1] + wgt[2])


def _attn_b_sample(qblk3, states_t, kvns, kvcs, layer, prev_new_states):
    n_seq = qblk3.shape[0]
    ng = len(B_GROUPS)
    n_fill = 1 if prev_new_states else states_t[0].shape[0]
    in_specs = [pl.BlockSpec((1, ng * _SAMPLE_ROWS, KV_W), lambda b, f: (b, 0, 0))]
    out_specs = [pl.BlockSpec((1, _SAMPLE_ROWS, KV_W), lambda b, f: (b, 0, 0))]
    out_shapes = [jax.ShapeDtypeStruct((n_seq, _SAMPLE_ROWS, KV_W), F32)]
    out_map = (lambda b, f: (layer, b, 0, 0)) if n_fill == 1 else (lambda b, f: (f, b, 0, 0))
    for st, (window, _) in zip(states_t, B_GROUPS):
        assert st.shape[2:] == (2 * KV_W, window)
        in_specs.append(pl.BlockSpec((None, None, 2 * KV_W, window), lambda b, f: (layer, b, 0, 0)))
        out_specs.append(pl.BlockSpec((None, None, 2 * KV_W, window), out_map))
        out_shapes.append(jax.ShapeDtypeStruct(st.shape, F32))
    in_specs += [pl.BlockSpec((1, 1, 2 * KV_W), lambda b, f: (b, 0, 0))] * ng
    in_specs += [pl.BlockSpec((2 * KV_W, n_seq), lambda b, f: (0, 0))] * ng
    aliases = {}
    if prev_new_states:
        first = len(in_specs)
        in_specs += [pl.BlockSpec(memory_space=pl.ANY)] * ng
        aliases = {first + gi: 1 + gi for gi in range(ng)}
    outs = pl.pallas_call(
        functools.partial(_attn_b_sample_body, n_seq=n_seq, n_alias=len(aliases), layer=layer, n_fill=n_fill),
        grid=(n_seq, n_fill),
        in_specs=in_specs,
        out_specs=out_specs,
        out_shape=out_shapes,
        input_output_aliases=aliases,
        compiler_params=_cp(2),
        name="attn_b_sample",
    )(qblk3, *states_t, *kvns, *kvcs, *prev_new_states)
    return outs[0], outs[1:]


def _memattn_sample_body(q_ref, mem_ref, o_ref):
    q = q_ref[0]
    mem = mem_ref[...].astype(BF)
    s = _dot(q, mem[:KV_W])
    e = jnp.exp(s - jnp.max(s, axis=1, keepdims=True))
    p = e / jnp.sum(e, axis=1, keepdims=True)
    o_ref[0] = _dot_nt(p.astype(BF), mem[KV_W:])


def _memattn_sample(mq3, mem_t, layer):
    n_seq, mem_len = mem_t.shape[1], mem_t.shape[3]
    return pl.pallas_call(
        _memattn_sample_body,
        grid=(n_seq,),
        in_specs=[pl.BlockSpec((1, _SAMPLE_ROWS, KV_W), lambda b: (b, 0, 0)),
                  pl.BlockSpec((None, None, 2 * KV_W, mem_len), lambda b: (layer, b, 0, 0))],
        out_specs=pl.BlockSpec((1, _SAMPLE_ROWS, KV_W), lambda b: (b, 0, 0)),
        out_shape=jax.ShapeDtypeStruct((n_seq, _SAMPLE_ROWS, KV_W), F32),
        compiler_params=_cp(1),
        name="memattn_sample",
    )(mq3, mem_t)


def _feature_major(x):
    nd = x.ndim
    xt = jnp.transpose(x, tuple(range(nd - 4)) + (nd - 3, nd - 2, nd - 1, nd - 4))
    return xt.reshape(x.shape[:nd - 4] + (2 * KV_W, x.shape[nd - 4]))


def _token_major(x, heads):
    nd = x.ndim
    xr = x.reshape(x.shape[:nd - 2] + (2, heads, HEAD_DIM, x.shape[nd - 1]))
    return jnp.transpose(xr, tuple(range(nd - 2)) + (nd + 1, nd - 2, nd - 1, nd))


def kernel(x_prompt, x_sample, cache_kv_a, cache_kidx_a, state_kv_b1, state_kv_b2, state_kv_b3, cache_mem_kv, page_table, mem_prompt, ffn_norm, w_ffn_in, w_ffn_out, mix_norm, mem_norm, w_mem_kv, w_in_a, w_out_a, w_in_b, w_out_b, final_norm):
    batch, seq, d = x_prompt.shape
    n_seq = x_sample.shape[0]
    depth = ffn_norm.shape[0]
    n_pages = page_table.shape[1]
    past = n_pages * PAGE_SIZE
    mem_len = mem_prompt.shape[1]
    assert x_sample.shape[1] == 1 and seq % TM == 0
    ng = len(B_GROUPS)

    yp = x_prompt.reshape(batch * seq, d)
    ys = x_sample.reshape(n_seq, d)
    tab_p = _rope_tables(jnp.arange(seq))
    tab_s = _rope_tables(jnp.full((n_seq,), past, I32))
    w_in_bf = w_ffn_in.astype(BF)
    w_out_bf = w_ffn_out.astype(BF)
    mem_flat = mem_prompt.reshape(batch * mem_len, d)
    kv_pool_t = _feature_major(cache_kv_a)
    kidx_pool_t = jnp.swapaxes(cache_kidx_a, 2, 3)
    states_t = [_feature_major(s) for s in (state_kv_b1, state_kv_b2, state_kv_b3)]
    mem_cache_t = _feature_major(cache_mem_kv)

    kv_a_p, kidx_a_p, kv_a_s, kidx_a_s = [], [], [], []
    b_p = ([], [], [])
    new_states = []
    mem_t = _memproj(mem_flat, mem_norm, w_mem_kv, batch)
    for i in range(depth):
        j = i // 2
        yp = _ffn(yp, ffn_norm[i, 0], w_in_bf, w_out_bf, i, 0, TM)
        ys = _ffn(ys, ffn_norm[i, 0], w_in_bf, w_out_bf, i, 0, n_seq)
        if i % 2 == 0:
            w = w_in_a[j]
            q_al, k_bf, iq_al, ik128, mq_al, kv_t, v_t, ik_t, iw_t = _proj_a_prompt(yp, mix_norm[i], w, tab_p, batch)
            o = _attn_a_prompt(q_al, iq_al, iw_t, k_bf, ik128, v_t, batch, seq)
            om = _memattn_prompt(mq_al, mem_t, i, seq, TM)
            n_o = A_HEADS * HEAD_DIM
            yp = _outproj(yp, [], [o, om], [w_out_a[j][:n_o], w_out_a[j][n_o:]], TM)
            kv_a_p.append(kv_t)
            kidx_a_p.append(ik_t)

            qblk, kvn, iq_s, ikn, iw_s, mq_blk = _proj_a_sample(ys, mix_norm[i], w, tab_s)
            bias = _idx_sample(page_table, iq_s.reshape(n_seq, IDX_HEADS, IDX_DIM),
                               iw_s[:, :IDX_HEADS].reshape(n_seq, IDX_HEADS, 1),
                               ikn.reshape(n_seq, 1, IDX_DIM), kidx_pool_t, j)
            o_s = _attn_a_sample(page_table, qblk.reshape(n_seq, _SAMPLE_A_ROWS, KV_W),
                                 bias.reshape(n_seq, 1, past + LANES), kvn.reshape(n_seq, 1, 2 * KV_W),
                                 kv_pool_t, j)
            om_s = _memattn_sample(mq_blk.reshape(n_seq, _SAMPLE_ROWS, KV_W), mem_cache_t, i)
            ys = _outproj(ys, [], [o_s.reshape(n_seq, -1), om_s.reshape(n_seq, -1)],
                          [_expand_out(w_out_a[j][:n_o], _A_Q_SLOT, _SAMPLE_A_ROWS),
                           _expand_out(w_out_a[j][n_o:], _MEM_SLOT, _SAMPLE_ROWS)], n_seq)
            kv_a_s.append(kvn.reshape(n_seq, 1, 2, A_KV_HEADS, HEAD_DIM))
            kidx_a_s.append(ikn.reshape(n_seq, 1, IDX_DIM))
        else:
            w = w_in_b[j]
            outs = _proj_b_prompt(yp, mix_norm[i], w, tab_p, batch)
            stats_o, stats_m, stats_d = [], [], []
            for gi, (window, dil) in enumerate(B_GROUPS):
                og, mg, dg = _attn_b_prompt(outs[2 * gi], outs[2 * gi + 1], batch, seq, window, dil)
                stats_o.append(og)
                stats_m.append(mg)
                stats_d.append(dg)
                b_p[gi].append(outs[2 * ng + 1 + gi][:, :, seq - min(window, seq):])
            om = _memattn_prompt(outs[2 * ng], mem_t, i, seq, TM)
            n_o = B_HEADS * HEAD_DIM
            yp = _outproj(yp, stats_o + stats_m + stats_d, [om], [w_out_b[j][:n_o], w_out_b[j][n_o:]], TM)

            souts = _proj_b_sample(ys, mix_norm[i], w, tab_s)
            kvns = [souts[1 + gi].reshape(n_seq, 1, 2 * KV_W) for gi in range(ng)]
            kvcs = [souts[2 + ng + gi] for gi in range(ng)]
            o_s, new_states = _attn_b_sample(souts[0].reshape(n_seq, ng * _SAMPLE_ROWS, KV_W),
                                             states_t, kvns, kvcs, j, new_states)
            om_s = _memattn_sample(souts[1 + ng].reshape(n_seq, _SAMPLE_ROWS, KV_W), mem_cache_t, i)
            ys = _outproj(ys, [], [o_s.reshape(n_seq, -1), om_s.reshape(n_seq, -1)],
                          [_expand_out(w_out_b[j][:n_o], _B_SLOT, _SAMPLE_ROWS),
                           _expand_out(w_out_b[j][n_o:], _MEM_SLOT, _SAMPLE_ROWS)], n_seq)
        yp = _ffn(yp, ffn_norm[i, 1], w_in_bf, w_out_bf, i, 1, TM)
        ys = _ffn(ys, ffn_norm[i, 1], w_in_bf, w_out_bf, i, 1, n_seq)

    y_prompt = _final_norm(yp, final_norm, TM).reshape(batch, seq, d)
    y_sample = _final_norm(ys, final_norm, n_seq).reshape(n_seq, 1, d)
    return (y_prompt, y_sample,
            _token_major(jnp.stack(kv_a_p), A_KV_HEADS), jnp.swapaxes(jnp.stack(kidx_a_p), 2, 3),
            _token_major(jnp.stack(b_p[0]), B_HEADS), _token_major(jnp.stack(b_p[1]), B_HEADS),
            _token_major(jnp.stack(b_p[2]), B_HEADS), _token_major(mem_t, MEM_HEADS),
            jnp.stack(kv_a_s), jnp.stack(kidx_a_s),
            _token_major(new_states[0], B_HEADS), _token_major(new_states[1], B_HEADS),
            _token_major(new_states[2], B_HEADS))
```

```python
import functools

import jax
import jax.numpy as jnp
import numpy as np
from jax import lax
from jax.experimental import pallas as pl
from jax.experimental.pallas import tpu as pltpu

F32 = jnp.float32
BF = jnp.bfloat16
I32 = jnp.int32

HEAD_DIM = 64
ROT_DIM = HEAD_DIM // 4
ROT_HALF = ROT_DIM // 2
ROPE_THETA = 500000.0
ATTN_SCALE = HEAD_DIM ** -0.5
A_HEADS = 12
A_KV_HEADS = 4
A_GROUP = A_HEADS // A_KV_HEADS
IDX_HEADS = 8
IDX_DIM = 64
IDX_SCALE = (IDX_HEADS * IDX_DIM) ** -0.5
TOPK_MAX = 256
B_GROUPS = ((128, 1), (512, 4), (2048, 16))
B_HEADS = 4
MEM_HEADS = 4
EPS = 1e-6
PAGE_SIZE = 128

LANES = 128
SUBLANES = 8
PAIR = 2 * HEAD_DIM
KV_W = A_KV_HEADS * HEAD_DIM
VMEM_LIMIT_V7X = 56 * 2 ** 20
NEG_BIG = -1e30
M_INIT = -3e38
INT_MAX = 2 ** 31 - 1
KEY_NEG_INF = -2139095041
MAX_SEARCH_PASSES = 32

TM = 512
TQ = 1024
KC = 512
CLS = 128
PAGES_PER_STEP = 16
ONES_ROWS = 16


def _cp(n_axes):
    return pltpu.CompilerParams(dimension_semantics=("arbitrary",) * n_axes,
                                vmem_limit_bytes=VMEM_LIMIT_V7X)


def _dot(a, b):
    return jnp.dot(a, b, preferred_element_type=F32)


def _dot_nt(a, b):
    return lax.dot_general(a, b, (((1,), (1,)), ((), ())), preferred_element_type=F32)


def _rms(x, g):
    return x * lax.rsqrt(jnp.mean(x * x, axis=-1, keepdims=True) + EPS) * g


def _sort_key(x):
    b = pltpu.bitcast(x + 0.0, I32)
    return b ^ ((b >> 31) & 0x7FFFFFFF)


def _resident(shape):
    return pl.BlockSpec(shape, lambda *_: (0,) * len(shape), pipeline_mode=pl.Buffered(1))


def _mix_residual(x, refs, n_x, combine):
    pos = 0
    xs = []
    if combine:
        og, mg, dg = refs[0:3], refs[3:6], refs[6:9]
        pos = 9
        ms = [r[...] for r in mg]
        mmax = jnp.maximum(jnp.maximum(ms[0], ms[1]), ms[2])
        wgt = [dg[k][...] * jnp.exp(ms[k] - mmax) for k in range(3)]
        num = wgt[0] * og[0][...] + wgt[1] * og[1][...] + wgt[2] * og[2][...]
        merged = num / (wgt[0] + wgt[1] + wgt[2])
        xs.append(jnp.concatenate([merged[c] for c in range(merged.shape[0])], axis=1))
    xs += [refs[pos + k][...] for k in range(n_x)]
    w_refs = refs[pos + n_x:pos + n_x + len(xs)]
    for a, w_ref in zip(xs, w_refs):
        x = x + _dot(a.astype(BF), w_ref[...])
    return x


def _ffn_body(*refs, d_ff, fc, n_x, combine, final):
    y_ref, g_ref, win_ref, wout_ref = refs[:4]
    mix_refs = refs[4:len(refs) - (2 if final else 1)]
    o_ref = refs[-1]
    x = y_ref[...]
    if mix_refs:
        x = _mix_residual(x, mix_refs, n_x, combine)
    xn = _rms(x, g_ref[...]).astype(BF)
    acc = jnp.zeros(x.shape, F32)
    for c in range(d_ff // fc):
        a = _dot(xn, win_ref[:, c * fc:(c + 1) * fc])
        u = _dot(xn, win_ref[:, d_ff + c * fc:d_ff + (c + 1) * fc])
        h = (a * jax.nn.sigmoid(a) * u).astype(BF)
        acc = acc + _dot(h, wout_ref[c * fc:(c + 1) * fc, :])
    out = x + 0.5 * acc
    if final:
        out = _rms(out, refs[-2][...])
    o_ref[...] = out


def _ffn(y, g, w_in_all, w_out_all, layer, half, tm, mix=None, final_g=None):
    m, d = y.shape
    d_ff = w_out_all.shape[2]
    fc = 256 if d_ff % 256 == 0 else LANES
    assert d_ff % fc == 0 and m % tm == 0
    row = lambda width: pl.BlockSpec((tm, width), lambda i: (i, 0))
    vec = pl.BlockSpec((1, d), lambda i: (0, 0))
    in_specs = [row(d), vec,
                pl.BlockSpec((None, None, d, 2 * d_ff), lambda i: (layer, half, 0, 0), pipeline_mode=pl.Buffered(1)),
                pl.BlockSpec((None, None, d_ff, d), lambda i: (layer, half, 0, 0), pipeline_mode=pl.Buffered(1))]
    args = [y, g.reshape(1, d), w_in_all, w_out_all]
    stats, xs, ws = mix if mix else ([], [], [])
    in_specs += [pl.BlockSpec((a.shape[0], tm, LANES), lambda i: (0, i, 0)) for a in stats]
    in_specs += [row(x.shape[1]) for x in xs] + [_resident(w.shape) for w in ws]
    args += list(stats) + list(xs) + [w.astype(BF) for w in ws]
    if final_g is not None:
        in_specs.append(vec)
        args.append(final_g.reshape(1, d))
    return pl.pallas_call(
        functools.partial(_ffn_body, d_ff=d_ff, fc=fc, n_x=len(xs), combine=bool(stats),
                          final=final_g is not None),
        grid=(m // tm,),
        in_specs=in_specs,
        out_specs=row(d),
        out_shape=jax.ShapeDtypeStruct((m, d), F32),
        compiler_params=_cp(1),
        name="ffn",
    )(*args)


def _rope_tables(pos):
    inv = jnp.power(ROPE_THETA, -jnp.arange(ROT_HALF, dtype=F32) / ROT_HALF)
    ang = pos.astype(F32)[:, None] * inv[None, :]
    cos, sin = jnp.cos(ang), jnp.sin(ang)
    t = pos.shape[0]
    ones = jnp.ones((t, HEAD_DIM - ROT_DIM), F32)
    zeros = jnp.zeros((t, HEAD_DIM - ROT_DIM), F32)
    zh = jnp.zeros((t, ROT_HALF), F32)
    cos64 = jnp.concatenate([cos, cos, ones], axis=1)
    sa64 = jnp.concatenate([-sin, zh, zeros], axis=1)
    sb64 = jnp.concatenate([zh, sin, zeros], axis=1)
    row = tuple(jnp.tile(a, (1, LANES // HEAD_DIM)) for a in (cos64, sa64, sb64))
    return row + (cos.T, sin.T)


def _proj_body(*refs, segs, nts, n_in):
    y_ref, g_ref, cos_ref, sa_ref, sb_ref, cost_ref, sint_ref = refs[:7]
    w_refs = refs[7:7 + len(segs)]
    wt_refs = refs[7 + len(segs):n_in]
    outs = refs[n_in:]
    xn = _rms(y_ref[...], g_ref[...]).astype(BF)
    for w_ref, (lo, hi, rope, scale, writes) in zip(w_refs, segs):
        p = _dot(xn, w_ref[...])
        if rope:
            cos, sa, sb = cos_ref[...], sa_ref[...], sb_ref[...]
            chunks = []
            for c in range((hi - lo) // LANES):
                pc = p[:, c * LANES:(c + 1) * LANES]
                chunks.append(pc * cos + pltpu.roll(pc, LANES - ROT_HALF, 1) * sa
                              + pltpu.roll(pc, ROT_HALF, 1) * sb)
            p = chunks[0] if len(chunks) == 1 else jnp.concatenate(chunks, axis=1)
        if scale != 1.0:
            p = p * scale
        for oi, slo, shi, dlo in writes:
            if len(outs[oi].shape) == 3:
                for c in range((shi - slo) // LANES):
                    outs[oi][dlo // LANES + c] = p[:, slo + c * LANES:slo + (c + 1) * LANES].astype(outs[oi].dtype)
            else:
                outs[oi][:, dlo:dlo + (shi - slo)] = p[:, slo:shi].astype(outs[oi].dtype)
    for t, (rope_rows, scale, writes) in enumerate(nts):
        r = _dot_nt(wt_refs[t][...], xn)
        if rope_rows:
            ct, st = cost_ref[...], sint_ref[...]
            pieces = []
            for h0 in range(0, rope_rows, HEAD_DIM):
                x1, x2 = r[h0:h0 + ROT_HALF], r[h0 + ROT_HALF:h0 + ROT_DIM]
                pieces += [x1 * ct - x2 * st, x2 * ct + x1 * st, r[h0 + ROT_DIM:h0 + HEAD_DIM]]
            if rope_rows < r.shape[0]:
                pieces.append(r[rope_rows:])
            r = jnp.concatenate(pieces, axis=0)
        if scale != 1.0:
            r = r * scale
        for oi, slo, shi, dlo in writes:
            outs[oi][dlo:dlo + (shi - slo), :] = r[slo:shi].astype(outs[oi].dtype)


def _proj(y, g, tables, n_seqs, w, wts, segs, nts, out_defs, tm):
    m, d = y.shape
    assert m % tm == 0
    per_seq = m // n_seqs // tm if tables[0].shape[0] != tm else 1
    n_in = 7 + len(w) + len(wts)
    pos_row = lambda i: (i % per_seq, 0)
    pos_col = lambda i: (0, i % per_seq)
    in_specs = [pl.BlockSpec((tm, d), lambda i: (i, 0)),
                pl.BlockSpec((1, d), lambda i: (0, 0)),
                pl.BlockSpec((tm, LANES), pos_row),
                pl.BlockSpec((tm, LANES), pos_row),
                pl.BlockSpec((tm, LANES), pos_row),
                pl.BlockSpec((ROT_HALF, tm), pos_col),
                pl.BlockSpec((ROT_HALF, tm), pos_col)]
    in_specs += [_resident(a.shape) for a in w] + [_resident(wt.shape) for wt in wts]
    out_specs, out_shapes = [], []
    for kind, size, dtype in out_defs:
        if kind == "row":
            out_specs.append(pl.BlockSpec((tm, size), lambda i: (i, 0)))
            out_shapes.append(jax.ShapeDtypeStruct((m, size), dtype))
        elif kind == "chunks":
            out_specs.append(pl.BlockSpec((size // LANES, tm, LANES), lambda i: (0, i, 0)))
            out_shapes.append(jax.ShapeDtypeStruct((size // LANES, m, LANES), dtype))
        elif kind == "col":
            out_specs.append(pl.BlockSpec((size, tm), lambda i: (0, i)))
            out_shapes.append(jax.ShapeDtypeStruct((size, m), dtype))
        else:
            out_specs.append(pl.BlockSpec((None, size, tm), lambda i: (i // per_seq, 0, i % per_seq)))
            out_shapes.append(jax.ShapeDtypeStruct((n_seqs, size, m // n_seqs), dtype))
    return pl.pallas_call(
        functools.partial(_proj_body, segs=tuple(segs), nts=tuple(nts), n_in=n_in),
        grid=(m // tm,),
        in_specs=in_specs,
        out_specs=out_specs,
        out_shape=out_shapes,
        compiler_params=_cp(1),
        name="proj",
    )(y, g.reshape(1, d), *tables, *w, *wts)


def _heads(w, lo, n_heads):
    return w[:, lo:lo + n_heads * HEAD_DIM].reshape(w.shape[0], n_heads, HEAD_DIM)


def _aligned(wh, halves):
    d, h, _ = wh.shape
    z = jnp.zeros_like(wh)
    sel = jnp.asarray(halves, I32).reshape(1, h, 1)
    lo = jnp.where(sel == 0, wh, z)
    hi = jnp.where(sel == 1, wh, z)
    return jnp.concatenate([lo, hi], axis=2).reshape(d, h * PAIR)


def _blocked(wh, slots, n_rows):
    d, h, _ = wh.shape
    onehot = jnp.asarray(np.eye(A_KV_HEADS, dtype=np.float32)[np.asarray(slots)], wh.dtype)
    blk = (wh[:, :, None, :] * onehot[None, :, :, None]).reshape(d, h * KV_W)
    return jnp.pad(blk, ((0, 0), (0, (n_rows - h) * KV_W)))


def _expand_out(w_rows, slots, n_rows):
    h = len(slots)
    d = w_rows.shape[1]
    wh = w_rows.reshape(h, HEAD_DIM, d)
    onehot = jnp.asarray(np.eye(A_KV_HEADS, dtype=np.float32)[np.asarray(slots)], w_rows.dtype)
    blk = (wh[:, None, :, :] * onehot[:, :, None, None]).reshape(h * KV_W, d)
    return jnp.pad(blk, ((0, (n_rows - h) * KV_W), (0, 0)))


_A_OFF = np.cumsum([0, A_HEADS * HEAD_DIM, KV_W, KV_W, IDX_HEADS * IDX_DIM, IDX_DIM, IDX_HEADS]).tolist()
_A_Q_HALF = [(h // A_GROUP) % 2 for h in range(A_HEADS)]
_A_Q_SLOT = [h // A_GROUP for h in range(A_HEADS)]
_MEM_HALF = [h % 2 for h in range(MEM_HEADS)]
_MEM_SLOT = list(range(MEM_HEADS))
_B_HALF = [h % 2 for h in range(B_HEADS)]
_B_SLOT = list(range(B_HEADS))
_SAMPLE_A_ROWS = 16
_SAMPLE_ROWS = 8


def _segments(items):
    segs, cols, lo = [], [], 0
    for wc, rope, scale, writes in items:
        hi = lo + wc.shape[1]
        segs.append((lo, hi, rope, scale, tuple(writes)))
        cols.append(wc.astype(BF))
        lo = hi
    return cols, segs


def _proj_a_prompt(y, g, w, tables, batch):
    o = _A_OFF
    zpad = jnp.zeros((w.shape[0], LANES - IDX_DIM), w.dtype)
    items = [
        (_aligned(_heads(w, o[0], A_HEADS), _A_Q_HALF), True, ATTN_SCALE, [(0, 0, A_HEADS * PAIR, 0)]),
        (w[:, o[1]:o[2]], True, 1.0, [(1, 0, KV_W, 0)]),
        (_aligned(_heads(w, o[3], IDX_HEADS), [0] * IDX_HEADS), True, 1.0, [(2, 0, IDX_HEADS * PAIR, 0)]),
        (jnp.concatenate([w[:, o[4]:o[5]], zpad], axis=1), True, 1.0, [(3, 0, LANES, 0)]),
        (_aligned(_heads(w, o[6], MEM_HEADS), _MEM_HALF), False, ATTN_SCALE, [(4, 0, MEM_HEADS * PAIR, 0)]),
    ]
    wcat, segs = _segments(items)
    wts = [w[:, o[1]:o[3]].T.astype(BF), w[:, o[4]:o[5]].T.astype(BF), w[:, o[5]:o[6]].T.astype(BF)]
    nts = [(KV_W, 1.0, ((5, 0, 2 * KV_W, 0), (6, KV_W, 2 * KV_W, 0))),
           (IDX_DIM, 1.0, ((7, 0, IDX_DIM, 0),)),
           (0, IDX_SCALE, ((8, 0, IDX_HEADS, 0),))]
    out_defs = [("row", A_HEADS * PAIR, BF), ("row", KV_W, BF), ("row", IDX_HEADS * PAIR, BF),
                ("row", LANES, BF), ("row", MEM_HEADS * PAIR, BF), ("bcol", 2 * KV_W, F32),
                ("col", KV_W, BF), ("bcol", IDX_DIM, F32), ("col", IDX_HEADS, F32)]
    return _proj(y, g, tables, batch, wcat, wts, segs, nts, out_defs, TM)


def _proj_a_sample(y, g, w, tables):
    o = _A_OFF
    zpad = jnp.zeros((w.shape[0], LANES - IDX_DIM), w.dtype)
    wpad = jnp.zeros((w.shape[0], LANES - IDX_HEADS), w.dtype)
    items = [
        (_blocked(_heads(w, o[0], A_HEADS), _A_Q_SLOT, _SAMPLE_A_ROWS), True, ATTN_SCALE,
         [(0, 0, _SAMPLE_A_ROWS * KV_W, 0)]),
        (w[:, o[1]:o[2]], True, 1.0, [(1, 0, KV_W, 0)]),
        (w[:, o[2]:o[3]], False, 1.0, [(1, 0, KV_W, KV_W)]),
        (w[:, o[3]:o[4]], True, 1.0, [(2, 0, IDX_HEADS * IDX_DIM, 0)]),
        (jnp.concatenate([w[:, o[4]:o[5]], zpad], axis=1), True, 1.0, [(3, 0, IDX_DIM, 0)]),
        (jnp.concatenate([w[:, o[5]:o[6]], wpad], axis=1), False, IDX_SCALE, [(4, 0, LANES, 0)]),
        (_blocked(_heads(w, o[6], MEM_HEADS), _MEM_SLOT, _SAMPLE_ROWS), False, ATTN_SCALE,
         [(5, 0, _SAMPLE_ROWS * KV_W, 0)]),
    ]
    wcat, segs = _segments(items)
    out_defs = [("row", _SAMPLE_A_ROWS * KV_W, BF), ("row", 2 * KV_W, F32), ("row", IDX_HEADS * IDX_DIM, BF),
                ("row", IDX_DIM, F32), ("row", LANES, F32), ("row", _SAMPLE_ROWS * KV_W, BF)]
    return _proj(y, g, tables, 1, wcat, [], segs, [], out_defs, y.shape[0])


def _b_off(g, t):
    return (g * 3 + t) * B_HEADS * HEAD_DIM


def _proj_b_prompt(y, g, w, tables, batch):
    ng = len(B_GROUPS)
    items, out_defs, wts, nts = [], [], [], []
    for gi in range(ng):
        items += [
            (_aligned(_heads(w, _b_off(gi, 0), B_HEADS), _B_HALF), True, ATTN_SCALE,
             [(2 * gi, 0, B_HEADS * PAIR, 0)]),
            (w[:, _b_off(gi, 1):_b_off(gi, 1) + KV_W], True, 1.0, [(2 * gi + 1, 0, KV_W, 0)]),
            (w[:, _b_off(gi, 2):_b_off(gi, 2) + KV_W], False, 1.0, [(2 * gi + 1, 0, KV_W, KV_W)]),
        ]
        out_defs += [("chunks", B_HEADS * PAIR, F32), ("chunks", 2 * KV_W, F32)]
    items.append((_aligned(_heads(w, _b_off(ng, 0), MEM_HEADS), _MEM_HALF), False, ATTN_SCALE,
                  [(2 * ng, 0, MEM_HEADS * PAIR, 0)]))
    out_defs.append(("row", MEM_HEADS * PAIR, BF))
    for gi in range(ng):
        wts.append(w[:, _b_off(gi, 1):_b_off(gi, 1) + 2 * KV_W].T.astype(BF))
        nts.append((KV_W, 1.0, ((2 * ng + 1 + gi, 0, 2 * KV_W, 0),)))
        out_defs.append(("bcol", 2 * KV_W, F32))
    wcat, segs = _segments(items)
    return _proj(y, g, tables, batch, wcat, wts, segs, nts, out_defs, TM)


def _proj_b_sample(y, g, w, tables):
    ng = len(B_GROUPS)
    items, out_defs, wts, nts = [], [("row", ng * _SAMPLE_ROWS * KV_W, BF)], [], []
    for gi in range(ng):
        items += [
            (_blocked(_heads(w, _b_off(gi, 0), B_HEADS), _B_SLOT, _SAMPLE_ROWS), True, ATTN_SCALE,
             [(0, 0, _SAMPLE_ROWS * KV_W, gi * _SAMPLE_ROWS * KV_W)]),
            (w[:, _b_off(gi, 1):_b_off(gi, 1) + KV_W], True, 1.0, [(1 + gi, 0, KV_W, 0)]),
            (w[:, _b_off(gi, 2):_b_off(gi, 2) + KV_W], False, 1.0, [(1 + gi, 0, KV_W, KV_W)]),
        ]
        out_defs.append(("row", 2 * KV_W, F32))
    items.append((_blocked(_heads(w, _b_off(ng, 0), MEM_HEADS), _MEM_SLOT, _SAMPLE_ROWS), False, ATTN_SCALE,
                  [(1 + ng, 0, _SAMPLE_ROWS * KV_W, 0)]))
    out_defs.append(("row", _SAMPLE_ROWS * KV_W, BF))
    for gi in range(ng):
        wts.append(w[:, _b_off(gi, 1):_b_off(gi, 1) + 2 * KV_W].T.astype(BF))
        nts.append((KV_W, 1.0, ((2 + ng + gi, 0, 2 * KV_W, 0),)))
        out_defs.append(("col", 2 * KV_W, F32))
    wcat, segs = _segments(items)
    return _proj(y, g, tables, 1, wcat, wts, segs, nts, out_defs, y.shape[0])


def _memproj_body(y_ref, g_ref, wt_ref, o_ref):
    xn = _rms(y_ref[...], g_ref[...]).astype(BF)
    o_ref[...] = _dot_nt(wt_ref[...], xn)


def _memproj(mem_flat, g_all, w_all, batch):
    m, d = mem_flat.shape
    mem_len = m // batch
    depth, _, n = w_all.shape
    return pl.pallas_call(
        _memproj_body,
        grid=(depth, batch),
        in_specs=[pl.BlockSpec((mem_len, d), lambda l, b: (b, 0)),
                  pl.BlockSpec((None, 1, d), lambda l, b: (l, 0, 0)),
                  pl.BlockSpec((None, n, d), lambda l, b: (l, 0, 0))],
        out_specs=pl.BlockSpec((None, None, n, mem_len), lambda l, b: (l, b, 0, 0)),
        out_shape=jax.ShapeDtypeStruct((depth, batch, n, mem_len), F32),
        compiler_params=_cp(2),
        name="memproj",
    )(mem_flat, g_all.reshape(depth, 1, d), jnp.swapaxes(w_all, 1, 2).astype(BF))


def _select_bias(count, read, write, n_chunks, rows, k_sel, red_shape, key_axis):
    lo0 = KEY_NEG_INF + 1
    c_valid = count(lo0)
    c_zero = count(0)
    pos = c_zero >= k_sel
    few = c_valid <= k_sel
    lo = jnp.where(pos, 0, lo0).astype(I32)
    hi = jnp.where(pos, INT_MAX, 0).astype(I32)
    c_lo = jnp.where(pos, c_zero, c_valid)
    c_hi = jnp.where(pos, 0, c_zero)
    done = jnp.where(jnp.logical_or(few, c_zero == k_sel), 1, 0).astype(I32)
    thr = jnp.where(few, lo0, 0).astype(I32)
    full = jnp.full(red_shape, k_sel, I32)

    def cond(st):
        return jnp.logical_and(st[0] < MAX_SEARCH_PASSES, st[1] == 0)

    def body(st):
        it, _, lo, hi, c_lo, c_hi, thr, need, n_ge, done = st
        cand = lo + jnp.maximum((hi - lo) >> 1, 1)
        c = count(cand)
        hit = c == k_sel
        up = c > k_sel
        lo_n = jnp.where(up, cand, lo)
        hi_n = jnp.where(up, hi, cand)
        c_lo_n = jnp.where(up, c, c_lo)
        c_hi_n = jnp.where(up, c_hi, c)
        newly = jnp.logical_and(done == 0, jnp.logical_or(hit, hi_n - lo_n == 1))
        thr_n = jnp.where(newly, jnp.where(hit, cand, lo_n), thr)
        need_n = jnp.where(jnp.logical_and(newly, jnp.logical_not(hit)), k_sel - c_hi_n, need)
        n_ge_n = jnp.where(jnp.logical_and(newly, jnp.logical_not(hit)), c_lo_n, n_ge)
        done_n = jnp.where(newly, 1, done)
        return (it + 1, jnp.min(done_n), lo_n, hi_n, c_lo_n, c_hi_n, thr_n, need_n, n_ge_n, done_n)

    st = lax.while_loop(cond, body, (jnp.int32(0), jnp.min(done), lo, hi, c_lo, c_hi, thr, full, full, done))
    thr, need_i, n_ge = st[6], st[7], st[8]
    has_tie = jnp.max(n_ge) > k_sel

    @pl.when(jnp.logical_not(has_tie))
    def _():
        def wbody(c, _):
            x = read(c)
            sel = jnp.logical_and(x >= thr, x > KEY_NEG_INF)
            write(c, jnp.where(sel, 0.0, NEG_BIG).astype(F32))
            return 0
        lax.fori_loop(0, n_chunks, wbody, 0)

    @pl.when(has_tie)
    def _():
        need = need_i.astype(F32)
        r = lax.broadcasted_iota(I32, (rows, rows), 0)
        cidx = lax.broadcasted_iota(I32, (rows, rows), 1)
        if key_axis == 0:
            tri = jnp.where(cidx < r, 1.0, 0.0).astype(BF)
        else:
            tri = jnp.where(r < cidx, 1.0, 0.0).astype(BF)

        def wbody(c, seen):
            x = read(c)
            eq = jnp.logical_and(x == thr, x > KEY_NEG_INF)
            eqf = jnp.where(eq, 1.0, 0.0).astype(BF)
            rank = (_dot(tri, eqf) if key_axis == 0 else _dot(eqf, tri)) + seen
            sel = jnp.logical_and(jnp.logical_or(x > thr, jnp.logical_and(eq, rank < need)), x > KEY_NEG_INF)
            write(c, jnp.where(sel, 0.0, NEG_BIG).astype(F32))
            return seen + jnp.sum(eqf.astype(F32), axis=key_axis, keepdims=True)
        lax.fori_loop(0, n_chunks, wbody, jnp.zeros(red_shape, F32))


def _attn_a_body(q_ref, iq_ref, iw_ref, k_ref, ik_ref, vt_ref, o_ref, sc_ref, acc_ref, *, tq, kc, k_sel):
    qi = pl.program_id(1)
    per_q = tq // kc
    n_full = qi * per_q
    n_chunks = n_full + per_q
    iw = iw_ref[...]

    def chunk_rows(c):
        return pl.ds(pl.multiple_of(c * kc, kc), kc)

    def idx_chunk(c, j):
        l0 = 0 if j is None else j * kc
        ikc = ik_ref[chunk_rows(c), :]
        acc = jnp.zeros((kc, tq - l0), F32)
        for h in range(IDX_HEADS):
            s = _dot_nt(ikc, iq_ref[l0:, h * PAIR:(h + 1) * PAIR])
            acc = acc + jnp.maximum(s, 0.0) * iw[h:h + 1, l0:]
        if j is not None:
            kpos = lax.broadcasted_iota(I32, (kc, tq - l0), 0)
            qpos = lax.broadcasted_iota(I32, (kc, tq - l0), 1)
            acc = jnp.where(kpos <= qpos, acc, -jnp.inf)
            if l0:
                sc_ref[chunk_rows(c), :l0] = jnp.full((kc, l0), KEY_NEG_INF, I32)
        sc_ref[chunk_rows(c), l0:] = _sort_key(acc)

    def full_chunk(c, _):
        idx_chunk(c, None)
        return 0
    lax.fori_loop(0, n_full, full_chunk, 0)
    for j in range(per_q):
        idx_chunk(n_full + j, j)

    def read(c):
        return sc_ref[chunk_rows(c), :]

    def write(c, v):
        sc_ref[chunk_rows(c), :] = pltpu.bitcast(v, I32)

    def count(t):
        def step(x, cnts, l0):
            tt = t if isinstance(t, int) else t[:, l0:]
            cnts = list(cnts)
            for r in range(kc // SUBLANES):
                a = r % len(cnts)
                cnts[a] = jnp.where(x[r * SUBLANES:(r + 1) * SUBLANES] >= tt, cnts[a] + 1, cnts[a])
            return tuple(cnts)

        def zeros(l0):
            return (jnp.zeros((SUBLANES, tq - l0), I32),) * 4
        parts = lax.fori_loop(0, n_full + 1, lambda c, cn: step(read(c), cn, 0), zeros(0))
        total = parts[0] + parts[1] + parts[2] + parts[3]
        for j in range(1, per_q):
            l0 = j * kc
            pj = step(sc_ref[chunk_rows(n_full + j), l0:], zeros(l0), l0)
            total = total + jnp.concatenate([jnp.zeros((SUBLANES, l0), I32), pj[0] + pj[1] + pj[2] + pj[3]], axis=1)
        return jnp.sum(total, axis=0, keepdims=True)

    _select_bias(count, read, write, n_chunks, kc, k_sel, (1, tq), 0)

    acc_ref[...] = jnp.zeros(acc_ref.shape, F32)
    ones_rows = jnp.ones((ONES_ROWS, kc), BF)
    pad_rows = jnp.zeros((2 * SUBLANES - A_HEADS, tq), F32)

    def att_chunk(c, carry, l0):
        m_all, l_all = carry
        rows_c = chunk_rows(c)
        bias = pltpu.bitcast(sc_ref[rows_c, l0:], F32)
        kcs = [k_ref[rows_c, pair * PAIR:(pair + 1) * PAIR] for pair in range(A_KV_HEADS // 2)]
        vts = [jnp.concatenate([vt_ref[n * HEAD_DIM:(n + 1) * HEAD_DIM, rows_c], ones_rows], axis=0)
               for n in range(A_KV_HEADS)]
        m_rows, l_rows = [], []
        for h in range(A_HEADS):
            n = h // A_GROUP
            s = _dot_nt(kcs[n // 2], q_ref[l0:, h * PAIR:(h + 1) * PAIR]) + bias
            m_old = m_all[h:h + 1, l0:]
            m_new = jnp.maximum(m_old, jnp.max(s, axis=0, keepdims=True))
            alpha = jnp.exp(m_old - m_new)
            p = jnp.exp(s - m_new).astype(BF)
            res = _dot(vts[n], p)
            rows = slice(h * HEAD_DIM, (h + 1) * HEAD_DIM)
            l_new = alpha * l_all[h:h + 1, l0:] + res[HEAD_DIM:HEAD_DIM + 1]
            m_rows.append(m_new if l0 == 0 else jnp.concatenate([m_all[h:h + 1, :l0], m_new], axis=1))
            l_rows.append(l_new if l0 == 0 else jnp.concatenate([l_all[h:h + 1, :l0], l_new], axis=1))
            acc_ref[rows, l0:] = alpha * acc_ref[rows, l0:] + res[:HEAD_DIM]
        return (jnp.concatenate(m_rows + [pad_rows], axis=0), jnp.concatenate(l_rows + [pad_rows], axis=0))

    carry = (jnp.full((2 * SUBLANES, tq), M_INIT, F32), jnp.zeros((2 * SUBLANES, tq), F32))
    carry = lax.fori_loop(0, n_full + 1, lambda c, cr: att_chunk(c, cr, 0), carry)
    for j in range(1, per_q):
        carry = att_chunk(n_full + j, carry, j * kc)
    l_all = carry[1]

    for hp in range(A_HEADS // 2):
        outs = [acc_ref[h * HEAD_DIM:(h + 1) * HEAD_DIM, :] / l_all[h:h + 1] for h in (2 * hp, 2 * hp + 1)]
        o_ref[:, hp * PAIR:(hp + 1) * PAIR] = jnp.concatenate(outs, axis=0).T.astype(o_ref.dtype)


def _attn_a_prompt(q_al, iq_al, iw_t, k_bf, ik128, v_t, batch, seq):
    tq = min(TQ, seq)
    kc = min(KC, tq)
    nq = seq // tq
    m = batch * seq
    k_sel = min(TOPK_MAX, seq // 4)
    return pl.pallas_call(
        functools.partial(_attn_a_body, tq=tq, kc=kc, k_sel=k_sel),
        grid=(batch, nq),
        in_specs=[pl.BlockSpec((tq, A_HEADS * PAIR), lambda b, i: (b * nq + i, 0)),
                  pl.BlockSpec((tq, IDX_HEADS * PAIR), lambda b, i: (b * nq + i, 0)),
                  pl.BlockSpec((IDX_HEADS, tq), lambda b, i: (0, b * nq + i)),
                  pl.BlockSpec((seq, KV_W), lambda b, i: (b, 0)),
                  pl.BlockSpec((seq, LANES), lambda b, i: (b, 0)),
                  pl.BlockSpec((KV_W, seq), lambda b, i: (0, b))],
        out_specs=pl.BlockSpec((tq, A_HEADS * HEAD_DIM), lambda b, i: (b * nq + i, 0)),
        out_shape=jax.ShapeDtypeStruct((m, A_HEADS * HEAD_DIM), BF),
        scratch_shapes=[pltpu.VMEM((seq, tq), I32),
                        pltpu.VMEM((A_HEADS * HEAD_DIM, tq), F32)],
        compiler_params=_cp(2),
        name="attn_a_prompt",
    )(q_al, iq_al, iw_t, k_bf, ik128, v_t)


def _attn_b_body(q_ref, kvc_ref, kvp_ref, o_ref, m_ref, d_ref, *, dil, span):
    i = pl.program_id(1)
    qrow = lax.broadcasted_iota(I32, (CLS, 2 * CLS), 0)
    kcol = lax.broadcasted_iota(I32, (CLS, 2 * CLS), 1)
    rel = qrow + CLS - kcol
    valid = jnp.logical_and(jnp.logical_and(rel >= 0, rel <= span), jnp.logical_or(kcol >= CLS, i > 0))
    n_pairs = B_HEADS // 2
    valid4 = jnp.concatenate([valid] * B_HEADS, axis=0)
    slot = lax.broadcasted_iota(I32, (CLS, KV_W), 1) >> (HEAD_DIM.bit_length() - 1)
    zeros = jnp.zeros((CLS, PAIR), BF)

    def one_class(rows):
        def cat(base):
            return jnp.concatenate(
                [jnp.concatenate([kvp_ref[base + p, rows, :], kvc_ref[base + p, rows, :]], axis=0)
                 for p in range(n_pairs)], axis=1).astype(BF)
        kcat, vcat = cat(0), cat(n_pairs)
        qrows = []
        for h in range(B_HEADS):
            qh = q_ref[h, rows, :].astype(BF)
            qrows.append(jnp.concatenate([qh, zeros] if h // 2 == 0 else [zeros, qh], axis=1))
        s = jnp.where(valid4, _dot_nt(jnp.concatenate(qrows, axis=0), kcat), -jnp.inf)
        mx = jnp.max(s, axis=1, keepdims=True)
        e = jnp.exp(s - mx)
        den = jnp.sum(e, axis=1, keepdims=True)
        o = _dot(e.astype(BF), vcat) / den
        o_nat = jnp.zeros((CLS, KV_W), F32)
        m_nat = jnp.zeros((CLS, KV_W), F32)
        d_nat = jnp.zeros((CLS, KV_W), F32)
        for h in range(B_HEADS):
            blk = slice(h * CLS, (h + 1) * CLS)
            o_nat = jnp.where(slot == h, o[blk], o_nat)
            m_nat = jnp.where(slot == h, mx[blk], m_nat)
            d_nat = jnp.where(slot == h, den[blk], d_nat)
        for p in range(n_pairs):
            o_ref[p, rows, :] = o_nat[:, p * PAIR:(p + 1) * PAIR]
            m_ref[p, rows, :] = m_nat[:, p * PAIR:(p + 1) * PAIR]
            d_ref[p, rows, :] = d_nat[:, p * PAIR:(p + 1) * PAIR]

    if dil == 1:
        one_class(pl.ds(0, CLS))
    else:
        def body(r, _):
            one_class(pl.ds(r, CLS, stride=dil))
            return 0
        lax.fori_loop(0, dil, body, 0)


def _attn_b_prompt(q_al, kv, batch, seq, window, dil):
    rows = CLS * dil
    span = window // dil
    assert span <= CLS and seq % rows == 0
    nr = seq // rows
    m = batch * seq
    n_pairs = B_HEADS // 2
    stat = jax.ShapeDtypeStruct((n_pairs, m, PAIR), F32)
    ospec = pl.BlockSpec((n_pairs, rows, PAIR), lambda b, i: (0, b * nr + i, 0))
    return pl.pallas_call(
        functools.partial(_attn_b_body, dil=dil, span=span),
        grid=(batch, nr),
        in_specs=[pl.BlockSpec((B_HEADS, rows, PAIR), lambda b, i: (0, b * nr + i, 0)),
                  pl.BlockSpec((2 * n_pairs, rows, PAIR), lambda b, i: (0, b * nr + i, 0)),
                  pl.BlockSpec((2 * n_pairs, rows, PAIR), lambda b, i: (0, b * nr + jnp.maximum(i - 1, 0), 0))],
        out_specs=[ospec, ospec, ospec],
        out_shape=[stat, stat, stat],
        compiler_params=_cp(2),
        name="attn_b_prompt",
    )(q_al, kv, kv)


def _memattn_body(mq_ref, mem_ref, o_ref):
    mq = mq_ref[...]
    mem = mem_ref[...].astype(BF)
    lo_half = lax.broadcasted_iota(I32, (mq.shape[0], PAIR), 1) < HEAD_DIM
    outs = []
    for pair in range(MEM_HEADS // 2):
        mk = mem[pair * PAIR:(pair + 1) * PAIR]
        mv = mem[KV_W + pair * PAIR:KV_W + (pair + 1) * PAIR]
        res = []
        for h in (2 * pair, 2 * pair + 1):
            s = _dot(mq[:, h * PAIR:(h + 1) * PAIR], mk)
            e = jnp.exp(s - jnp.max(s, axis=1, keepdims=True))
            p = e / jnp.sum(e, axis=1, keepdims=True)
            res.append(_dot_nt(p.astype(BF), mv))
        outs.append(jnp.where(lo_half, res[0], res[1]))
    o_ref[...] = jnp.concatenate(outs, axis=1).astype(o_ref.dtype)


def _memattn_prompt(mq_al, mem_t, layer, seq, tm):
    m = mq_al.shape[0]
    mem_len = mem_t.shape[3]
    tiles_per_seq = seq // tm
    return pl.pallas_call(
        _memattn_body,
        grid=(m // tm,),
        in_specs=[pl.BlockSpec((tm, MEM_HEADS * PAIR), lambda i: (i, 0)),
                  pl.BlockSpec((None, None, 2 * KV_W, mem_len), lambda i: (layer, i // tiles_per_seq, 0, 0))],
        out_specs=pl.BlockSpec((tm, KV_W), lambda i: (i, 0)),
        out_shape=jax.ShapeDtypeStruct((m, KV_W), BF),
        compiler_params=_cp(1),
        name="memattn_prompt",
    )(mq_al, mem_t)


def _page_copies(pool_ref, buf_ref, sem_ref, pt_ref, layer, b, first_page, n_pages, slot):
    return [pltpu.make_async_copy(pool_ref.at[layer, pt_ref[b, first_page + p]],
                                  buf_ref.at[slot, :, pl.ds(p * PAGE_SIZE, PAGE_SIZE)],
                                  sem_ref.at[slot]) for p in range(n_pages)]


def _idx_sample_body(pt_ref, iq_ref, iw_ref, ikn_ref, pool_ref, bias_ref, kbuf, sem, sc_ref,
                     *, layer, n_seq, n_pages, k_sel):
    b = pl.program_id(0)
    past = n_pages * PAGE_SIZE
    slot = b % 2

    @pl.when(b == 0)
    def _():
        sc_ref[...] = jnp.zeros(sc_ref.shape, I32)
        for cp in _page_copies(pool_ref, kbuf, sem, pt_ref, layer, 0, 0, n_pages, 0):
            cp.start()

    @pl.when(b + 1 < n_seq)
    def _():
        for cp in _page_copies(pool_ref, kbuf, sem, pt_ref, layer, b + 1, 0, n_pages, 1 - slot):
            cp.start()

    for cp in _page_copies(pool_ref, kbuf, sem, pt_ref, layer, b, 0, n_pages, slot):
        cp.wait()

    iq = iq_ref[0]
    iw = iw_ref[0]
    s = _dot(iq, kbuf[slot].astype(BF))
    row = jnp.sum(jnp.maximum(s, 0.0) * iw, axis=0, keepdims=True)
    ikn = ikn_ref[0].astype(BF).astype(F32)
    s_new = jnp.sum(iq.astype(F32) * ikn, axis=1, keepdims=True)
    r_new = jnp.sum(jnp.maximum(s_new, 0.0) * iw, axis=0, keepdims=True)
    lane = lax.broadcasted_iota(I32, (1, LANES), 1)
    tail = jnp.where(lane == 0, r_new, -jnp.inf)
    is_b = lax.broadcasted_iota(I32, (n_seq, 1), 0) == b
    sc_ref[:, :past] = jnp.where(is_b, _sort_key(row), sc_ref[:, :past])
    sc_ref[:, past:] = jnp.where(is_b, _sort_key(tail), sc_ref[:, past:])

    @pl.when(b == n_seq - 1)
    def _():
        def read(c):
            return sc_ref[:, pl.ds(pl.multiple_of(c * LANES, LANES), LANES)]

        def write(c, v):
            bias_ref[:, pl.ds(pl.multiple_of(c * LANES, LANES), LANES)] = v

        def count(t):
            return jnp.sum(jnp.where(sc_ref[...] >= t, 1, 0).astype(I32), axis=1, keepdims=True)

        _select_bias(count, read, write, past // LANES + 1, LANES, k_sel, (n_seq, 1), 1)


def _idx_sample(page_table, iq3, iw3, ikn3, pool_t, layer):
    n_seq, n_pages = page_table.shape
    past = n_pages * PAGE_SIZE
    width = past + LANES
    k_sel = min(TOPK_MAX, (past + 1) // 4)
    grid_spec = pltpu.PrefetchScalarGridSpec(
        num_scalar_prefetch=1,
        grid=(n_seq,),
        in_specs=[pl.BlockSpec((1, IDX_HEADS, IDX_DIM), lambda b, pt: (b, 0, 0)),
                  pl.BlockSpec((1, IDX_HEADS, 1), lambda b, pt: (b, 0, 0)),
                  pl.BlockSpec((1, 1, IDX_DIM), lambda b, pt: (b, 0, 0)),
                  pl.BlockSpec(memory_space=pl.ANY)],
        out_specs=pl.BlockSpec((n_seq, width), lambda b, pt: (0, 0)),
        scratch_shapes=[pltpu.VMEM((2, IDX_DIM, past), F32),
                        pltpu.SemaphoreType.DMA((2,)),
                        pltpu.VMEM((n_seq, width), I32)])
    return pl.pallas_call(
        functools.partial(_idx_sample_body, layer=layer, n_seq=n_seq, n_pages=n_pages, k_sel=k_sel),
        grid_spec=grid_spec,
        out_shape=jax.ShapeDtypeStruct((n_seq, width), F32),
        compiler_params=_cp(1),
        name="idx_sample",
    )(page_table, iq3, iw3, ikn3, pool_t)


def _attn_a_sample_body(pt_ref, q_ref, bias_ref, kvn_ref, pool_ref, o_ref, kvbuf, sem, m_s, l_s, acc_s,
                        *, layer, n_seq, n_steps, pg):
    b, s = pl.program_id(0), pl.program_id(1)
    t = b * n_steps + s
    slot = t % 2
    keys = pg * PAGE_SIZE
    past = n_steps * keys

    @pl.when(t == 0)
    def _():
        for cp in _page_copies(pool_ref, kvbuf, sem, pt_ref, layer, 0, 0, pg, 0):
            cp.start()

    @pl.when(t + 1 < n_seq * n_steps)
    def _():
        wrap = s + 1 == n_steps
        nb = jnp.where(wrap, b + 1, b)
        ns = jnp.where(wrap, 0, s + 1)
        for cp in _page_copies(pool_ref, kvbuf, sem, pt_ref, layer, nb, ns * pg, pg, 1 - slot):
            cp.start()

    for cp in _page_copies(pool_ref, kvbuf, sem, pt_ref, layer, b, s * pg, pg, slot):
        cp.wait()

    @pl.when(s == 0)
    def _():
        m_s[...] = jnp.full(m_s.shape, M_INIT, F32)
        l_s[...] = jnp.zeros(l_s.shape, F32)
        acc_s[...] = jnp.zeros(acc_s.shape, F32)

    qb = q_ref[0]
    kv = kvbuf[slot]
    bias = bias_ref[0, :, pl.ds(pl.multiple_of(s * keys, keys), keys)]
    sc = _dot(qb, kv[:KV_W].astype(BF)) + bias
    m_old = m_s[...]
    m_new = jnp.maximum(m_old, jnp.max(sc, axis=1, keepdims=True))
    alpha = jnp.exp(m_old - m_new)
    p = jnp.exp(sc - m_new)
    l_new = alpha * l_s[...] + jnp.sum(p, axis=1, keepdims=True)
    acc_new = alpha * acc_s[...] + _dot_nt(p.astype(BF), kv[KV_W:].astype(BF))
    m_s[...] = m_new
    l_s[...] = l_new
    acc_s[...] = acc_new

    @pl.when(s == n_steps - 1)
    def _():
        kn = kvn_ref[0].astype(BF).astype(F32)
        b_new = bias_ref[0, :, past:past + LANES][:, 0:1]
        s_new = jnp.sum(qb.astype(F32) * kn[:, :KV_W], axis=1, keepdims=True) + b_new
        m_fin = jnp.maximum(m_new, s_new)
        a_fin = jnp.exp(m_new - m_fin)
        p_new = jnp.exp(s_new - m_fin)
        l_fin = a_fin * l_new + p_new
        o_ref[0] = (a_fin * acc_new + p_new.astype(BF).astype(F32) * kn[:, KV_W:]) / l_fin


def _attn_a_sample(page_table, qblk, bias3, kvn3, pool_t, layer):
    n_seq, n_pages = page_table.shape
    pg = min(PAGES_PER_STEP, n_pages)
    assert n_pages % pg == 0
    n_steps = n_pages // pg
    width = bias3.shape[2]
    grid_spec = pltpu.PrefetchScalarGridSpec(
        num_scalar_prefetch=1,
        grid=(n_seq, n_steps),
        in_specs=[pl.BlockSpec((1, _SAMPLE_A_ROWS, KV_W), lambda b, s, pt: (b, 0, 0)),
                  pl.BlockSpec((1, 1, width), lambda b, s, pt: (b, 0, 0)),
                  pl.BlockSpec((1, 1, 2 * KV_W), lambda b, s, pt: (b, 0, 0)),
                  pl.BlockSpec(memory_space=pl.ANY)],
        out_specs=pl.BlockSpec((1, _SAMPLE_A_ROWS, KV_W), lambda b, s, pt: (b, 0, 0)),
        scratch_shapes=[pltpu.VMEM((2, 2 * KV_W, pg * PAGE_SIZE), F32),
                        pltpu.SemaphoreType.DMA((2,)),
                        pltpu.VMEM((_SAMPLE_A_ROWS, 1), F32),
                        pltpu.VMEM((_SAMPLE_A_ROWS, 1), F32),
                        pltpu.VMEM((_SAMPLE_A_ROWS, KV_W), F32)])
    return pl.pallas_call(
        functools.partial(_attn_a_sample_body, layer=layer, n_seq=n_seq, n_steps=n_steps, pg=pg),
        grid_spec=grid_spec,
        out_shape=jax.ShapeDtypeStruct((n_seq, _SAMPLE_A_ROWS, KV_W), F32),
        compiler_params=_cp(2),
        name="attn_a_sample",
    )(page_table, qblk, bias3, kvn3, pool_t)


def _attn_b_sample_body(*refs, n_seq, n_alias, layer, n_fill):
    ng = len(B_GROUPS)
    nss = refs[2 + 3 * ng + n_alias:]
    if n_fill == 1:
        _attn_b_sample_compute(refs, n_alias)
        return
    fill = pl.program_id(1)

    @pl.when(fill == layer)
    def _():
        _attn_b_sample_compute(refs, n_alias)

    @pl.when(fill != layer)
    def _():
        for ns_ref in nss:
            ns_ref[...] = jnp.zeros(ns_ref.shape, F32)


def _attn_b_sample_compute(refs, n_alias):
    ng = len(B_GROUPS)
    q_ref = refs[0]
    sts, kns, kcs = refs[1:1 + ng], refs[1 + ng:1 + 2 * ng], refs[1 + 2 * ng:1 + 3 * ng]
    o_ref = refs[1 + 3 * ng + n_alias]
    nss = refs[2 + 3 * ng + n_alias:]
    b = pl.program_id(0)
    res = []
    groups = tuple(zip(sts, kns, kcs, nss))
    for gi, (st_ref, kn_ref, kc_ref, ns_ref) in enumerate(groups):
        window, dil = B_GROUPS[gi]
        q = q_ref[0, gi * _SAMPLE_ROWS:(gi + 1) * _SAMPLE_ROWS, :]
        st = st_ref[...]
        kn = kn_ref[0].astype(BF).astype(F32)
        s = _dot(q, st[:KV_W].astype(BF))
        if dil > 1:
            lane = lax.broadcasted_iota(I32, s.shape, 1)
            s = jnp.where((lane & (dil - 1)) == 0, s, -jnp.inf)
        s_new = jnp.sum(q.astype(F32) * kn[:, :KV_W], axis=1, keepdims=True)
        mx = jnp.maximum(jnp.max(s, axis=1, keepdims=True), s_new)
        e = jnp.exp(s - mx)
        e_new = jnp.exp(s_new - mx)
        den = jnp.sum(e, axis=1, keepdims=True) + e_new
        o = (_dot_nt(e.astype(BF), st[KV_W:].astype(BF)) + e_new.astype(BF).astype(F32) * kn[:, KV_W:]) / den
        res.append((o, mx, den))
        kc = kc_ref[...]
        col = jnp.sum(jnp.where(lax.broadcasted_iota(I32, kc.shape, 1) == b, kc, 0.0), axis=1, keepdims=True)
        last = lax.broadcasted_iota(I32, st.shape, 1) == window - 1
        ns_ref[...] = jnp.where(last, col, pltpu.roll(st, window - 1, 1))
    mmax = jnp.maximum(jnp.maximum(res[0][1], res[1][1]), res[2][1])
    wgt = [r[2] * jnp.exp(r[1] - mmax) for r in res]
    num = wgt[0] * res[0][0] + wgt[1] * res[1][0] + wgt[2] * res[2][0]
    o_ref[0] = num / (wgt[0] + wgt[1] + wgt[2])


def _attn_b_sample(qblk3, states_t, kvns, kvcs, layer, prev_new_states):
    n_seq = qblk3.shape[0]
    ng = len(B_GROUPS)
    n_fill = 1 if prev_new_states else states_t[0].shape[0]
    in_specs = [pl.BlockSpec((1, ng * _SAMPLE_ROWS, KV_W), lambda b, f: (b, 0, 0))]
    out_specs = [pl.BlockSpec((1, _SAMPLE_ROWS, KV_W), lambda b, f: (b, 0, 0))]
    out_shapes = [jax.ShapeDtypeStruct((n_seq, _SAMPLE_ROWS, KV_W), F32)]
    out_map = (lambda b, f: (layer, b, 0, 0)) if n_fill == 1 else (lambda b, f: (f, b, 0, 0))
    for st, (window, _) in zip(states_t, B_GROUPS):
        assert st.shape[2:] == (2 * KV_W, window)
        in_specs.append(pl.BlockSpec((None, None, 2 * KV_W, window), lambda b, f: (layer, b, 0, 0)))
        out_specs.append(pl.BlockSpec((None, None, 2 * KV_W, window), out_map))
        out_shapes.append(jax.ShapeDtypeStruct(st.shape, F32))
    in_specs += [pl.BlockSpec((1, 1, 2 * KV_W), lambda b, f: (b, 0, 0))] * ng
    in_specs += [pl.BlockSpec((2 * KV_W, n_seq), lambda b, f: (0, 0))] * ng
    aliases = {}
    if prev_new_states:
        first = len(in_specs)
        in_specs += [pl.BlockSpec(memory_space=pl.ANY)] * ng
        aliases = {first + gi: 1 + gi for gi in range(ng)}
    outs = pl.pallas_call(
        functools.partial(_attn_b_sample_body, n_seq=n_seq, n_alias=len(aliases), layer=layer, n_fill=n_fill),
        grid=(n_seq, n_fill),
        in_specs=in_specs,
        out_specs=out_specs,
        out_shape=out_shapes,
        input_output_aliases=aliases,
        compiler_params=_cp(2),
        name="attn_b_sample",
    )(qblk3, *states_t, *kvns, *kvcs, *prev_new_states)
    return outs[0], outs[1:]


def _memattn_sample_body(q_ref, mem_ref, o_ref):
    q = q_ref[0]
    mem = mem_ref[...].astype(BF)
    s = _dot(q, mem[:KV_W])
    e = jnp.exp(s - jnp.max(s, axis=1, keepdims=True))
    p = e / jnp.sum(e, axis=1, keepdims=True)
    o_ref[0] = _dot_nt(p.astype(BF), mem[KV_W:])


def _memattn_sample(mq3, mem_t, layer):
    n_seq, mem_len = mem_t.shape[1], mem_t.shape[3]
    return pl.pallas_call(
        _memattn_sample_body,
        grid=(n_seq,),
        in_specs=[pl.BlockSpec((1, _SAMPLE_ROWS, KV_W), lambda b: (b, 0, 0)),
                  pl.BlockSpec((None, None, 2 * KV_W, mem_len), lambda b: (layer, b, 0, 0))],
        out_specs=pl.BlockSpec((1, _SAMPLE_ROWS, KV_W), lambda b: (b, 0, 0)),
        out_shape=jax.ShapeDtypeStruct((n_seq, _SAMPLE_ROWS, KV_W), F32),
        compiler_params=_cp(1),
        name="memattn_sample",
    )(mq3, mem_t)


def _feature_major(x):
    nd = x.ndim
    xt = jnp.transpose(x, tuple(range(nd - 4)) + (nd - 3, nd - 2, nd - 1, nd - 4))
    return xt.reshape(x.shape[:nd - 4] + (2 * KV_W, x.shape[nd - 4]))


def _token_major(x, heads):
    nd = x.ndim
    xr = x.reshape(x.shape[:nd - 2] + (2, heads, HEAD_DIM, x.shape[nd - 1]))
    return jnp.transpose(xr, tuple(range(nd - 2)) + (nd + 1, nd - 2, nd - 1, nd))


def kernel(x_prompt, x_sample, cache_kv_a, cache_kidx_a, state_kv_b1, state_kv_b2, state_kv_b3, cache_mem_kv, page_table, mem_prompt, ffn_norm, w_ffn_in, w_ffn_out, mix_norm, mem_norm, w_mem_kv, w_in_a, w_out_a, w_in_b, w_out_b, final_norm):
    batch, seq, d = x_prompt.shape
    n_seq = x_sample.shape[0]
    depth = ffn_norm.shape[0]
    n_pages = page_table.shape[1]
    past = n_pages * PAGE_SIZE
    mem_len = mem_prompt.shape[1]
    assert x_sample.shape[1] == 1 and seq % TM == 0
    ng = len(B_GROUPS)

    yp = x_prompt.reshape(batch * seq, d)
    ys = x_sample.reshape(n_seq, d)
    tab_p = _rope_tables(jnp.arange(seq))
    tab_s = _rope_tables(jnp.full((n_seq,), past, I32))
    w_in_bf = w_ffn_in.astype(BF)
    w_out_bf = w_ffn_out.astype(BF)
    mem_flat = mem_prompt.reshape(batch * mem_len, d)
    kv_pool_t = _feature_major(cache_kv_a)
    kidx_pool_t = jnp.swapaxes(cache_kidx_a, 2, 3)
    states_t = [_feature_major(s) for s in (state_kv_b1, state_kv_b2, state_kv_b3)]
    mem_cache_t = _feature_major(cache_mem_kv)

    kv_a_p, kidx_a_p, kv_a_s, kidx_a_s = [], [], [], []
    b_p = ([], [], [])
    new_states = []
    mem_t = _memproj(mem_flat, mem_norm, w_mem_kv, batch)
    for i in range(depth):
        j = i // 2
        yp = _ffn(yp, ffn_norm[i, 0], w_in_bf, w_out_bf, i, 0, TM)
        ys = _ffn(ys, ffn_norm[i, 0], w_in_bf, w_out_bf, i, 0, n_seq)
        if i % 2 == 0:
            w = w_in_a[j]
            q_al, k_bf, iq_al, ik128, mq_al, kv_t, v_t, ik_t, iw_t = _proj_a_prompt(yp, mix_norm[i], w, tab_p, batch)
            o = _attn_a_prompt(q_al, iq_al, iw_t, k_bf, ik128, v_t, batch, seq)
            om = _memattn_prompt(mq_al, mem_t, i, seq, TM)
            n_o = A_HEADS * HEAD_DIM
            mix_p = ([], [o, om], [w_out_a[j][:n_o], w_out_a[j][n_o:]])
            kv_a_p.append(kv_t)
            kidx_a_p.append(ik_t)

            qblk, kvn, iq_s, ikn, iw_s, mq_blk = _proj_a_sample(ys, mix_norm[i], w, tab_s)
            bias = _idx_sample(page_table, iq_s.reshape(n_seq, IDX_HEADS, IDX_DIM),
                               iw_s[:, :IDX_HEADS].reshape(n_seq, IDX_HEADS, 1),
                               ikn.reshape(n_seq, 1, IDX_DIM), kidx_pool_t, j)
            o_s = _attn_a_sample(page_table, qblk.reshape(n_seq, _SAMPLE_A_ROWS, KV_W),
                                 bias.reshape(n_seq, 1, past + LANES), kvn.reshape(n_seq, 1, 2 * KV_W),
                                 kv_pool_t, j)
            om_s = _memattn_sample(mq_blk.reshape(n_seq, _SAMPLE_ROWS, KV_W), mem_cache_t, i)
            mix_s = ([], [o_s.reshape(n_seq, -1), om_s.reshape(n_seq, -1)],
                     [_expand_out(w_out_a[j][:n_o], _A_Q_SLOT, _SAMPLE_A_ROWS),
                      _expand_out(w_out_a[j][n_o:], _MEM_SLOT, _SAMPLE_ROWS)])
            kv_a_s.append(kvn.reshape(n_seq, 1, 2, A_KV_HEADS, HEAD_DIM))
            kidx_a_s.append(ikn.reshape(n_seq, 1, IDX_DIM))
        else:
            w = w_in_b[j]
            outs = _proj_b_prompt(yp, mix_norm[i], w, tab_p, batch)
            stats_o, stats_m, stats_d = [], [], []
            for gi, (window, dil) in enumerate(B_GROUPS):
                og, mg, dg = _attn_b_prompt(outs[2 * gi], outs[2 * gi + 1], batch, seq, window, dil)
                stats_o.append(og)
                stats_m.append(mg)
                stats_d.append(dg)
                b_p[gi].append(outs[2 * ng + 1 + gi][:, :, seq - min(window, seq):])
            om = _memattn_prompt(outs[2 * ng], mem_t, i, seq, TM)
            n_o = B_HEADS * HEAD_DIM
            mix_p = (stats_o + stats_m + stats_d, [om], [w_out_b[j][:n_o], w_out_b[j][n_o:]])

            souts = _proj_b_sample(ys, mix_norm[i], w, tab_s)
            kvns = [souts[1 + gi].reshape(n_seq, 1, 2 * KV_W) for gi in range(ng)]
            kvcs = [souts[2 + ng + gi] for gi in range(ng)]
            o_s, new_states = _attn_b_sample(souts[0].reshape(n_seq, ng * _SAMPLE_ROWS, KV_W),
                                             states_t, kvns, kvcs, j, new_states)
            om_s = _memattn_sample(souts[1 + ng].reshape(n_seq, _SAMPLE_ROWS, KV_W), mem_cache_t, i)
            mix_s = ([], [o_s.reshape(n_seq, -1), om_s.reshape(n_seq, -1)],
                     [_expand_out(w_out_b[j][:n_o], _B_SLOT, _SAMPLE_ROWS),
                      _expand_out(w_out_b[j][n_o:], _MEM_SLOT, _SAMPLE_ROWS)])
        closing = final_norm if i == depth - 1 else None
        yp = _ffn(yp, ffn_norm[i, 1], w_in_bf, w_out_bf, i, 1, TM, mix_p, closing)
        ys = _ffn(ys, ffn_norm[i, 1], w_in_bf, w_out_bf, i, 1, n_seq, mix_s, closing)

    y_prompt = yp.reshape(batch, seq, d)
    y_sample = ys.reshape(n_seq, 1, d)
    return (y_prompt, y_sample,
            _token_major(jnp.stack(kv_a_p), A_KV_HEADS), jnp.swapaxes(jnp.stack(kidx_a_p), 2, 3),
            _token_major(jnp.stack(b_p[0]), B_HEADS), _token_major(jnp.stack(b_p[1]), B_HEADS),
            _token_major(jnp.stack(b_p[2]), B_HEADS), _token_major(mem_t, MEM_HEADS),
            jnp.stack(kv_a_s), jnp.stack(kidx_a_s),
            _token_major(new_states[0], B_HEADS), _token_major(new_states[1], B_HEADS),
            _token_major(new_states[2], B_HEADS))
```

```python
import functools

import jax
import jax.numpy as jnp
import numpy as np
from jax import lax
from jax.experimental import pallas as pl
from jax.experimental.pallas import tpu as pltpu

F32 = jnp.float32
BF = jnp.bfloat16
I32 = jnp.int32

HEAD_DIM = 64
ROT_DIM = HEAD_DIM // 4
ROT_HALF = ROT_DIM // 2
ROPE_THETA = 500000.0
ATTN_SCALE = HEAD_DIM ** -0.5
A_HEADS = 12
A_KV_HEADS = 4
A_GROUP = A_HEADS // A_KV_HEADS
IDX_HEADS = 8
IDX_DIM = 64
IDX_SCALE = (IDX_HEADS * IDX_DIM) ** -0.5
TOPK_MAX = 256
B_GROUPS = ((128, 1), (512, 4), (2048, 16))
B_HEADS = 4
MEM_HEADS = 4
EPS = 1e-6
PAGE_SIZE = 128

LANES = 128
SUBLANES = 8
PAIR = 2 * HEAD_DIM
KV_W = A_KV_HEADS * HEAD_DIM
VMEM_LIMIT_V7X = 56 * 2 ** 20
NEG_BIG = -1e30
M_INIT = -3e38
INT_MAX = 2 ** 31 - 1
KEY_NEG_INF = -2139095041
F32_LOWEST = float(np.finfo(np.float32).min)
MAX_SEARCH_PASSES = 32

TM = 512
TQ = 1024
KC = 512
CLS = 128
PAGES_PER_STEP = 16
ONES_ROWS = 16


def _cp(n_axes):
    return pltpu.CompilerParams(dimension_semantics=("arbitrary",) * n_axes,
                                vmem_limit_bytes=VMEM_LIMIT_V7X)


def _dot(a, b):
    return jnp.dot(a, b, preferred_element_type=F32)


def _dot_nt(a, b):
    return lax.dot_general(a, b, (((1,), (1,)), ((), ())), preferred_element_type=F32)


def _rms(x, g):
    return x * lax.rsqrt(jnp.mean(x * x, axis=-1, keepdims=True) + EPS) * g


def _key_to_float(k):
    return pltpu.bitcast(k ^ ((k >> 31) & 0x7FFFFFFF), F32)


def _resident(shape):
    return pl.BlockSpec(shape, lambda *_: (0,) * len(shape), pipeline_mode=pl.Buffered(1))


def _mix_residual(x, refs, n_x, combine):
    pos = 0
    xs = []
    if combine:
        og, mg, dg = refs[0:3], refs[3:6], refs[6:9]
        pos = 9
        ms = [r[...] for r in mg]
        mmax = jnp.maximum(jnp.maximum(ms[0], ms[1]), ms[2])
        wgt = [dg[k][...] * jnp.exp(ms[k] - mmax) for k in range(3)]
        num = wgt[0] * og[0][...] + wgt[1] * og[1][...] + wgt[2] * og[2][...]
        merged = num / (wgt[0] + wgt[1] + wgt[2])
        xs.append(jnp.concatenate([merged[c] for c in range(merged.shape[0])], axis=1))
    xs += [refs[pos + k][...] for k in range(n_x)]
    w_refs = refs[pos + n_x:pos + n_x + len(xs)]
    for a, w_ref in zip(xs, w_refs):
        x = x + _dot(a.astype(BF), w_ref[...])
    return x


def _ffn_body(*refs, d_ff, fc, n_x, combine, final):
    y_ref, g_ref, win_ref, wout_ref = refs[:4]
    mix_refs = refs[4:len(refs) - (2 if final else 1)]
    o_ref = refs[-1]
    x = y_ref[...]
    if mix_refs:
        x = _mix_residual(x, mix_refs, n_x, combine)
    xn = _rms(x, g_ref[...]).astype(BF)
    acc = jnp.zeros(x.shape, F32)
    for c in range(d_ff // fc):
        a = _dot(xn, win_ref[:, c * fc:(c + 1) * fc])
        u = _dot(xn, win_ref[:, d_ff + c * fc:d_ff + (c + 1) * fc])
        h = (a * jax.nn.sigmoid(a) * u).astype(BF)
        acc = acc + _dot(h, wout_ref[c * fc:(c + 1) * fc, :])
    out = x + 0.5 * acc
    if final:
        out = _rms(out, refs[-2][...])
    o_ref[...] = out


def _ffn(y, g, w_in_all, w_out_all, layer, half, tm, mix=None, final_g=None):
    m, d = y.shape
    d_ff = w_out_all.shape[2]
    fc = 256 if d_ff % 256 == 0 else LANES
    assert d_ff % fc == 0 and m % tm == 0
    row = lambda width: pl.BlockSpec((tm, width), lambda i: (i, 0))
    vec = pl.BlockSpec((1, d), lambda i: (0, 0))
    in_specs = [row(d), vec,
                pl.BlockSpec((None, None, d, 2 * d_ff), lambda i: (layer, half, 0, 0), pipeline_mode=pl.Buffered(1)),
                pl.BlockSpec((None, None, d_ff, d), lambda i: (layer, half, 0, 0), pipeline_mode=pl.Buffered(1))]
    args = [y, g.reshape(1, d), w_in_all, w_out_all]
    stats, xs, ws = mix if mix else ([], [], [])
    in_specs += [pl.BlockSpec((a.shape[0], tm, LANES), lambda i: (0, i, 0)) for a in stats]
    in_specs += [row(x.shape[1]) for x in xs] + [_resident(w.shape) for w in ws]
    args += list(stats) + list(xs) + [w.astype(BF) for w in ws]
    if final_g is not None:
        in_specs.append(vec)
        args.append(final_g.reshape(1, d))
    return pl.pallas_call(
        functools.partial(_ffn_body, d_ff=d_ff, fc=fc, n_x=len(xs), combine=bool(stats),
                          final=final_g is not None),
        grid=(m // tm,),
        in_specs=in_specs,
        out_specs=row(d),
        out_shape=jax.ShapeDtypeStruct((m, d), F32),
        compiler_params=_cp(1),
        name="ffn",
    )(*args)


def _rope_tables(pos):
    inv = jnp.power(ROPE_THETA, -jnp.arange(ROT_HALF, dtype=F32) / ROT_HALF)
    ang = pos.astype(F32)[:, None] * inv[None, :]
    cos, sin = jnp.cos(ang), jnp.sin(ang)
    t = pos.shape[0]
    ones = jnp.ones((t, HEAD_DIM - ROT_DIM), F32)
    zeros = jnp.zeros((t, HEAD_DIM - ROT_DIM), F32)
    zh = jnp.zeros((t, ROT_HALF), F32)
    cos64 = jnp.concatenate([cos, cos, ones], axis=1)
    sa64 = jnp.concatenate([-sin, zh, zeros], axis=1)
    sb64 = jnp.concatenate([zh, sin, zeros], axis=1)
    row = tuple(jnp.tile(a, (1, LANES // HEAD_DIM)) for a in (cos64, sa64, sb64))
    return row + (cos.T, sin.T)


def _proj_body(*refs, segs, nts, n_in):
    y_ref, g_ref, cos_ref, sa_ref, sb_ref, cost_ref, sint_ref = refs[:7]
    w_refs = refs[7:7 + len(segs)]
    wt_refs = refs[7 + len(segs):n_in]
    outs = refs[n_in:]
    xn = _rms(y_ref[...], g_ref[...]).astype(BF)
    for w_ref, (lo, hi, rope, scale, writes) in zip(w_refs, segs):
        p = _dot(xn, w_ref[...])
        if rope:
            cos, sa, sb = cos_ref[...], sa_ref[...], sb_ref[...]
            chunks = []
            for c in range((hi - lo) // LANES):
                pc = p[:, c * LANES:(c + 1) * LANES]
                chunks.append(pc * cos + pltpu.roll(pc, LANES - ROT_HALF, 1) * sa
                              + pltpu.roll(pc, ROT_HALF, 1) * sb)
            p = chunks[0] if len(chunks) == 1 else jnp.concatenate(chunks, axis=1)
        if scale != 1.0:
            p = p * scale
        for oi, slo, shi, dlo in writes:
            if len(outs[oi].shape) == 3:
                for c in range((shi - slo) // LANES):
                    outs[oi][dlo // LANES + c] = p[:, slo + c * LANES:slo + (c + 1) * LANES].astype(outs[oi].dtype)
            else:
                outs[oi][:, dlo:dlo + (shi - slo)] = p[:, slo:shi].astype(outs[oi].dtype)
    for t, (rope_rows, scale, writes) in enumerate(nts):
        r = _dot_nt(wt_refs[t][...], xn)
        if rope_rows:
            ct, st = cost_ref[...], sint_ref[...]
            pieces = []
            for h0 in range(0, rope_rows, HEAD_DIM):
                x1, x2 = r[h0:h0 + ROT_HALF], r[h0 + ROT_HALF:h0 + ROT_DIM]
                pieces += [x1 * ct - x2 * st, x2 * ct + x1 * st, r[h0 + ROT_DIM:h0 + HEAD_DIM]]
            if rope_rows < r.shape[0]:
                pieces.append(r[rope_rows:])
            r = jnp.concatenate(pieces, axis=0)
        if scale != 1.0:
            r = r * scale
        for oi, slo, shi, dlo in writes:
            outs[oi][dlo:dlo + (shi - slo), :] = r[slo:shi].astype(outs[oi].dtype)


def _proj(y, g, tables, n_seqs, w, wts, segs, nts, out_defs, tm):
    m, d = y.shape
    assert m % tm == 0
    per_seq = m // n_seqs // tm if tables[0].shape[0] != tm else 1
    n_in = 7 + len(w) + len(wts)
    pos_row = lambda i: (i % per_seq, 0)
    pos_col = lambda i: (0, i % per_seq)
    in_specs = [pl.BlockSpec((tm, d), lambda i: (i, 0)),
                pl.BlockSpec((1, d), lambda i: (0, 0)),
                pl.BlockSpec((tm, LANES), pos_row),
                pl.BlockSpec((tm, LANES), pos_row),
                pl.BlockSpec((tm, LANES), pos_row),
                pl.BlockSpec((ROT_HALF, tm), pos_col),
                pl.BlockSpec((ROT_HALF, tm), pos_col)]
    in_specs += [_resident(a.shape) for a in w] + [_resident(wt.shape) for wt in wts]
    out_specs, out_shapes = [], []
    for kind, size, dtype in out_defs:
        if kind == "row":
            out_specs.append(pl.BlockSpec((tm, size), lambda i: (i, 0)))
            out_shapes.append(jax.ShapeDtypeStruct((m, size), dtype))
        elif kind == "chunks":
            out_specs.append(pl.BlockSpec((size // LANES, tm, LANES), lambda i: (0, i, 0)))
            out_shapes.append(jax.ShapeDtypeStruct((size // LANES, m, LANES), dtype))
        elif kind == "col":
            out_specs.append(pl.BlockSpec((size, tm), lambda i: (0, i)))
            out_shapes.append(jax.ShapeDtypeStruct((size, m), dtype))
        else:
            out_specs.append(pl.BlockSpec((None, size, tm), lambda i: (i // per_seq, 0, i % per_seq)))
            out_shapes.append(jax.ShapeDtypeStruct((n_seqs, size, m // n_seqs), dtype))
    return pl.pallas_call(
        functools.partial(_proj_body, segs=tuple(segs), nts=tuple(nts), n_in=n_in),
        grid=(m // tm,),
        in_specs=in_specs,
        out_specs=out_specs,
        out_shape=out_shapes,
        compiler_params=_cp(1),
        name="proj",
    )(y, g.reshape(1, d), *tables, *w, *wts)


def _heads(w, lo, n_heads):
    return w[:, lo:lo + n_heads * HEAD_DIM].reshape(w.shape[0], n_heads, HEAD_DIM)


def _aligned(wh, halves):
    d, h, _ = wh.shape
    z = jnp.zeros_like(wh)
    sel = jnp.asarray(halves, I32).reshape(1, h, 1)
    lo = jnp.where(sel == 0, wh, z)
    hi = jnp.where(sel == 1, wh, z)
    return jnp.concatenate([lo, hi], axis=2).reshape(d, h * PAIR)


def _blocked(wh, slots, n_rows):
    d, h, _ = wh.shape
    onehot = jnp.asarray(np.eye(A_KV_HEADS, dtype=np.float32)[np.asarray(slots)], wh.dtype)
    blk = (wh[:, :, None, :] * onehot[None, :, :, None]).reshape(d, h * KV_W)
    return jnp.pad(blk, ((0, 0), (0, (n_rows - h) * KV_W)))


def _expand_out(w_rows, slots, n_rows):
    h = len(slots)
    d = w_rows.shape[1]
    wh = w_rows.reshape(h, HEAD_DIM, d)
    onehot = jnp.asarray(np.eye(A_KV_HEADS, dtype=np.float32)[np.asarray(slots)], w_rows.dtype)
    blk = (wh[:, None, :, :] * onehot[:, :, None, None]).reshape(h * KV_W, d)
    return jnp.pad(blk, ((0, (n_rows - h) * KV_W), (0, 0)))


_A_OFF = np.cumsum([0, A_HEADS * HEAD_DIM, KV_W, KV_W, IDX_HEADS * IDX_DIM, IDX_DIM, IDX_HEADS]).tolist()
_A_Q_HALF = [(h // A_GROUP) % 2 for h in range(A_HEADS)]
_A_Q_SLOT = [h // A_GROUP for h in range(A_HEADS)]
_MEM_HALF = [h % 2 for h in range(MEM_HEADS)]
_MEM_SLOT = list(range(MEM_HEADS))
_B_HALF = [h % 2 for h in range(B_HEADS)]
_B_SLOT = list(range(B_HEADS))
_SAMPLE_A_ROWS = 16
_SAMPLE_ROWS = 8


def _segments(items):
    segs, cols, lo = [], [], 0
    for wc, rope, scale, writes in items:
        hi = lo + wc.shape[1]
        segs.append((lo, hi, rope, scale, tuple(writes)))
        cols.append(wc.astype(BF))
        lo = hi
    return cols, segs


def _proj_a_prompt(y, g, w, tables, batch):
    o = _A_OFF
    zpad = jnp.zeros((w.shape[0], LANES - IDX_DIM), w.dtype)
    items = [
        (_aligned(_heads(w, o[0], A_HEADS), _A_Q_HALF), True, ATTN_SCALE, [(0, 0, A_HEADS * PAIR, 0)]),
        (w[:, o[1]:o[2]], True, 1.0, [(1, 0, KV_W, 0)]),
        (_aligned(_heads(w, o[3], IDX_HEADS), [0] * IDX_HEADS), True, 1.0, [(2, 0, IDX_HEADS * PAIR, 0)]),
        (jnp.concatenate([w[:, o[4]:o[5]], zpad], axis=1), True, 1.0, [(3, 0, LANES, 0)]),
        (_aligned(_heads(w, o[6], MEM_HEADS), _MEM_HALF), False, ATTN_SCALE, [(4, 0, MEM_HEADS * PAIR, 0)]),
    ]
    wcat, segs = _segments(items)
    wts = [w[:, o[1]:o[3]].T.astype(BF), w[:, o[4]:o[5]].T.astype(BF), w[:, o[5]:o[6]].T.astype(BF)]
    nts = [(KV_W, 1.0, ((5, 0, 2 * KV_W, 0), (6, KV_W, 2 * KV_W, 0))),
           (IDX_DIM, 1.0, ((7, 0, IDX_DIM, 0),)),
           (0, IDX_SCALE, ((8, 0, IDX_HEADS, 0),))]
    out_defs = [("row", A_HEADS * PAIR, BF), ("row", KV_W, BF), ("row", IDX_HEADS * PAIR, BF),
                ("row", LANES, BF), ("row", MEM_HEADS * PAIR, BF), ("bcol", 2 * KV_W, F32),
                ("col", KV_W, BF), ("bcol", IDX_DIM, F32), ("col", IDX_HEADS, F32)]
    return _proj(y, g, tables, batch, wcat, wts, segs, nts, out_defs, TM)


def _proj_a_sample(y, g, w, tables):
    o = _A_OFF
    zpad = jnp.zeros((w.shape[0], LANES - IDX_DIM), w.dtype)
    wpad = jnp.zeros((w.shape[0], LANES - IDX_HEADS), w.dtype)
    items = [
        (_blocked(_heads(w, o[0], A_HEADS), _A_Q_SLOT, _SAMPLE_A_ROWS), True, ATTN_SCALE,
         [(0, 0, _SAMPLE_A_ROWS * KV_W, 0)]),
        (w[:, o[1]:o[2]], True, 1.0, [(1, 0, KV_W, 0)]),
        (w[:, o[2]:o[3]], False, 1.0, [(1, 0, KV_W, KV_W)]),
        (w[:, o[3]:o[4]], True, 1.0, [(2, 0, IDX_HEADS * IDX_DIM, 0)]),
        (jnp.concatenate([w[:, o[4]:o[5]], zpad], axis=1), True, 1.0, [(3, 0, IDX_DIM, 0)]),
        (jnp.concatenate([w[:, o[5]:o[6]], wpad], axis=1), False, IDX_SCALE, [(4, 0, LANES, 0)]),
        (_blocked(_heads(w, o[6], MEM_HEADS), _MEM_SLOT, _SAMPLE_ROWS), False, ATTN_SCALE,
         [(5, 0, _SAMPLE_ROWS * KV_W, 0)]),
    ]
    wcat, segs = _segments(items)
    out_defs = [("row", _SAMPLE_A_ROWS * KV_W, BF), ("row", 2 * KV_W, F32), ("row", IDX_HEADS * IDX_DIM, BF),
                ("row", IDX_DIM, F32), ("row", LANES, F32), ("row", _SAMPLE_ROWS * KV_W, BF)]
    return _proj(y, g, tables, 1, wcat, [], segs, [], out_defs, y.shape[0])


def _b_off(g, t):
    return (g * 3 + t) * B_HEADS * HEAD_DIM


def _proj_b_prompt(y, g, w, tables, batch):
    ng = len(B_GROUPS)
    items, out_defs, wts, nts = [], [], [], []
    for gi in range(ng):
        items += [
            (_aligned(_heads(w, _b_off(gi, 0), B_HEADS), _B_HALF), True, ATTN_SCALE,
             [(2 * gi, 0, B_HEADS * PAIR, 0)]),
            (w[:, _b_off(gi, 1):_b_off(gi, 1) + KV_W], True, 1.0, [(2 * gi + 1, 0, KV_W, 0)]),
            (w[:, _b_off(gi, 2):_b_off(gi, 2) + KV_W], False, 1.0, [(2 * gi + 1, 0, KV_W, KV_W)]),
        ]
        out_defs += [("chunks", B_HEADS * PAIR, F32), ("chunks", 2 * KV_W, F32)]
    items.append((_aligned(_heads(w, _b_off(ng, 0), MEM_HEADS), _MEM_HALF), False, ATTN_SCALE,
                  [(2 * ng, 0, MEM_HEADS * PAIR, 0)]))
    out_defs.append(("row", MEM_HEADS * PAIR, BF))
    for gi in range(ng):
        wts.append(w[:, _b_off(gi, 1):_b_off(gi, 1) + 2 * KV_W].T.astype(BF))
        nts.append((KV_W, 1.0, ((2 * ng + 1 + gi, 0, 2 * KV_W, 0),)))
        out_defs.append(("bcol", 2 * KV_W, F32))
    wcat, segs = _segments(items)
    return _proj(y, g, tables, batch, wcat, wts, segs, nts, out_defs, TM)


def _proj_b_sample(y, g, w, tables):
    ng = len(B_GROUPS)
    items, out_defs, wts, nts = [], [("row", ng * _SAMPLE_ROWS * KV_W, BF)], [], []
    for gi in range(ng):
        items += [
            (_blocked(_heads(w, _b_off(gi, 0), B_HEADS), _B_SLOT, _SAMPLE_ROWS), True, ATTN_SCALE,
             [(0, 0, _SAMPLE_ROWS * KV_W, gi * _SAMPLE_ROWS * KV_W)]),
            (w[:, _b_off(gi, 1):_b_off(gi, 1) + KV_W], True, 1.0, [(1 + gi, 0, KV_W, 0)]),
            (w[:, _b_off(gi, 2):_b_off(gi, 2) + KV_W], False, 1.0, [(1 + gi, 0, KV_W, KV_W)]),
        ]
        out_defs.append(("row", 2 * KV_W, F32))
    items.append((_blocked(_heads(w, _b_off(ng, 0), MEM_HEADS), _MEM_SLOT, _SAMPLE_ROWS), False, ATTN_SCALE,
                  [(1 + ng, 0, _SAMPLE_ROWS * KV_W, 0)]))
    out_defs.append(("row", _SAMPLE_ROWS * KV_W, BF))
    for gi in range(ng):
        wts.append(w[:, _b_off(gi, 1):_b_off(gi, 1) + 2 * KV_W].T.astype(BF))
        nts.append((KV_W, 1.0, ((2 + ng + gi, 0, 2 * KV_W, 0),)))
        out_defs.append(("col", 2 * KV_W, F32))
    wcat, segs = _segments(items)
    return _proj(y, g, tables, 1, wcat, wts, segs, nts, out_defs, y.shape[0])


def _memproj_body(y_ref, g_ref, wt_ref, o_ref):
    xn = _rms(y_ref[...], g_ref[...]).astype(BF)
    o_ref[...] = _dot_nt(wt_ref[...], xn)


def _memproj(mem_flat, g_all, w_all, batch):
    m, d = mem_flat.shape
    mem_len = m // batch
    depth, _, n = w_all.shape
    return pl.pallas_call(
        _memproj_body,
        grid=(depth, batch),
        in_specs=[pl.BlockSpec((mem_len, d), lambda l, b: (b, 0)),
                  pl.BlockSpec((None, 1, d), lambda l, b: (l, 0, 0)),
                  pl.BlockSpec((None, n, d), lambda l, b: (l, 0, 0))],
        out_specs=pl.BlockSpec((None, None, n, mem_len), lambda l, b: (l, b, 0, 0)),
        out_shape=jax.ShapeDtypeStruct((depth, batch, n, mem_len), F32),
        compiler_params=_cp(2),
        name="memproj",
    )(mem_flat, g_all.reshape(depth, 1, d), jnp.swapaxes(w_all, 1, 2).astype(BF))


def _select_bias(count, read, write, n_chunks, rows, k_sel, red_shape, key_axis):
    lo0 = KEY_NEG_INF + 1
    c_valid = count(F32_LOWEST)
    c_zero = count(0.0)
    pos = c_zero >= k_sel
    few = c_valid <= k_sel
    lo = jnp.where(pos, 0, lo0).astype(I32)
    hi = jnp.where(pos, INT_MAX, 0).astype(I32)
    c_lo = jnp.where(pos, c_zero, c_valid)
    c_hi = jnp.where(pos, 0, c_zero)
    done = jnp.where(jnp.logical_or(few, c_zero == k_sel), 1, 0).astype(I32)
    thr = jnp.where(few, lo0, 0).astype(I32)
    full = jnp.full(red_shape, k_sel, I32)

    def cond(st):
        return jnp.logical_and(st[0] < MAX_SEARCH_PASSES, st[1] == 0)

    def body(st):
        it, _, lo, hi, c_lo, c_hi, thr, need, n_ge, done = st
        cand = lo + jnp.maximum((hi - lo) >> 1, 1)
        c = count(_key_to_float(cand))
        hit = c == k_sel
        up = c > k_sel
        lo_n = jnp.where(up, cand, lo)
        hi_n = jnp.where(up, hi, cand)
        c_lo_n = jnp.where(up, c, c_lo)
        c_hi_n = jnp.where(up, c_hi, c)
        newly = jnp.logical_and(done == 0, jnp.logical_or(hit, hi_n - lo_n == 1))
        thr_n = jnp.where(newly, jnp.where(hit, cand, lo_n), thr)
        need_n = jnp.where(jnp.logical_and(newly, jnp.logical_not(hit)), k_sel - c_hi_n, need)
        n_ge_n = jnp.where(jnp.logical_and(newly, jnp.logical_not(hit)), c_lo_n, n_ge)
        done_n = jnp.where(newly, 1, done)
        return (it + 1, jnp.min(done_n), lo_n, hi_n, c_lo_n, c_hi_n, thr_n, need_n, n_ge_n, done_n)

    st = lax.while_loop(cond, body, (jnp.int32(0), jnp.min(done), lo, hi, c_lo, c_hi, thr, full, full, done))
    thr, need_i, n_ge = _key_to_float(st[6]), st[7], st[8]
    has_tie = jnp.max(n_ge) > k_sel

    @pl.when(jnp.logical_not(has_tie))
    def _():
        def wbody(c, _):
            x = read(c)
            sel = jnp.logical_and(x >= thr, x > -jnp.inf)
            write(c, jnp.where(sel, 0.0, NEG_BIG).astype(F32))
            return 0
        lax.fori_loop(0, n_chunks, wbody, 0)

    @pl.when(has_tie)
    def _():
        need = need_i.astype(F32)
        r = lax.broadcasted_iota(I32, (rows, rows), 0)
        cidx = lax.broadcasted_iota(I32, (rows, rows), 1)
        if key_axis == 0:
            tri = jnp.where(cidx < r, 1.0, 0.0).astype(BF)
        else:
            tri = jnp.where(r < cidx, 1.0, 0.0).astype(BF)

        def wbody(c, seen):
            x = read(c)
            eq = jnp.logical_and(x == thr, x > -jnp.inf)
            eqf = jnp.where(eq, 1.0, 0.0).astype(BF)
            rank = (_dot(tri, eqf) if key_axis == 0 else _dot(eqf, tri)) + seen
            sel = jnp.logical_and(jnp.logical_or(x > thr, jnp.logical_and(eq, rank < need)), x > -jnp.inf)
            write(c, jnp.where(sel, 0.0, NEG_BIG).astype(F32))
            return seen + jnp.sum(eqf.astype(F32), axis=key_axis, keepdims=True)
        lax.fori_loop(0, n_chunks, wbody, jnp.zeros(red_shape, F32))


def _attn_a_body(q_ref, iq_ref, iw_ref, k_ref, ik_ref, vt_ref, o_ref, sc_ref, acc_ref, *, tq, kc, k_sel):
    qi = pl.program_id(1)
    per_q = tq // kc
    n_full = qi * per_q
    n_chunks = n_full + per_q
    iw = iw_ref[...]

    def chunk_rows(c):
        return pl.ds(pl.multiple_of(c * kc, kc), kc)

    def idx_chunk(c, j):
        l0 = 0 if j is None else j * kc
        ikc = ik_ref[chunk_rows(c), :]
        acc = jnp.zeros((kc, tq - l0), F32)
        for h in range(IDX_HEADS):
            s = _dot_nt(ikc, iq_ref[l0:, h * PAIR:(h + 1) * PAIR])
            acc = acc + jnp.maximum(s, 0.0) * iw[h:h + 1, l0:]
        if j is not None:
            kpos = lax.broadcasted_iota(I32, (kc, tq - l0), 0)
            qpos = lax.broadcasted_iota(I32, (kc, tq - l0), 1)
            acc = jnp.where(kpos <= qpos, acc, -jnp.inf)
            if l0:
                sc_ref[chunk_rows(c), :l0] = jnp.full((kc, l0), -jnp.inf, F32)
        sc_ref[chunk_rows(c), l0:] = acc

    def full_chunk(c, _):
        idx_chunk(c, None)
        return 0
    lax.fori_loop(0, n_full, full_chunk, 0)
    for j in range(per_q):
        idx_chunk(n_full + j, j)

    def read(c):
        return sc_ref[chunk_rows(c), :]

    def write(c, v):
        sc_ref[chunk_rows(c), :] = v

    def count(t):
        def step(x, cnts, l0):
            tt = t if isinstance(t, float) else t[:, l0:]
            cnts = list(cnts)
            for r in range(kc // SUBLANES):
                a = r % len(cnts)
                cnts[a] = jnp.where(x[r * SUBLANES:(r + 1) * SUBLANES] >= tt, cnts[a] + 1, cnts[a])
            return tuple(cnts)

        def zeros(l0):
            return (jnp.zeros((SUBLANES, tq - l0), I32),) * 4
        parts = lax.fori_loop(0, n_full + 1, lambda c, cn: step(read(c), cn, 0), zeros(0))
        total = parts[0] + parts[1] + parts[2] + parts[3]
        for j in range(1, per_q):
            l0 = j * kc
            pj = step(sc_ref[chunk_rows(n_full + j), l0:], zeros(l0), l0)
            total = total + jnp.concatenate([jnp.zeros((SUBLANES, l0), I32), pj[0] + pj[1] + pj[2] + pj[3]], axis=1)
        return jnp.sum(total, axis=0, keepdims=True)

    _select_bias(count, read, write, n_chunks, kc, k_sel, (1, tq), 0)

    acc_ref[...] = jnp.zeros(acc_ref.shape, F32)
    ones_rows = jnp.ones((ONES_ROWS, kc), BF)
    pad_rows = jnp.zeros((2 * SUBLANES - A_HEADS, tq), F32)

    def att_chunk(c, carry, l0):
        m_all, l_all = carry
        rows_c = chunk_rows(c)
        bias = sc_ref[rows_c, l0:]
        kcs = [k_ref[rows_c, pair * PAIR:(pair + 1) * PAIR] for pair in range(A_KV_HEADS // 2)]
        vts = [jnp.concatenate([vt_ref[n * HEAD_DIM:(n + 1) * HEAD_DIM, rows_c], ones_rows], axis=0)
               for n in range(A_KV_HEADS)]
        m_rows, l_rows = [], []
        for h in range(A_HEADS):
            n = h // A_GROUP
            s = _dot_nt(kcs[n // 2], q_ref[l0:, h * PAIR:(h + 1) * PAIR]) + bias
            m_old = m_all[h:h + 1, l0:]
            m_new = jnp.maximum(m_old, jnp.max(s, axis=0, keepdims=True))
            alpha = jnp.exp(m_old - m_new)
            p = jnp.exp(s - m_new).astype(BF)
            res = _dot(vts[n], p)
            rows = slice(h * HEAD_DIM, (h + 1) * HEAD_DIM)
            l_new = alpha * l_all[h:h + 1, l0:] + res[HEAD_DIM:HEAD_DIM + 1]
            m_rows.append(m_new if l0 == 0 else jnp.concatenate([m_all[h:h + 1, :l0], m_new], axis=1))
            l_rows.append(l_new if l0 == 0 else jnp.concatenate([l_all[h:h + 1, :l0], l_new], axis=1))
            acc_ref[rows, l0:] = alpha * acc_ref[rows, l0:] + res[:HEAD_DIM]
        return (jnp.concatenate(m_rows + [pad_rows], axis=0), jnp.concatenate(l_rows + [pad_rows], axis=0))

    carry = (jnp.full((2 * SUBLANES, tq), M_INIT, F32), jnp.zeros((2 * SUBLANES, tq), F32))
    carry = lax.fori_loop(0, n_full + 1, lambda c, cr: att_chunk(c, cr, 0), carry)
    for j in range(1, per_q):
        carry = att_chunk(n_full + j, carry, j * kc)
    l_all = carry[1]

    for hp in range(A_HEADS // 2):
        outs = [acc_ref[h * HEAD_DIM:(h + 1) * HEAD_DIM, :] / l_all[h:h + 1] for h in (2 * hp, 2 * hp + 1)]
        o_ref[:, hp * PAIR:(hp + 1) * PAIR] = jnp.concatenate(outs, axis=0).T.astype(o_ref.dtype)


def _attn_a_prompt(q_al, iq_al, iw_t, k_bf, ik128, v_t, batch, seq):
    tq = min(TQ, seq)
    kc = min(KC, tq)
    nq = seq // tq
    m = batch * seq
    k_sel = min(TOPK_MAX, seq // 4)
    return pl.pallas_call(
        functools.partial(_attn_a_body, tq=tq, kc=kc, k_sel=k_sel),
        grid=(batch, nq),
        in_specs=[pl.BlockSpec((tq, A_HEADS * PAIR), lambda b, i: (b * nq + i, 0)),
                  pl.BlockSpec((tq, IDX_HEADS * PAIR), lambda b, i: (b * nq + i, 0)),
                  pl.BlockSpec((IDX_HEADS, tq), lambda b, i: (0, b * nq + i)),
                  pl.BlockSpec((seq, KV_W), lambda b, i: (b, 0)),
                  pl.BlockSpec((seq, LANES), lambda b, i: (b, 0)),
                  pl.BlockSpec((KV_W, seq), lambda b, i: (0, b))],
        out_specs=pl.BlockSpec((tq, A_HEADS * HEAD_DIM), lambda b, i: (b * nq + i, 0)),
        out_shape=jax.ShapeDtypeStruct((m, A_HEADS * HEAD_DIM), BF),
        scratch_shapes=[pltpu.VMEM((seq, tq), F32),
                        pltpu.VMEM((A_HEADS * HEAD_DIM, tq), F32)],
        compiler_params=_cp(2),
        name="attn_a_prompt",
    )(q_al, iq_al, iw_t, k_bf, ik128, v_t)


def _attn_b_body(q_ref, kvc_ref, kvp_ref, o_ref, m_ref, d_ref, *, dil, span):
    i = pl.program_id(1)
    qrow = lax.broadcasted_iota(I32, (CLS, 2 * CLS), 0)
    kcol = lax.broadcasted_iota(I32, (CLS, 2 * CLS), 1)
    rel = qrow + CLS - kcol
    valid = jnp.logical_and(jnp.logical_and(rel >= 0, rel <= span), jnp.logical_or(kcol >= CLS, i > 0))
    n_pairs = B_HEADS // 2
    valid4 = jnp.concatenate([valid] * B_HEADS, axis=0)
    slot = lax.broadcasted_iota(I32, (CLS, KV_W), 1) >> (HEAD_DIM.bit_length() - 1)
    zeros = jnp.zeros((CLS, PAIR), BF)

    def one_class(rows):
        def cat(base):
            return jnp.concatenate(
                [jnp.concatenate([kvp_ref[base + p, rows, :], kvc_ref[base + p, rows, :]], axis=0)
                 for p in range(n_pairs)], axis=1).astype(BF)
        kcat, vcat = cat(0), cat(n_pairs)
        qrows = []
        for h in range(B_HEADS):
            qh = q_ref[h, rows, :].astype(BF)
            qrows.append(jnp.concatenate([qh, zeros] if h // 2 == 0 else [zeros, qh], axis=1))
        s = jnp.where(valid4, _dot_nt(jnp.concatenate(qrows, axis=0), kcat), -jnp.inf)
        mx = jnp.max(s, axis=1, keepdims=True)
        e = jnp.exp(s - mx)
        den = jnp.sum(e, axis=1, keepdims=True)
        o = _dot(e.astype(BF), vcat) / den
        o_nat = jnp.zeros((CLS, KV_W), F32)
        m_nat = jnp.zeros((CLS, KV_W), F32)
        d_nat = jnp.zeros((CLS, KV_W), F32)
        for h in range(B_HEADS):
            blk = slice(h * CLS, (h + 1) * CLS)
            o_nat = jnp.where(slot == h, o[blk], o_nat)
            m_nat = jnp.where(slot == h, mx[blk], m_nat)
            d_nat = jnp.where(slot == h, den[blk], d_nat)
        for p in range(n_pairs):
            o_ref[p, rows, :] = o_nat[:, p * PAIR:(p + 1) * PAIR]
            m_ref[p, rows, :] = m_nat[:, p * PAIR:(p + 1) * PAIR]
            d_ref[p, rows, :] = d_nat[:, p * PAIR:(p + 1) * PAIR]

    if dil == 1:
        one_class(pl.ds(0, CLS))
    else:
        def body(r, _):
            one_class(pl.ds(r, CLS, stride=dil))
            return 0
        lax.fori_loop(0, dil, body, 0)


def _attn_b_prompt(q_al, kv, batch, seq, window, dil):
    rows = CLS * dil
    span = window // dil
    assert span <= CLS and seq % rows == 0
    nr = seq // rows
    m = batch * seq
    n_pairs = B_HEADS // 2
    stat = jax.ShapeDtypeStruct((n_pairs, m, PAIR), F32)
    ospec = pl.BlockSpec((n_pairs, rows, PAIR), lambda b, i: (0, b * nr + i, 0))
    return pl.pallas_call(
        functools.partial(_attn_b_body, dil=dil, span=span),
        grid=(batch, nr),
        in_specs=[pl.BlockSpec((B_HEADS, rows, PAIR), lambda b, i: (0, b * nr + i, 0)),
                  pl.BlockSpec((2 * n_pairs, rows, PAIR), lambda b, i: (0, b * nr + i, 0)),
                  pl.BlockSpec((2 * n_pairs, rows, PAIR), lambda b, i: (0, b * nr + jnp.maximum(i - 1, 0), 0))],
        out_specs=[ospec, ospec, ospec],
        out_shape=[stat, stat, stat],
        compiler_params=_cp(2),
        name="attn_b_prompt",
    )(q_al, kv, kv)


def _memattn_body(mq_ref, mem_ref, o_ref):
    mq = mq_ref[...]
    mem = mem_ref[...].astype(BF)
    lo_half = lax.broadcasted_iota(I32, (mq.shape[0], PAIR), 1) < HEAD_DIM
    outs = []
    for pair in range(MEM_HEADS // 2):
        mk = mem[pair * PAIR:(pair + 1) * PAIR]
        mv = mem[KV_W + pair * PAIR:KV_W + (pair + 1) * PAIR]
        res = []
        for h in (2 * pair, 2 * pair + 1):
            s = _dot(mq[:, h * PAIR:(h + 1) * PAIR], mk)
            e = jnp.exp(s - jnp.max(s, axis=1, keepdims=True))
            p = e / jnp.sum(e, axis=1, keepdims=True)
            res.append(_dot_nt(p.astype(BF), mv))
        outs.append(jnp.where(lo_half, res[0], res[1]))
    o_ref[...] = jnp.concatenate(outs, axis=1).astype(o_ref.dtype)


def _memattn_prompt(mq_al, mem_t, layer, seq, tm):
    m = mq_al.shape[0]
    mem_len = mem_t.shape[3]
    tiles_per_seq = seq // tm
    return pl.pallas_call(
        _memattn_body,
        grid=(m // tm,),
        in_specs=[pl.BlockSpec((tm, MEM_HEADS * PAIR), lambda i: (i, 0)),
                  pl.BlockSpec((None, None, 2 * KV_W, mem_len), lambda i: (layer, i // tiles_per_seq, 0, 0))],
        out_specs=pl.BlockSpec((tm, KV_W), lambda i: (i, 0)),
        out_shape=jax.ShapeDtypeStruct((m, KV_W), BF),
        compiler_params=_cp(1),
        name="memattn_prompt",
    )(mq_al, mem_t)


def _page_copies(pool_ref, buf_ref, sem_ref, pt_ref, layer, b, first_page, n_pages, slot):
    return [pltpu.make_async_copy(pool_ref.at[layer, pt_ref[b, first_page + p]],
                                  buf_ref.at[slot, :, pl.ds(p * PAGE_SIZE, PAGE_SIZE)],
                                  sem_ref.at[slot]) for p in range(n_pages)]


def _idx_sample_body(pt_ref, iq_ref, iw_ref, ikn_ref, pool_ref, bias_ref, kbuf, sem, sc_ref,
                     *, layer, n_seq, n_pages, k_sel):
    b = pl.program_id(0)
    past = n_pages * PAGE_SIZE
    slot = b % 2

    @pl.when(b == 0)
    def _():
        sc_ref[...] = jnp.zeros(sc_ref.shape, F32)
        for cp in _page_copies(pool_ref, kbuf, sem, pt_ref, layer, 0, 0, n_pages, 0):
            cp.start()

    @pl.when(b + 1 < n_seq)
    def _():
        for cp in _page_copies(pool_ref, kbuf, sem, pt_ref, layer, b + 1, 0, n_pages, 1 - slot):
            cp.start()

    for cp in _page_copies(pool_ref, kbuf, sem, pt_ref, layer, b, 0, n_pages, slot):
        cp.wait()

    iq = iq_ref[0]
    iw = iw_ref[0]
    s = _dot(iq, kbuf[slot].astype(BF))
    row = jnp.sum(jnp.maximum(s, 0.0) * iw, axis=0, keepdims=True)
    ikn = ikn_ref[0].astype(BF).astype(F32)
    s_new = jnp.sum(iq.astype(F32) * ikn, axis=1, keepdims=True)
    r_new = jnp.sum(jnp.maximum(s_new, 0.0) * iw, axis=0, keepdims=True)
    lane = lax.broadcasted_iota(I32, (1, LANES), 1)
    tail = jnp.where(lane == 0, r_new, -jnp.inf)
    is_b = lax.broadcasted_iota(I32, (n_seq, 1), 0) == b
    sc_ref[:, :past] = jnp.where(is_b, row, sc_ref[:, :past])
    sc_ref[:, past:] = jnp.where(is_b, tail, sc_ref[:, past:])

    @pl.when(b == n_seq - 1)
    def _():
        def read(c):
            return sc_ref[:, pl.ds(pl.multiple_of(c * LANES, LANES), LANES)]

        def write(c, v):
            bias_ref[:, pl.ds(pl.multiple_of(c * LANES, LANES), LANES)] = v

        def count(t):
            return jnp.sum(jnp.where(sc_ref[...] >= t, 1, 0).astype(I32), axis=1, keepdims=True)

        _select_bias(count, read, write, past // LANES + 1, LANES, k_sel, (n_seq, 1), 1)


def _idx_sample(page_table, iq3, iw3, ikn3, pool_t, layer):
    n_seq, n_pages = page_table.shape
    past = n_pages * PAGE_SIZE
    width = past + LANES
    k_sel = min(TOPK_MAX, (past + 1) // 4)
    grid_spec = pltpu.PrefetchScalarGridSpec(
        num_scalar_prefetch=1,
        grid=(n_seq,),
        in_specs=[pl.BlockSpec((1, IDX_HEADS, IDX_DIM), lambda b, pt: (b, 0, 0)),
                  pl.BlockSpec((1, IDX_HEADS, 1), lambda b, pt: (b, 0, 0)),
                  pl.BlockSpec((1, 1, IDX_DIM), lambda b, pt: (b, 0, 0)),
                  pl.BlockSpec(memory_space=pl.ANY)],
        out_specs=pl.BlockSpec((n_seq, width), lambda b, pt: (0, 0)),
        scratch_shapes=[pltpu.VMEM((2, IDX_DIM, past), F32),
                        pltpu.SemaphoreType.DMA((2,)),
                        pltpu.VMEM((n_seq, width), F32)])
    return pl.pallas_call(
        functools.partial(_idx_sample_body, layer=layer, n_seq=n_seq, n_pages=n_pages, k_sel=k_sel),
        grid_spec=grid_spec,
        out_shape=jax.ShapeDtypeStruct((n_seq, width), F32),
        compiler_params=_cp(1),
        name="idx_sample",
    )(page_table, iq3, iw3, ikn3, pool_t)


def _attn_a_sample_body(pt_ref, q_ref, bias_ref, kvn_ref, pool_ref, o_ref, kvbuf, sem, m_s, l_s, acc_s,
                        *, layer, n_seq, n_steps, pg):
    b, s = pl.program_id(0), pl.program_id(1)
    t = b * n_steps + s
    slot = t % 2
    keys = pg * PAGE_SIZE
    past = n_steps * keys

    @pl.when(t == 0)
    def _():
        for cp in _page_copies(pool_ref, kvbuf, sem, pt_ref, layer, 0, 0, pg, 0):
            cp.start()

    @pl.when(t + 1 < n_seq * n_steps)
    def _():
        wrap = s + 1 == n_steps
        nb = jnp.where(wrap, b + 1, b)
        ns = jnp.where(wrap, 0, s + 1)
        for cp in _page_copies(pool_ref, kvbuf, sem, pt_ref, layer, nb, ns * pg, pg, 1 - slot):
            cp.start()

    for cp in _page_copies(pool_ref, kvbuf, sem, pt_ref, layer, b, s * pg, pg, slot):
        cp.wait()

    @pl.when(s == 0)
    def _():
        m_s[...] = jnp.full(m_s.shape, M_INIT, F32)
        l_s[...] = jnp.zeros(l_s.shape, F32)
        acc_s[...] = jnp.zeros(acc_s.shape, F32)

    qb = q_ref[0]
    kv = kvbuf[slot]
    bias = bias_ref[0, :, pl.ds(pl.multiple_of(s * keys, keys), keys)]
    sc = _dot(qb, kv[:KV_W].astype(BF)) + bias
    m_old = m_s[...]
    m_new = jnp.maximum(m_old, jnp.max(sc, axis=1, keepdims=True))
    alpha = jnp.exp(m_old - m_new)
    p = jnp.exp(sc - m_new)
    l_new = alpha * l_s[...] + jnp.sum(p, axis=1, keepdims=True)
    acc_new = alpha * acc_s[...] + _dot_nt(p.astype(BF), kv[KV_W:].astype(BF))
    m_s[...] = m_new
    l_s[...] = l_new
    acc_s[...] = acc_new

    @pl.when(s == n_steps - 1)
    def _():
        kn = kvn_ref[0].astype(BF).astype(F32)
        b_new = bias_ref[0, :, past:past + LANES][:, 0:1]
        s_new = jnp.sum(qb.astype(F32) * kn[:, :KV_W], axis=1, keepdims=True) + b_new
        m_fin = jnp.maximum(m_new, s_new)
        a_fin = jnp.exp(m_new - m_fin)
        p_new = jnp.exp(s_new - m_fin)
        l_fin = a_fin * l_new + p_new
        o_ref[0] = (a_fin * acc_new + p_new.astype(BF).astype(F32) * kn[:, KV_W:]) / l_fin


def _attn_a_sample(page_table, qblk, bias3, kvn3, pool_t, layer):
    n_seq, n_pages = page_table.shape
    pg = min(PAGES_PER_STEP, n_pages)
    assert n_pages % pg == 0
    n_steps = n_pages // pg
    width = bias3.shape[2]
    grid_spec = pltpu.PrefetchScalarGridSpec(
        num_scalar_prefetch=1,
        grid=(n_seq, n_steps),
        in_specs=[pl.BlockSpec((1, _SAMPLE_A_ROWS, KV_W), lambda b, s, pt: (b, 0, 0)),
                  pl.BlockSpec((1, 1, width), lambda b, s, pt: (b, 0, 0)),
                  pl.BlockSpec((1, 1, 2 * KV_W), lambda b, s, pt: (b, 0, 0)),
                  pl.BlockSpec(memory_space=pl.ANY)],
        out_specs=pl.BlockSpec((1, _SAMPLE_A_ROWS, KV_W), lambda b, s, pt: (b, 0, 0)),
        scratch_shapes=[pltpu.VMEM((2, 2 * KV_W, pg * PAGE_SIZE), F32),
                        pltpu.SemaphoreType.DMA((2,)),
                        pltpu.VMEM((_SAMPLE_A_ROWS, 1), F32),
                        pltpu.VMEM((_SAMPLE_A_ROWS, 1), F32),
                        pltpu.VMEM((_SAMPLE_A_ROWS, KV_W), F32)])
    return pl.pallas_call(
        functools.partial(_attn_a_sample_body, layer=layer, n_seq=n_seq, n_steps=n_steps, pg=pg),
        grid_spec=grid_spec,
        out_shape=jax.ShapeDtypeStruct((n_seq, _SAMPLE_A_ROWS, KV_W), F32),
        compiler_params=_cp(2),
        name="attn_a_sample",
    )(page_table, qblk, bias3, kvn3, pool_t)


def _attn_b_sample_body(*refs, n_seq, n_alias, layer, n_fill):
    ng = len(B_GROUPS)
    nss = refs[2 + 3 * ng + n_alias:]
    if n_fill == 1:
        _attn_b_sample_compute(refs, n_alias)
        return
    fill = pl.program_id(1)

    @pl.when(fill == layer)
    def _():
        _attn_b_sample_compute(refs, n_alias)

    @pl.when(fill != layer)
    def _():
        for ns_ref in nss:
            ns_ref[...] = jnp.zeros(ns_ref.shape, F32)


def _attn_b_sample_compute(refs, n_alias):
    ng = len(B_GROUPS)
    q_ref = refs[0]
    sts, kns, kcs = refs[1:1 + ng], refs[1 + ng:1 + 2 * ng], refs[1 + 2 * ng:1 + 3 * ng]
    o_ref = refs[1 + 3 * ng + n_alias]
    nss = refs[2 + 3 * ng + n_alias:]
    b = pl.program_id(0)
    res = []
    groups = tuple(zip(sts, kns, kcs, nss))
    for gi, (st_ref, kn_ref, kc_ref, ns_ref) in enumerate(groups):
        window, dil = B_GROUPS[gi]
        q = q_ref[0, gi * _SAMPLE_ROWS:(gi + 1) * _SAMPLE_ROWS, :]
        st = st_ref[...]
        kn = kn_ref[0].astype(BF).astype(F32)
        s = _dot(q, st[:KV_W].astype(BF))
        if dil > 1:
            lane = lax.broadcasted_iota(I32, s.shape, 1)
            s = jnp.where((lane & (dil - 1)) == 0, s, -jnp.inf)
        s_new = jnp.sum(q.astype(F32) * kn[:, :KV_W], axis=1, keepdims=True)
        mx = jnp.maximum(jnp.max(s, axis=1, keepdims=True), s_new)
        e = jnp.exp(s - mx)
        e_new = jnp.exp(s_new - mx)
        den = jnp.sum(e, axis=1, keepdims=True) + e_new
        o = (_dot_nt(e.astype(BF), st[KV_W:].astype(BF)) + e_new.astype(BF).astype(F32) * kn[:, KV_W:]) / den
        res.append((o, mx, den))
        kc = kc_ref[...]
        col = jnp.sum(jnp.where(lax.broadcasted_iota(I32, kc.shape, 1) == b, kc, 0.0), axis=1, keepdims=True)
        last = lax.broadcasted_iota(I32, st.shape, 1) == window - 1
        ns_ref[...] = jnp.where(last, col, pltpu.roll(st, window - 1, 1))
    mmax = jnp.maximum(jnp.maximum(res[0][1], res[1][1]), res[2][1])
    wgt = [r[2] * jnp.exp(r[1] - mmax) for r in res]
    num = wgt[0] * res[0][0] + wgt[1] * res[1][0] + wgt[2] * res[2][0]
    o_ref[0] = num / (wgt[0] + wgt[1] + wgt[2])


def _attn_b_sample(qblk3, states_t, kvns, kvcs, layer, prev_new_states):
    n_seq = qblk3.shape[0]
    ng = len(B_GROUPS)
    n_fill = 1 if prev_new_states else states_t[0].shape[0]
    in_specs = [pl.BlockSpec((1, ng * _SAMPLE_ROWS, KV_W), lambda b, f: (b, 0, 0))]
    out_specs = [pl.BlockSpec((1, _SAMPLE_ROWS, KV_W), lambda b, f: (b, 0, 0))]
    out_shapes = [jax.ShapeDtypeStruct((n_seq, _SAMPLE_ROWS, KV_W), F32)]
    out_map = (lambda b, f: (layer, b, 0, 0)) if n_fill == 1 else (lambda b, f: (f, b, 0, 0))
    for st, (window, _) in zip(states_t, B_GROUPS):
        assert st.shape[2:] == (2 * KV_W, window)
        in_specs.append(pl.BlockSpec((None, None, 2 * KV_W, window), lambda b, f: (layer, b, 0, 0)))
        out_specs.append(pl.BlockSpec((None, None, 2 * KV_W, window), out_map))
        out_shapes.append(jax.ShapeDtypeStruct(st.shape, F32))
    in_specs += [pl.BlockSpec((1, 1, 2 * KV_W), lambda b, f: (b, 0, 0))] * ng
    in_specs += [pl.BlockSpec((2 * KV_W, n_seq), lambda b, f: (0, 0))] * ng
    aliases = {}
    if prev_new_states:
        first = len(in_specs)
        in_specs += [pl.BlockSpec(memory_space=pl.ANY)] * ng
        aliases = {first + gi: 1 + gi for gi in range(ng)}
    outs = pl.pallas_call(
        functools.partial(_attn_b_sample_body, n_seq=n_seq, n_alias=len(aliases), layer=layer, n_fill=n_fill),
        grid=(n_seq, n_fill),
        in_specs=in_specs,
        out_specs=out_specs,
        out_shape=out_shapes,
        input_output_aliases=aliases,
        compiler_params=_cp(2),
        name="attn_b_sample",
    )(qblk3, *states_t, *kvns, *kvcs, *prev_new_states)
    return outs[0], outs[1:]


def _memattn_sample_body(q_ref, mem_ref, o_ref):
    q = q_ref[0]
    mem = mem_ref[...].astype(BF)
    s = _dot(q, mem[:KV_W])
    e = jnp.exp(s - jnp.max(s, axis=1, keepdims=True))
    p = e / jnp.sum(e, axis=1, keepdims=True)
    o_ref[0] = _dot_nt(p.astype(BF), mem[KV_W:])


def _memattn_sample(mq3, mem_t, layer):
    n_seq, mem_len = mem_t.shape[1], mem_t.shape[3]
    return pl.pallas_call(
        _memattn_sample_body,
        grid=(n_seq,),
        in_specs=[pl.BlockSpec((1, _SAMPLE_ROWS, KV_W), lambda b: (b, 0, 0)),
                  pl.BlockSpec((None, None, 2 * KV_W, mem_len), lambda b: (layer, b, 0, 0))],
        out_specs=pl.BlockSpec((1, _SAMPLE_ROWS, KV_W), lambda b: (b, 0, 0)),
        out_shape=jax.ShapeDtypeStruct((n_seq, _SAMPLE_ROWS, KV_W), F32),
        compiler_params=_cp(1),
        name="memattn_sample",
    )(mq3, mem_t)


def _feature_major(x):
    nd = x.ndim
    xt = jnp.transpose(x, tuple(range(nd - 4)) + (nd - 3, nd - 2, nd - 1, nd - 4))
    return xt.reshape(x.shape[:nd - 4] + (2 * KV_W, x.shape[nd - 4]))


def _token_major(x, heads):
    nd = x.ndim
    xr = x.reshape(x.shape[:nd - 2] + (2, heads, HEAD_DIM, x.shape[nd - 1]))
    return jnp.transpose(xr, tuple(range(nd - 2)) + (nd + 1, nd - 2, nd - 1, nd))


def kernel(x_prompt, x_sample, cache_kv_a, cache_kidx_a, state_kv_b1, state_kv_b2, state_kv_b3, cache_mem_kv, page_table, mem_prompt, ffn_norm, w_ffn_in, w_ffn_out, mix_norm, mem_norm, w_mem_kv, w_in_a, w_out_a, w_in_b, w_out_b, final_norm):
    batch, seq, d = x_prompt.shape
    n_seq = x_sample.shape[0]
    depth = ffn_norm.shape[0]
    n_pages = page_table.shape[1]
    past = n_pages * PAGE_SIZE
    mem_len = mem_prompt.shape[1]
    assert x_sample.shape[1] == 1 and seq % TM == 0
    ng = len(B_GROUPS)

    yp = x_prompt.reshape(batch * seq, d)
    ys = x_sample.reshape(n_seq, d)
    tab_p = _rope_tables(jnp.arange(seq))
    tab_s = _rope_tables(jnp.full((n_seq,), past, I32))
    w_in_bf = w_ffn_in.astype(BF)
    w_out_bf = w_ffn_out.astype(BF)
    mem_flat = mem_prompt.reshape(batch * mem_len, d)
    kv_pool_t = _feature_major(cache_kv_a)
    kidx_pool_t = jnp.swapaxes(cache_kidx_a, 2, 3)
    states_t = [_feature_major(s) for s in (state_kv_b1, state_kv_b2, state_kv_b3)]
    mem_cache_t = _feature_major(cache_mem_kv)

    kv_a_p, kidx_a_p, kv_a_s, kidx_a_s = [], [], [], []
    b_p = ([], [], [])
    new_states = []
    mem_t = _memproj(mem_flat, mem_norm, w_mem_kv, batch)
    for i in range(depth):
        j = i // 2
        yp = _ffn(yp, ffn_norm[i, 0], w_in_bf, w_out_bf, i, 0, TM)
        ys = _ffn(ys, ffn_norm[i, 0], w_in_bf, w_out_bf, i, 0, n_seq)
        if i % 2 == 0:
            w = w_in_a[j]
            q_al, k_bf, iq_al, ik128, mq_al, kv_t, v_t, ik_t, iw_t = _proj_a_prompt(yp, mix_norm[i], w, tab_p, batch)
            o = _attn_a_prompt(q_al, iq_al, iw_t, k_bf, ik128, v_t, batch, seq)
            om = _memattn_prompt(mq_al, mem_t, i, seq, TM)
            n_o = A_HEADS * HEAD_DIM
            mix_p = ([], [o, om], [w_out_a[j][:n_o], w_out_a[j][n_o:]])
            kv_a_p.append(kv_t)
            kidx_a_p.append(ik_t)

            qblk, kvn, iq_s, ikn, iw_s, mq_blk = _proj_a_sample(ys, mix_norm[i], w, tab_s)
            bias = _idx_sample(page_table, iq_s.reshape(n_seq, IDX_HEADS, IDX_DIM),
                               iw_s[:, :IDX_HEADS].reshape(n_seq, IDX_HEADS, 1),
                               ikn.reshape(n_seq, 1, IDX_DIM), kidx_pool_t, j)
            o_s = _attn_a_sample(page_table, qblk.reshape(n_seq, _SAMPLE_A_ROWS, KV_W),
                                 bias.reshape(n_seq, 1, past + LANES), kvn.reshape(n_seq, 1, 2 * KV_W),
                                 kv_pool_t, j)
            om_s = _memattn_sample(mq_blk.reshape(n_seq, _SAMPLE_ROWS, KV_W), mem_cache_t, i)
            mix_s = ([], [o_s.reshape(n_seq, -1), om_s.reshape(n_seq, -1)],
                     [_expand_out(w_out_a[j][:n_o], _A_Q_SLOT, _SAMPLE_A_ROWS),
                      _expand_out(w_out_a[j][n_o:], _MEM_SLOT, _SAMPLE_ROWS)])
            kv_a_s.append(kvn.reshape(n_seq, 1, 2, A_KV_HEADS, HEAD_DIM))
            kidx_a_s.append(ikn.reshape(n_seq, 1, IDX_DIM))
        else:
            w = w_in_b[j]
            outs = _proj_b_prompt(yp, mix_norm[i], w, tab_p, batch)
            stats_o, stats_m, stats_d = [], [], []
            for gi, (window, dil) in enumerate(B_GROUPS):
                og, mg, dg = _attn_b_prompt(outs[2 * gi], outs[2 * gi + 1], batch, seq, window, dil)
                stats_o.append(og)
                stats_m.append(mg)
                stats_d.append(dg)
                b_p[gi].append(outs[2 * ng + 1 + gi][:, :, seq - min(window, seq):])
            om = _memattn_prompt(outs[2 * ng], mem_t, i, seq, TM)
            n_o = B_HEADS * HEAD_DIM
            mix_p = (stats_o + stats_m + stats_d, [om], [w_out_b[j][:n_o], w_out_b[j][n_o:]])

            souts = _proj_b_sample(ys, mix_norm[i], w, tab_s)
            kvns = [souts[1 + gi].reshape(n_seq, 1, 2 * KV_W) for gi in range(ng)]
            kvcs = [souts[2 + ng + gi] for gi in range(ng)]
            o_s, new_states = _attn_b_sample(souts[0].reshape(n_seq, ng * _SAMPLE_ROWS, KV_W),
                                             states_t, kvns, kvcs, j, new_states)
            om_s = _memattn_sample(souts[1 + ng].reshape(n_seq, _SAMPLE_ROWS, KV_W), mem_cache_t, i)
            mix_s = ([], [o_s.reshape(n_seq, -1), om_s.reshape(n_seq, -1)],
                     [_expand_out(w_out_b[j][:n_o], _B_SLOT, _SAMPLE_ROWS),
                      _expand_out(w_out_b[j][n_o:], _MEM_SLOT, _SAMPLE_ROWS)])
        closing = final_norm if i == depth - 1 else None
        yp = _ffn(yp, ffn_norm[i, 1], w_in_bf, w_out_bf, i, 1, TM, mix_p, closing)
        ys = _ffn(ys, ffn_norm[i, 1], w_in_bf, w_out_bf, i, 1, n_seq, mix_s, closing)

    y_prompt = yp.reshape(batch, seq, d)
    y_sample = ys.reshape(n_seq, 1, d)
    return (y_prompt, y_sample,
            _token_major(jnp.stack(kv_a_p), A_KV_HEADS), jnp.swapaxes(jnp.stack(kidx_a_p), 2, 3),
            _token_major(jnp.stack(b_p[0]), B_HEADS), _token_major(jnp.stack(b_p[1]), B_HEADS),
            _token_major(jnp.stack(b_p[2]), B_HEADS), _token_major(mem_t, MEM_HEADS),
            jnp.stack(kv_a_s), jnp.stack(kidx_a_s),
            _token_major(new_states[0], B_HEADS), _token_major(new_states[1], B_HEADS),
            _token_major(new_states[2], B_HEADS))
```
